```python
import jax, jax.numpy as jnp
from jax import lax
import numpy as np

D_MODEL = 1024
BATCH = 8
SEQ = 2048
DEPTH = 2
DEC_BATCH = 128
DEC_SEQ = 8
PAST_LEN = 16384
PAGE_SIZE = 128

N_META = 16
N_EVEN = (DEPTH + 1) // 2
N_ODD = DEPTH // 2
H_A = 4
DK_A = D_MODEL // (2 * H_A)
DV_A = D_MODEL // (2 * H_A)
H_B = 4
DK_B = D_MODEL // (2 * H_B)
DV_B = D_MODEL // (2 * H_B)
DK_C = 128
H_C = D_MODEL // DK_C
DV_C = D_MODEL // H_C
D_FF = 2816
CONV_W = 3
CHUNK_A = 128
CHUNK_B = 128
CHUNK_C = 32
ROPE_BASE = 10000.0
EPS = 1e-6
F_BIAS_LO = 3.0
F_BIAS_HI = 6.0
AB_SIZES = (H_A * DK_A, H_A * DK_A, H_A * DV_A, H_A * DV_A, H_A, H_A, H_B * DK_B, H_B * DK_B, H_B * DV_B, H_B * DV_B)
C_SIZES = (H_C * DK_C, H_C * DK_C, H_C * DV_C, H_C * DV_C)
MIX_AB = H_A * DV_A + H_B * DV_B

kernel_name = 'hybrid_mlstm_retention_hgrn2_convffn_step'


def split_cols(z, sizes):
    out, start = [], 0
    for s in sizes:
        out.append(z[..., start:start + s])
        start += s
    return out


def rms_norm(x, gain):
    xf = x.astype(jnp.float32)
    y = xf * lax.rsqrt(jnp.mean(xf * xf, axis=-1, keepdims=True) + EPS)
    return (y * gain.astype(jnp.float32)).astype(x.dtype)


def head_layer_norm(h, gain):
    mu = jnp.mean(h, axis=-1, keepdims=True)
    c = h - mu
    y = c * lax.rsqrt(jnp.mean(c * c, axis=-1, keepdims=True) + EPS)
    return y.reshape(h.shape[:2] + (-1,)) * gain.astype(jnp.float32)


def head_rms_norm(h, gain):
    y = h * lax.rsqrt(jnp.mean(h * h, axis=-1, keepdims=True) + EPS)
    return y.reshape(h.shape[:2] + (-1,)) * gain.astype(jnp.float32)


def rotary(x, pos):
    half = x.shape[-1] // 2
    inv = 1.0 / (ROPE_BASE ** jnp.linspace(0.0, 1.0, half, dtype=jnp.float32))
    ang = pos[:, None] * inv[None, :]
    cos = jnp.cos(ang)[None, :, None, :]
    sin = jnp.sin(ang)[None, :, None, :]
    x1, x2 = x[..., :half], x[..., half:]
    return jnp.concatenate([x1 * cos - x2 * sin, x2 * cos + x1 * sin], axis=-1)


def run_blocks(step, state, xs, chunk, prompt):
    if not prompt:
        return step(state, *xs)
    state, y_meta = step(state, *(a[:, :N_META] for a in xs))

    def to_chunks(a):
        a = a[:, N_META:]
        bsz, t = a.shape[:2]
        return jnp.moveaxis(a.reshape((bsz, t // chunk, chunk) + a.shape[2:]), 1, 0)

    state, ys = lax.scan(lambda s, c: step(s, *c), state, tuple(to_chunks(a) for a in xs))
    ys = jnp.moveaxis(ys, 0, 1)
    ys = ys.reshape((ys.shape[0], ys.shape[1] * ys.shape[2]) + ys.shape[3:])
    return state, jnp.concatenate([y_meta, ys], axis=1)


def mlstm_block(state, q, k, v, li, lf):
    c0, n0, m0 = state
    L = q.shape[1]
    causal = jnp.tril(jnp.ones((L, L), dtype=bool))
    b = jnp.cumsum(lf, axis=1)
    log_d = b[:, :, None, :] - b[:, None, :, :] + li[:, None, :, :]
    log_d = jnp.where(causal[None, :, :, None], log_d, -jnp.inf)
    log_inter = b + m0[:, None, :]
    m = jnp.maximum(log_inter, jnp.max(log_d, axis=2))
    d = jnp.exp(log_d - m[:, :, None, :])
    inter = jnp.exp(log_inter - m)
    s = jnp.einsum('bthd,bshd->btsh', q, k) * d
    num = jnp.einsum('btsh,bshv->bthv', s, v) + inter[..., None] * jnp.einsum('bthd,bhdv->bthv', q, c0)
    den = jnp.sum(s, axis=2) + inter * jnp.einsum('bthd,bhd->bth', q, n0)
    h = num / jnp.maximum(jnp.abs(den), jnp.exp(-m))[..., None]
    m_new = m[:, -1]
    w = jnp.exp(li + b[:, -1:, :] - b - m_new[:, None, :])
    carry = jnp.exp(log_inter[:, -1] - m_new)
    c_new = carry[..., None, None] * c0 + jnp.einsum('bsh,bshd,bshv->bhdv', w, k, v)
    n_new = carry[..., None] * n0 + jnp.einsum('bsh,bshd->bhd', w, k)
    return (c_new, n_new, m_new), h


def retention_block(s0, q, k, v, log_gamma):
    L = q.shape[1]
    idx = jnp.arange(L, dtype=jnp.float32)
    causal = idx[:, None] >= idx[None, :]
    rel = jnp.where(causal, idx[:, None] - idx[None, :], 0.0)
    decay = jnp.where(causal[..., None], jnp.exp(rel[..., None] * log_gamma), 0.0)
    s = jnp.einsum('bthd,bshd->btsh', q, k) * decay[None]
    inner = jnp.exp((idx + 1.0)[:, None] * log_gamma)
    o = jnp.einsum('btsh,bshv->bthv', s, v) + jnp.einsum('bthd,bhdv->bthv', q, s0) * inner[None, :, :, None]
    tail = jnp.exp((L - 1.0 - idx)[:, None] * log_gamma)
    s_new = jnp.exp(L * log_gamma)[None, :, None, None] * s0 + jnp.einsum('bshd,bshv,sh->bhdv', k, v, tail)
    return s_new, o


def hgrn_block(s0, q, k, v, log_f):
    L = q.shape[1]
    causal = jnp.tril(jnp.ones((L, L), dtype=bool))
    b = jnp.cumsum(log_f, axis=1)
    diff = b[:, :, None] - b[:, None, :]
    decay = jnp.exp(jnp.where(causal[None, :, :, None, None], diff, -jnp.inf))
    a = jnp.einsum('bthc,bshc,btshc->btsh', q, k, decay)
    o = jnp.einsum('btsh,bshv->bthv', a, v) + jnp.einsum('bthc,bhcv->bthv', q * jnp.exp(b), s0)
    b_last = b[:, -1]
    s_new = jnp.exp(b_last)[..., None] * s0 + jnp.einsum('bshc,bshv->bhcv', k * jnp.exp(b_last[:, None] - b), v)
    return s_new, o


def ab_mixer(xn, pos, prompt, w_in, b_i, b_f, g_a, g_b, w_out, state_a, state_b):
    bsz, t = xn.shape[:2]
    q_a, k_a, v_a, o_a, i_a, f_a, q_b, k_b, v_b, g_gate = split_cols(xn @ w_in, AB_SIZES)
    heads = lambda a, h: a.astype(jnp.float32).reshape(bsz, t, h, -1)
    li = i_a.astype(jnp.float32) + b_i.astype(jnp.float32)
    lf = jax.nn.log_sigmoid(f_a.astype(jnp.float32) + b_f.astype(jnp.float32))
    xs_a = (heads(q_a, H_A), heads(k_a, H_A) * DK_A ** -0.5, heads(v_a, H_A), li, lf)
    state_a, h_a = run_blocks(mlstm_block, state_a, xs_a, CHUNK_A, prompt)
    h_a = jax.nn.sigmoid(o_a.astype(jnp.float32)) * head_layer_norm(h_a, g_a)
    log_gamma = jnp.log1p(-jnp.exp2(-5.0 - jnp.arange(H_B, dtype=jnp.float32)))
    ret_step = lambda s, q, k, v: retention_block(s, q, k, v, log_gamma)
    xs_b = (rotary(heads(q_b, H_B), pos), rotary(heads(k_b, H_B), pos) * DK_B ** -0.5, heads(v_b, H_B))
    state_b, h_b = run_blocks(ret_step, state_b, xs_b, CHUNK_B, prompt)
    h_b = jax.nn.silu(g_gate.astype(jnp.float32)) * head_layer_norm(h_b, g_b)
    mixed = jnp.concatenate([h_a, h_b], axis=-1).astype(xn.dtype)
    return mixed @ w_out, state_a, state_b


def hgrn_lower_bound(lb_logits, layer):
    p = jax.nn.softmax(lb_logits.astype(jnp.float32), axis=0)
    return jnp.cumsum(p, axis=0)[layer] - p[0]


def hgrn_mixer(xn, prompt, w_in, lb, gain, w_out, s0):
    bsz, t = xn.shape[:2]
    q, f, i, g = split_cols(xn @ w_in, C_SIZES)
    f = f.astype(jnp.float32)
    log_f = jnp.log(lb + (1.0 - lb) * jax.nn.sigmoid(f))
    k = (1.0 - lb) * jax.nn.sigmoid(-f)
    heads = lambda a, h: a.astype(jnp.float32).reshape(bsz, t, h, -1)
    xs = (heads(q, H_C), heads(k, H_C), heads(i, H_C), heads(log_f, H_C))
    s_new, o = run_blocks(hgrn_block, s0, xs, CHUNK_C, prompt)
    o = head_rms_norm(o, gain) * jax.nn.silu(g.astype(jnp.float32))
    return o.astype(xn.dtype) @ w_out, s_new


def conv_ffn(xn, w_in, conv_w, conv_b, w_out, buf):
    t = xn.shape[1]
    u, gate = jnp.split(xn @ w_in, 2, axis=-1)
    padded = jnp.concatenate([buf.astype(u.dtype), u], axis=1)
    cw = conv_w.astype(jnp.float32)
    conv = conv_b.astype(jnp.float32) + sum(padded[:, j:j + t].astype(jnp.float32) * cw[j] for j in range(CONV_W))
    h = (jax.nn.silu(conv) * gate.astype(jnp.float32)).astype(xn.dtype)
    return h @ w_out, padded[:, t:]


def trunk(x, pos, prompt, st_c, st_n, st_m, st_r, st_s, st_conv,
          norm_mix, w_in_ab, b_igate, b_fgate, gn_mlstm, gn_ret, w_out_ab,
          lb_logits, w_in_c, gn_hgrn, w_out_c,
          norm_ffn, w_ffn_in, conv_w, conv_b, w_ffn_out, norm_final):
    f32 = jnp.float32
    new_c, new_n, new_m, new_r, new_s, new_conv = [], [], [], [], [], []
    for layer in range(DEPTH):
        j = layer // 2
        xn = rms_norm(x, norm_mix[layer])
        if layer % 2 == 0:
            out, (c, n, m), r = ab_mixer(xn, pos, prompt, w_in_ab[j], b_igate[j], b_fgate[j], gn_mlstm[j], gn_ret[j], w_out_ab[j],
                                         (st_c[j].astype(f32), st_n[j].astype(f32), st_m[j].astype(f32)), st_r[j].astype(f32))
            new_c.append(c)
            new_n.append(n)
            new_m.append(m)
            new_r.append(r)
        else:
            lb = hgrn_lower_bound(lb_logits, layer)
            out, s = hgrn_mixer(xn, prompt, w_in_c[j], lb, gn_hgrn[j], w_out_c[j], st_s[j].astype(f32))
            new_s.append(s)
        x = x + out.astype(x.dtype)
        h, buf = conv_ffn(rms_norm(x, norm_ffn[layer]), w_ffn_in[layer], conv_w[layer], conv_b[layer], w_ffn_out[layer], st_conv[layer])
        x = x + h.astype(x.dtype)
        new_conv.append(buf)
    y = rms_norm(x, norm_final)
    return y, jnp.stack(new_c), jnp.stack(new_n), jnp.stack(new_m), jnp.stack(new_r), jnp.stack(new_s), jnp.stack(new_conv)


def setup_inputs(seed: int = 0) -> dict:
    key = jax.random.key(seed)
    ks = jax.random.split(key, 32)
    nrm = lambda k, shape, scale: scale * jax.random.normal(k, shape, jnp.float32)
    ab_width = sum(AB_SIZES)
    c_width = sum(C_SIZES)
    return {
        'x_prompt': nrm(ks[0], (BATCH, SEQ, D_MODEL), 1.0),
        'x_sample': nrm(ks[1], (DEC_BATCH, DEC_SEQ, D_MODEL), 1.0),
        'state_mlstm_C': nrm(ks[2], (N_EVEN, DEC_BATCH, H_A, DK_A, DV_A), 0.1),
        'state_mlstm_n': nrm(ks[3], (N_EVEN, DEC_BATCH, H_A, DK_A), 0.1),
        'state_mlstm_m': nrm(ks[4], (N_EVEN, DEC_BATCH, H_A), 1.0),
        'state_ret_S': nrm(ks[5], (N_EVEN, DEC_BATCH, H_B, DK_B, DV_B), 0.5),
        'state_hgrn_S': nrm(ks[6], (N_ODD, DEC_BATCH, H_C, DK_C, DV_C), 0.5),
        'state_ffn_conv': nrm(ks[7], (DEPTH, DEC_BATCH, CONV_W - 1, D_FF), 1.0),
        'meta_tokens': nrm(ks[8], (N_META, D_MODEL), 1.0),
        'norm_mix': 1.0 + nrm(ks[9], (DEPTH, D_MODEL), 0.02),
        'w_in_ab': nrm(ks[10], (N_EVEN, D_MODEL, ab_width), D_MODEL ** -0.5),
        'b_igate': nrm(ks[11], (N_EVEN, H_A), 0.1),
        'b_fgate': jnp.linspace(F_BIAS_LO, F_BIAS_HI, H_A, dtype=jnp.float32)[None, :] + nrm(ks[12], (N_EVEN, H_A), 0.1),
        'gn_mlstm': 1.0 + nrm(ks[13], (N_EVEN, H_A * DV_A), 0.02),
        'gn_ret': 1.0 + nrm(ks[14], (N_EVEN, H_B * DV_B), 0.02),
        'w_out_ab': nrm(ks[15], (N_EVEN, MIX_AB, D_MODEL), MIX_AB ** -0.5),
        'lb_logits': nrm(ks[16], (DEPTH, H_C * DK_C), 1.0),
        'w_in_c': nrm(ks[17], (N_ODD, D_MODEL, c_width), D_MODEL ** -0.5),
        'gn_hgrn': 1.0 + nrm(ks[18], (N_ODD, H_C * DV_C), 0.02),
        'w_out_c': nrm(ks[19], (N_ODD, H_C * DV_C, D_MODEL), (H_C * DV_C) ** -0.5),
        'norm_ffn': 1.0 + nrm(ks[20], (DEPTH, D_MODEL), 0.02),
        'w_ffn_in': nrm(ks[21], (DEPTH, D_MODEL, 2 * D_FF), D_MODEL ** -0.5),
        'conv_w': nrm(ks[22], (DEPTH, CONV_W, D_FF), CONV_W ** -0.5),
        'conv_b': nrm(ks[23], (DEPTH, D_FF), 0.02),
        'w_ffn_out': nrm(ks[24], (DEPTH, D_FF, D_MODEL), D_FF ** -0.5),
        'norm_final': 1.0 + nrm(ks[25], (D_MODEL,), 0.02),
    }


def reference(x_prompt, x_sample, state_mlstm_C, state_mlstm_n, state_mlstm_m, state_ret_S, state_hgrn_S, state_ffn_conv,
              meta_tokens, norm_mix, w_in_ab, b_igate, b_fgate, gn_mlstm, gn_ret, w_out_ab,
              lb_logits, w_in_c, gn_hgrn, w_out_c, norm_ffn, w_ffn_in, conv_w, conv_b, w_ffn_out, norm_final):
    f32 = jnp.float32
    bp = x_prompt.shape[0]
    meta = jnp.broadcast_to(meta_tokens.astype(x_prompt.dtype)[None], (bp, N_META, D_MODEL))
    xp = jnp.concatenate([meta, x_prompt], axis=1)
    pos_p = jnp.arange(xp.shape[1], dtype=f32)
    pos_s = PAST_LEN + jnp.arange(x_sample.shape[1], dtype=f32)

    yp, cp, n_p, mp, rp, sp, convp = trunk(
        xp, pos_p, True,
        jnp.zeros((N_EVEN, bp, H_A, DK_A, DV_A), f32), jnp.zeros((N_EVEN, bp, H_A, DK_A), f32),
        jnp.zeros((N_EVEN, bp, H_A), f32), jnp.zeros((N_EVEN, bp, H_B, DK_B, DV_B), f32),
        jnp.zeros((N_ODD, bp, H_C, DK_C, DV_C), f32), jnp.zeros((DEPTH, bp, CONV_W - 1, D_FF), x_prompt.dtype),
        norm_mix, w_in_ab, b_igate, b_fgate, gn_mlstm, gn_ret, w_out_ab,
        lb_logits, w_in_c, gn_hgrn, w_out_c, norm_ffn, w_ffn_in, conv_w, conv_b, w_ffn_out, norm_final)

    ys, cs, n_s, ms, rs, ss, convs = trunk(
        x_sample, pos_s, False,
        state_mlstm_C, state_mlstm_n, state_mlstm_m, state_ret_S, state_hgrn_S, state_ffn_conv,
        norm_mix, w_in_ab, b_igate, b_fgate, gn_mlstm, gn_ret, w_out_ab,
        lb_logits, w_in_c, gn_hgrn, w_out_c, norm_ffn, w_ffn_in, conv_w, conv_b, w_ffn_out, norm_final)

    y_prompt = yp[:, N_META:]
    return (y_prompt, ys,
            cp.astype(state_mlstm_C.dtype), cs.astype(state_mlstm_C.dtype),
            n_p.astype(state_mlstm_n.dtype), n_s.astype(state_mlstm_n.dtype),
            mp.astype(state_mlstm_m.dtype), ms.astype(state_mlstm_m.dtype),
            rp.astype(state_ret_S.dtype), rs.astype(state_ret_S.dtype),
            sp.astype(state_hgrn_S.dtype), ss.astype(state_hgrn_S.dtype),
            convp.astype(state_ffn_conv.dtype), convs.astype(state_ffn_conv.dtype))
```

```python
import functools
import math

import jax
import jax.numpy as jnp
from jax import lax
from jax.experimental import pallas as pl
from jax.experimental.pallas import tpu as pltpu

EPS = 1e-6
N_META = 16
PAST_LEN = 16384
ROPE_BASE = 10000.0
HEAD_DIM = 128
H_AB = 4
CONV_W = 3
LANES = 128
SUBLANES = 8
VMEM_LIMIT = 56 * 1024 * 1024
F32 = jnp.float32
BF16 = jnp.bfloat16


def _bf(x):
    return x.astype(BF16)


def _dot(a, b):
    return jnp.dot(_bf(a), _bf(b), preferred_element_type=F32)


def _dot_nt(a, b):
    return lax.dot_general(_bf(a), _bf(b), (((1,), (1,)), ((), ())), preferred_element_type=F32)


def _dot_tn(a, b):
    return lax.dot_general(_bf(a), _bf(b), (((0,), (0,)), ((), ())), preferred_element_type=F32)


def _rms_norm(x, gain):
    y = x * lax.rsqrt(jnp.mean(x * x, axis=-1, keepdims=True) + EPS)
    return y * gain


def _sigmoid(x):
    return 1.0 / (1.0 + jnp.exp(-x))


def _cumsum_rows(x):
    n = x.shape[0]
    row = lax.broadcasted_iota(jnp.int32, x.shape, 0)
    k = 1
    while k < n:
        x = x + jnp.where(row >= k, pltpu.roll(x, k, 0), 0.0)
        k *= 2
    return x


def _params(n_grid):
    return pltpu.CompilerParams(dimension_semantics=("arbitrary",) * n_grid,
                                vmem_limit_bytes=VMEM_LIMIT)


def _proj_kernel(x_ref, g_ref, w_ref, o_ref):
    xn = _rms_norm(x_ref[...], g_ref[...])
    o_ref[...] = _dot(xn, w_ref[...])


def _proj(x2d, gain, w, tm):
    rows, d = x2d.shape
    n = w.shape[1]
    return pl.pallas_call(
        _proj_kernel,
        grid=(rows // tm,),
        in_specs=[pl.BlockSpec((tm, d), lambda i: (i, 0)),
                  pl.BlockSpec((1, d), lambda i: (0, 0)),
                  pl.BlockSpec((d, n), lambda i: (0, 0))],
        out_specs=pl.BlockSpec((tm, n), lambda i: (i, 0)),
        out_shape=jax.ShapeDtypeStruct((rows, n), F32),
        compiler_params=_params(1),
        name="proj",
    )(x2d, gain, w)


def _ab_kernel(qa_ref, ka_ref, va_ref, oa_ref, qb_ref, kb_ref, vb_ref, gg_ref, gate_ref,
               cos_ref, sin_ref, bi_ref, bf_ref, lg_ref, gna_ref, gnb_ref,
               c0_ref, n0_ref, m0_ref, r0_ref,
               mxa_ref, mxb_ref, c_ref, n_ref, m_ref, r_ref, *, sb, L):
    h = pl.program_id(1)
    chunk = pl.program_id(2)

    @pl.when(chunk == 0)
    def _():
        c_ref[...] = jnp.broadcast_to(c0_ref[...], c_ref.shape)
        n_ref[...] = jnp.broadcast_to(n0_ref[...], n_ref.shape)
        m_ref[...] = jnp.broadcast_to(m0_ref[...], m_ref.shape)
        r_ref[...] = jnp.broadcast_to(r0_ref[...], r_ref.shape)

    scale = HEAD_DIM ** -0.5
    row = lax.broadcasted_iota(jnp.int32, (L, L), 0)
    col = lax.broadcasted_iota(jnp.int32, (L, L), 1)
    causal = row >= col
    eye = row == col
    lane = lax.broadcasted_iota(jnp.int32, (L, LANES), 1)
    lg = lg_ref[0][:, :1]
    rel = jnp.where(causal, row - col, 0).astype(F32)
    decay = jnp.where(causal, jnp.exp(rel * lg), 0.0)
    tcol = lax.broadcasted_iota(jnp.int32, (L, 1), 0).astype(F32)
    inner = jnp.exp((tcol + 1.0) * lg)
    tail = jnp.exp((L - 1.0 - tcol) * lg)
    g_pow = jnp.exp(L * lg)
    cosv = cos_ref[...]
    sinv = sin_ref[...]
    bi = bi_ref[0]
    bf = bf_ref[0]
    gna = gna_ref[0]
    gnb = gnb_ref[0]

    def body(s, carry_):
        gates = gate_ref[s]
        li = jnp.sum(jnp.where(lane == h, gates, 0.0), axis=-1, keepdims=True) + bi
        fpre = jnp.sum(jnp.where(lane == h + H_AB, gates, 0.0), axis=-1, keepdims=True) + bf
        lf = jnp.minimum(fpre, 0.0) - jnp.log1p(jnp.exp(-jnp.abs(fpre)))
        b = _cumsum_rows(lf)
        a = li - b
        a_row = jnp.sum(jnp.where(eye, a[:, :L], 0.0), axis=0, keepdims=True)
        a_col = a[:, :1]
        b_col = b[:, :1]
        m0 = m_ref[s, 0][:, :1]
        log_d = jnp.where(causal, b_col + a_row, -jnp.inf)
        log_inter = b_col + m0
        m_t = jnp.maximum(log_inter, jnp.max(log_d, axis=-1, keepdims=True))
        d = jnp.exp(log_d - m_t)
        inter = jnp.exp(log_inter - m_t)
        q = qa_ref[s]
        k = ka_ref[s] * scale
        v = va_ref[s]
        sm = _dot_nt(q, k) * d
        c_prev = c_ref[s, 0]
        n_prev = n_ref[s, 0]
        num = _dot(sm, v) + inter * _dot(q, c_prev)
        den = jnp.sum(sm, axis=-1, keepdims=True) + inter * jnp.sum(q * n_prev, axis=-1, keepdims=True)
        hv = num / jnp.maximum(jnp.abs(den), jnp.exp(-m_t))
        m_new = m_t[L - 1:L, :]
        b_last = b_col[L - 1:L, :]
        w_col = jnp.exp(a_col + b_last - m_new)
        keep = jnp.exp(b_last + m0 - m_new)
        kw = k * w_col
        c_ref[s, 0] = keep * c_prev + _dot_tn(kw, v)
        n_ref[s, 0] = keep * n_prev + jnp.sum(kw, axis=0, keepdims=True)
        m_ref[s, 0] = jnp.broadcast_to(m_new, (1, LANES))
        mu = jnp.mean(hv, axis=-1, keepdims=True)
        cen = hv - mu
        y = cen * lax.rsqrt(jnp.mean(cen * cen, axis=-1, keepdims=True) + EPS) * gna
        mxa_ref[s] = (_sigmoid(oa_ref[s]) * y).astype(mxa_ref.dtype)

        q2 = qb_ref[s]
        k2 = kb_ref[s]
        v2 = vb_ref[s]
        qr = q2 * cosv + pltpu.roll(q2, HEAD_DIM // 2, 1) * sinv
        kr = (k2 * cosv + pltpu.roll(k2, HEAD_DIM // 2, 1) * sinv) * scale
        s2 = _dot_nt(qr, kr) * decay
        r_prev = r_ref[s, 0]
        o2 = _dot(s2, v2) + _dot(qr, r_prev) * inner
        r_ref[s, 0] = g_pow * r_prev + _dot_tn(kr * tail, v2)
        mu2 = jnp.mean(o2, axis=-1, keepdims=True)
        cen2 = o2 - mu2
        y2 = cen2 * lax.rsqrt(jnp.mean(cen2 * cen2, axis=-1, keepdims=True) + EPS) * gnb
        gv = gg_ref[s]
        mxb_ref[s] = (gv * _sigmoid(gv) * y2).astype(mxb_ref.dtype)
        return carry_

    lax.fori_loop(0, sb, body, 0)


def _ab_mixer(z, cos_t, sin_t, bi, bf, lg, gna, gnb, c0, n0, m0, r0, *, n_seq, seq_len, chunk, sb):
    n_chunks = seq_len // chunk
    bcast = c0.shape[0] == 1 and n_seq > 1
    sb0 = 1 if bcast else sb
    st = (lambda i: 0) if bcast else (lambda i: i)

    def zspec(off):
        return pl.BlockSpec((sb, chunk, LANES), lambda i, h, c, off=off: (i, c, off + h))

    def hspec():
        return pl.BlockSpec((1, 1, LANES), lambda i, h, c: (h, 0, 0))

    mat_in = pl.BlockSpec((sb0, 1, HEAD_DIM, HEAD_DIM), lambda i, h, c: (st(i), h, 0, 0))
    vec_in = pl.BlockSpec((sb0, 1, 1, LANES), lambda i, h, c: (st(i), h, 0, 0))
    mat_out = pl.BlockSpec((sb, 1, HEAD_DIM, HEAD_DIM), lambda i, h, c: (i, h, 0, 0))
    vec_out = pl.BlockSpec((sb, 1, 1, LANES), lambda i, h, c: (i, h, 0, 0))
    mix_out = pl.BlockSpec((sb, chunk, LANES), lambda i, h, c: (i, c, h))
    tab = pl.BlockSpec((chunk, LANES), lambda i, h, c: (c, 0))
    in_specs = ([zspec(H_AB * j) for j in range(8)]
                + [pl.BlockSpec((sb, chunk, LANES), lambda i, h, c: (i, c, 8 * H_AB))]
                + [tab, tab, hspec(), hspec(), hspec(), hspec(), hspec(), mat_in, vec_in, vec_in, mat_in])
    out_shape = (jax.ShapeDtypeStruct((n_seq, seq_len, H_AB * HEAD_DIM), BF16),
                 jax.ShapeDtypeStruct((n_seq, seq_len, H_AB * HEAD_DIM), BF16),
                 jax.ShapeDtypeStruct((n_seq, H_AB, HEAD_DIM, HEAD_DIM), F32),
                 jax.ShapeDtypeStruct((n_seq, H_AB, 1, LANES), F32),
                 jax.ShapeDtypeStruct((n_seq, H_AB, 1, LANES), F32),
                 jax.ShapeDtypeStruct((n_seq, H_AB, HEAD_DIM, HEAD_DIM), F32))
    return pl.pallas_call(
        functools.partial(_ab_kernel, sb=sb, L=chunk),
        grid=(n_seq // sb, H_AB, n_chunks),
        in_specs=in_specs,
        out_specs=(mix_out, mix_out, mat_out, vec_out, vec_out, mat_out),
        out_shape=out_shape,
        compiler_params=_params(3),
        name="ab_mixer",
    )(*([z] * 9), cos_t, sin_t, bi, bf, lg, gna, gnb, c0, n0, m0, r0)


def _hgrn_kernel(q_ref, f_ref, i_ref, g_ref, lbl_ref, gn_ref, s0_ref, mx_ref, s_ref, *, sb, L, layer):
    chunk = pl.program_id(2)

    @pl.when(chunk == 0)
    def _():
        s_ref[...] = jnp.broadcast_to(s0_ref[...], s_ref.shape)

    logits = lbl_ref[...]
    ex = jnp.exp(logits - jnp.max(logits, axis=0, keepdims=True))
    p = ex / jnp.sum(ex, axis=0, keepdims=True)
    cum = p[0:1, :]
    for r in range(1, layer + 1):
        cum = cum + p[r:r + 1, :]
    lb = cum - p[0:1, :]
    gain = gn_ref[0]

    row = lax.broadcasted_iota(jnp.int32, (L, LANES), 0)
    row_ll = lax.broadcasted_iota(jnp.int32, (L, L), 0)
    col_ll = lax.broadcasted_iota(jnp.int32, (L, L), 1)
    blk = min(SUBLANES, L)

    def body(s, carry_):
        qv = q_ref[s]
        fv = f_ref[s]
        iv = i_ref[s]
        e = jnp.exp(-jnp.abs(fv))
        rcp = 1.0 / (1.0 + e)
        pos = fv >= 0.0
        sig = jnp.where(pos, rcp, e * rcp)
        sig_neg = jnp.where(pos, e * rcp, rcp)
        log_f = jnp.log(lb + (1.0 - lb) * sig)
        k = (1.0 - lb) * sig_neg
        b = _cumsum_rows(log_f)
        s_prev = s_ref[s, 0]
        o = _dot(qv * jnp.exp(b), s_prev)

        o = o + jnp.sum(qv * k, axis=-1, keepdims=True) * iv
        for j in range(1, blk):
            valid = (row & (blk - 1)) >= j
            dec = jnp.exp(jnp.minimum(b - pltpu.roll(b, j, 0), 0.0))
            a = jnp.sum(jnp.where(valid, qv * pltpu.roll(k, j, 0) * dec, 0.0), axis=-1, keepdims=True)
            o = o + a * pltpu.roll(iv, j, 0)

        if L > blk:
            acc = jnp.zeros((L, L), F32)
            size = blk
            while size < L:
                grp = 2 * size
                b_mid = jnp.concatenate(
                    [jnp.broadcast_to(b[g0 + size - 1:g0 + size, :], (grp, LANES)) for g0 in range(0, L, grp)],
                    axis=0)
                upper = (row & size) != 0
                qe = jnp.where(upper, qv * jnp.exp(jnp.minimum(b - b_mid, 0.0)), 0.0)
                ke = jnp.where(upper, 0.0, k * jnp.exp(jnp.minimum(b_mid - b, 0.0)))
                same = (row_ll // grp) == (col_ll // grp)
                acc = acc + jnp.where(same, _dot_nt(qe, ke), 0.0)
                size = grp
            o = o + _dot(acc, iv)

        b_last = b[L - 1:L, :]
        e_col = jnp.transpose(jnp.broadcast_to(jnp.exp(b_last), (HEAD_DIM, LANES)))
        s_ref[s, 0] = e_col * s_prev + _dot_tn(k * jnp.exp(b_last - b), iv)
        y = o * lax.rsqrt(jnp.mean(o * o, axis=-1, keepdims=True) + EPS) * gain
        gv = g_ref[s]
        mx_ref[s] = (y * (gv * _sigmoid(gv))).astype(mx_ref.dtype)
        return carry_

    lax.fori_loop(0, sb, body, 0)


def _hgrn_mixer(z, lb_logits, gain, s0, *, n_seq, seq_len, chunk, sb, layer):
    n_heads = s0.shape[1]
    n_chunks = seq_len // chunk
    bcast = s0.shape[0] == 1 and n_seq > 1
    sb0 = 1 if bcast else sb
    st = (lambda i: 0) if bcast else (lambda i: i)
    depth = lb_logits.shape[0]

    def zspec(off):
        return pl.BlockSpec((sb, chunk, LANES), lambda i, h, c, off=off: (i, c, off + h))

    in_specs = [zspec(0), zspec(n_heads), zspec(2 * n_heads), zspec(3 * n_heads),
                pl.BlockSpec((depth, LANES), lambda i, h, c: (0, h)),
                pl.BlockSpec((1, 1, LANES), lambda i, h, c: (h, 0, 0)),
                pl.BlockSpec((sb0, 1, HEAD_DIM, HEAD_DIM), lambda i, h, c: (st(i), h, 0, 0))]
    out_specs = (pl.BlockSpec((sb, chunk, LANES), lambda i, h, c: (i, c, h)),
                 pl.BlockSpec((sb, 1, HEAD_DIM, HEAD_DIM), lambda i, h, c: (i, h, 0, 0)))
    out_shape = (jax.ShapeDtypeStruct((n_seq, seq_len, n_heads * HEAD_DIM), BF16),
                 jax.ShapeDtypeStruct((n_seq, n_heads, HEAD_DIM, HEAD_DIM), F32))
    return pl.pallas_call(
        functools.partial(_hgrn_kernel, sb=sb, L=chunk, layer=layer),
        grid=(n_seq // sb, n_heads, n_chunks),
        in_specs=in_specs,
        out_specs=out_specs,
        out_shape=out_shape,
        compiler_params=_params(3),
        name="hgrn_mixer",
    )(z, z, z, z, lb_logits, gain, s0)


def _ffn_kernel(*refs, n_mixed, s_blk, L, d_ff, final_norm):
    x_ref = refs[0]
    mix_refs = refs[1:1 + n_mixed]
    wo_refs = refs[1 + n_mixed:1 + 2 * n_mixed]
    (gn_ref, win_ref, cw_ref, cb_ref, wout_ref, buf_ref) = refs[1 + 2 * n_mixed:7 + 2 * n_mixed]
    pos = 7 + 2 * n_mixed
    if final_norm:
        gfin_ref = refs[pos]
        pos += 1
    xo_ref, bufo_ref, carry_ref = refs[pos:pos + 3]
    tile = pl.program_id(1)
    tm = s_blk * L

    @pl.when(tile == 0)
    def _():
        carry_ref[...] = jnp.broadcast_to(buf_ref[...], carry_ref.shape)

    x1 = x_ref[...]
    for m_ref, w_ref in zip(mix_refs, wo_refs):
        x1 = x1 + jnp.dot(m_ref[...], w_ref[...], preferred_element_type=F32)
    xn = _rms_norm(x1, gn_ref[...])
    ug = _dot(xn, win_ref[...])
    u = ug[:, :d_ff]
    gate = ug[:, d_ff:]

    t = lax.broadcasted_iota(jnp.int32, (tm, 1), 0) & (L - 1)
    if s_blk == 1:
        p0 = carry_ref[0, 0:1, :]
        p1 = carry_ref[0, 1:2, :]
    else:
        cr = carry_ref[...]
        p0 = jnp.broadcast_to(cr[:, 0:1, :], (s_blk, L, d_ff)).reshape(tm, d_ff)
        p1 = jnp.broadcast_to(cr[:, 1:2, :], (s_blk, L, d_ff)).reshape(tm, d_ff)
    prev1 = jnp.where(t == 0, p1, pltpu.roll(u, 1, 0))
    prev2 = jnp.where(t == 0, p0, jnp.where(t == 1, p1, pltpu.roll(u, 2, 0)))
    cw = cw_ref[...]
    conv = cb_ref[...] + (prev2 * cw[0:1, :] + prev1 * cw[1:2, :] + u * cw[2:3, :])
    hid = conv * _sigmoid(conv) * gate
    x2 = x1 + _dot(hid, wout_ref[...])
    if final_norm:
        xo_ref[...] = _rms_norm(x2, gfin_ref[...])
    else:
        xo_ref[...] = x2

    if s_blk == 1:
        carry_ref[0] = u[tm - (CONV_W - 1):tm, :]
    else:
        carry_ref[...] = u.reshape(s_blk, L, d_ff)[:, L - (CONV_W - 1):L, :]
    bufo_ref[...] = carry_ref[...]


def _ffn(x2d, mixed, wo_parts, gn, w_in, cw, cb, w_out, buf, gfin, *, n_seq, seq_len, tm):
    rows, d = x2d.shape
    d_ff = w_out.shape[0]
    if tm <= seq_len:
        s_blk, L = 1, tm
        tiles = seq_len // tm
        grid = (n_seq, tiles)
        rowmap = lambda s, j: (s * tiles + j, 0)
    else:
        s_blk, L = tm // seq_len, seq_len
        grid = (n_seq // s_blk, 1)
        rowmap = lambda s, j: (s, 0)
    bcast = buf.shape[0] == 1 and n_seq > 1
    bufmap = (lambda s, j: (0, 0, 0)) if bcast else (lambda s, j: (s, 0, 0))
    buf_blk = 1 if bcast else s_blk
    const = lambda s, j: (0, 0)
    n_mixed = len(mixed)
    in_specs = [pl.BlockSpec((tm, d), rowmap)]
    in_specs += [pl.BlockSpec((tm, m.shape[1]), rowmap) for m in mixed]
    in_specs += [pl.BlockSpec(w.shape, const) for w in wo_parts]
    in_specs += [pl.BlockSpec((1, d), const),
                 pl.BlockSpec(w_in.shape, const),
                 pl.BlockSpec(cw.shape, const),
                 pl.BlockSpec((1, d_ff), const),
                 pl.BlockSpec(w_out.shape, const),
                 pl.BlockSpec((buf_blk, CONV_W - 1, d_ff), bufmap)]
    args = [x2d, *mixed, *wo_parts, gn, w_in, cw, cb, w_out, buf]
    if gfin is not None:
        in_specs.append(pl.BlockSpec((1, d), const))
        args.append(gfin)
    out_specs = (pl.BlockSpec((tm, d), rowmap),
                 pl.BlockSpec((s_blk, CONV_W - 1, d_ff), lambda s, j: (s, 0, 0)))
    out_shape = (jax.ShapeDtypeStruct((rows, d), F32),
                 jax.ShapeDtypeStruct((n_seq, CONV_W - 1, d_ff), F32))
    return pl.pallas_call(
        functools.partial(_ffn_kernel, n_mixed=n_mixed, s_blk=s_blk, L=L, d_ff=d_ff,
                          final_norm=gfin is not None),
        grid=grid,
        in_specs=in_specs,
        out_specs=out_specs,
        out_shape=out_shape,
        scratch_shapes=[pltpu.VMEM((s_blk, CONV_W - 1, d_ff), F32)],
        compiler_params=_params(2),
        name="ffn",
    )(*args)


def _tile_rows(n_seq, seq_len, target):
    rows = n_seq * seq_len
    if seq_len >= target:
        return target
    return min(rows, target)


def _trunk(x, pos, st, w, *, chunk_ab, chunk_c, sb_ab, sb_c):
    n_seq, seq_len, d = x.shape
    rows = n_seq * seq_len
    tm = _tile_rows(n_seq, seq_len, 256)
    x2d = x.reshape(rows, d)

    half = HEAD_DIM // 2
    inv = 1.0 / (ROPE_BASE ** jnp.linspace(0.0, 1.0, half, dtype=F32))
    ang = pos[:, None] * inv[None, :]
    cos_t = jnp.concatenate([jnp.cos(ang), jnp.cos(ang)], axis=-1)
    sin_t = jnp.concatenate([-jnp.sin(ang), jnp.sin(ang)], axis=-1)

    z = _proj(x2d, w["norm_mix"][0], w["w_in_ab"], tm).reshape(n_seq, seq_len, -1)
    mxa, mxb, c_new, n_new, m_new, r_new = _ab_mixer(
        z, cos_t, sin_t, w["b_i"], w["b_f"], w["lg"], w["gn_a"], w["gn_b"],
        st["c"], st["n"], st["m"], st["r"], n_seq=n_seq, seq_len=seq_len, chunk=chunk_ab, sb=sb_ab)
    hd = H_AB * HEAD_DIM
    x2d, buf0 = _ffn(x2d, [mxa.reshape(rows, hd), mxb.reshape(rows, hd)], w["w_out_ab"],
                     w["norm_ffn"][0], w["w_ffn_in"][0], w["conv_w"][0], w["conv_b"][0], w["w_ffn_out"][0],
                     st["conv"][0], None, n_seq=n_seq, seq_len=seq_len, tm=tm)

    z = _proj(x2d, w["norm_mix"][1], w["w_in_c"], tm).reshape(n_seq, seq_len, -1)
    mx, s_new = _hgrn_mixer(z, w["lb_logits"], w["gn_c"], st["s"],
                            n_seq=n_seq, seq_len=seq_len, chunk=chunk_c, sb=sb_c, layer=1)
    y2d, buf1 = _ffn(x2d, [mx.reshape(rows, -1)], w["w_out_c"],
                     w["norm_ffn"][1], w["w_ffn_in"][1], w["conv_w"][1], w["conv_b"][1], w["w_ffn_out"][1],
                     st["conv"][1], w["norm_final"], n_seq=n_seq, seq_len=seq_len, tm=tm)
    new_st = {"c": c_new, "n": n_new, "m": m_new, "r": r_new, "s": s_new, "conv": (buf0, buf1)}
    return y2d.reshape(n_seq, seq_len, d), new_st


def _lane_rep(v):
    return jnp.broadcast_to(v[..., None, None], v.shape + (1, LANES))


def kernel(x_prompt, x_sample, state_mlstm_C, state_mlstm_n, state_mlstm_m, state_ret_S, state_hgrn_S,
           state_ffn_conv, meta_tokens, norm_mix, w_in_ab, b_igate, b_fgate, gn_mlstm, gn_ret, w_out_ab,
           lb_logits, w_in_c, gn_hgrn, w_out_c, norm_ffn, w_ffn_in, conv_w, conv_b, w_ffn_out, norm_final):
    bp, seq, d = x_prompt.shape
    bs, dec_seq, _ = x_sample.shape
    hd = H_AB * HEAD_DIM
    n_hc = state_hgrn_S.shape[2]
    d_ff = w_ffn_out.shape[1]
    assert w_in_ab.shape[0] == 1 and w_in_c.shape[0] == 1 and norm_mix.shape[0] == 2

    wab = w_in_ab[0]
    g0 = 4 * hd
    w_ab = jnp.concatenate([wab[:, :g0], wab[:, g0 + 2 * H_AB:], wab[:, g0:g0 + 2 * H_AB],
                            jnp.zeros((d, LANES - 2 * H_AB), wab.dtype)], axis=1).astype(BF16)
    wo_ab = w_out_ab[0].astype(BF16)
    log_gamma = jnp.log1p(-jnp.exp2(-5.0 - jnp.arange(H_AB, dtype=F32)))
    w = {
        "norm_mix": norm_mix[:, None, :],
        "norm_ffn": norm_ffn[:, None, :],
        "norm_final": norm_final[None, :],
        "w_in_ab": w_ab,
        "b_i": _lane_rep(b_igate[0]),
        "b_f": _lane_rep(b_fgate[0]),
        "lg": _lane_rep(log_gamma),
        "gn_a": gn_mlstm[0].reshape(H_AB, 1, HEAD_DIM),
        "gn_b": gn_ret[0].reshape(H_AB, 1, HEAD_DIM),
        "w_out_ab": [wo_ab[:hd], wo_ab[hd:]],
        "lb_logits": lb_logits,
        "w_in_c": w_in_c[0].astype(BF16),
        "gn_c": gn_hgrn[0].reshape(n_hc, 1, HEAD_DIM),
        "w_out_c": [w_out_c[0].astype(BF16)],
        "w_ffn_in": w_ffn_in.astype(BF16),
        "conv_w": conv_w,
        "conv_b": conv_b[:, None, :],
        "w_ffn_out": w_ffn_out.astype(BF16),
    }

    def zero_state(n):
        return {"c": jnp.zeros((n, H_AB, HEAD_DIM, HEAD_DIM), F32),
                "n": jnp.zeros((n, H_AB, 1, LANES), F32),
                "m": jnp.zeros((n, H_AB, 1, LANES), F32),
                "r": jnp.zeros((n, H_AB, HEAD_DIM, HEAD_DIM), F32),
                "s": jnp.zeros((n, n_hc, HEAD_DIM, HEAD_DIM), F32),
                "conv": (jnp.zeros((n, CONV_W - 1, d_ff), F32), jnp.zeros((n, CONV_W - 1, d_ff), F32))}

    _, st_meta = _trunk(meta_tokens[None].astype(F32), jnp.arange(N_META, dtype=F32), zero_state(1), w,
                        chunk_ab=N_META, chunk_c=N_META, sb_ab=1, sb_c=1)
    pos_p = N_META + jnp.arange(seq, dtype=F32)
    y_prompt, st_p = _trunk(x_prompt, pos_p, st_meta, w, chunk_ab=128, chunk_c=128,
                            sb_ab=min(bp, 8), sb_c=min(bp, 8))
    st_s = {"c": state_mlstm_C[0], "n": state_mlstm_n[0][:, :, None, :], "m": _lane_rep(state_mlstm_m[0]),
            "r": state_ret_S[0], "s": state_hgrn_S[0], "conv": (state_ffn_conv[0], state_ffn_conv[1])}
    pos_s = PAST_LEN + jnp.arange(dec_seq, dtype=F32)
    y_sample, st_s = _trunk(x_sample, pos_s, st_s, w, chunk_ab=dec_seq, chunk_c=dec_seq,
                            sb_ab=min(bs, 32), sb_c=min(bs, 32))

    def outs(s):
        return (s["c"][None], s["n"][:, :, 0, :][None], s["m"][:, :, 0, 0][None], s["r"][None], s["s"][None],
                jnp.stack(s["conv"]))

    cp, n_p, mp, rp, sp, convp = outs(st_p)
    cs, n_s, ms, rs, ss, convs = outs(st_s)
    return (y_prompt, y_sample, cp, cs, n_p, n_s, mp, ms, rp, rs, sp, ss, convp, convs)
```

```python
import functools

import numpy as np

import jax
import jax.numpy as jnp
from jax import lax
from jax.experimental import pallas as pl
from jax.experimental.pallas import tpu as pltpu

EPS = 1e-6
N_META = 16
PAST_LEN = 16384
ROPE_BASE = 10000.0
HEAD_DIM = 128
H_AB = 4
CONV_W = 3
LANES = 128
SUBLANES = 8
VMEM_LIMIT = 56 * 1024 * 1024
F32 = jnp.float32
BF16 = jnp.bfloat16


def _bf(x):
    return x.astype(BF16)


def _dot(a, b):
    return jnp.dot(_bf(a), _bf(b), preferred_element_type=F32)


def _dot_nt(a, b):
    return lax.dot_general(_bf(a), _bf(b), (((1,), (1,)), ((), ())), preferred_element_type=F32)


def _dot_tn(a, b):
    return lax.dot_general(_bf(a), _bf(b), (((0,), (0,)), ((), ())), preferred_element_type=F32)


def _rms_norm(x, gain):
    y = x * lax.rsqrt(jnp.mean(x * x, axis=-1, keepdims=True) + EPS)
    return y * gain


def _sigmoid(x):
    return 1.0 / (1.0 + jnp.exp(-x))


def _layer_norm(x, gain):
    mu = jnp.mean(x, axis=-1, keepdims=True)
    cen = x - mu
    return cen * lax.rsqrt(jnp.mean(cen * cen, axis=-1, keepdims=True) + EPS) * gain


def _cumsum_rows(x):
    n = x.shape[0]
    row = lax.broadcasted_iota(jnp.int32, x.shape, 0)
    k = 1
    while k < n:
        x = x + jnp.where(row >= k, pltpu.roll(x, k, 0), 0.0)
        k *= 2
    return x


def _split_dot(mat_bf16, x):
    hi = _bf(x)
    lo = _bf(x - hi.astype(F32))
    n = x.shape[1]
    r = jnp.dot(mat_bf16, jnp.concatenate([hi, lo], axis=1), preferred_element_type=F32)
    return r[:, :n] + r[:, n:]


def _params(n_grid):
    return pltpu.CompilerParams(dimension_semantics=("arbitrary",) * n_grid,
                                vmem_limit_bytes=VMEM_LIMIT)


def _proj_kernel(x_ref, g_ref, w_ref, o_ref):
    xn = _rms_norm(x_ref[...], g_ref[...])
    o_ref[...] = _dot(xn, w_ref[...])


def _proj(x2d, gain, w, tm):
    rows, d = x2d.shape
    n = w.shape[1]
    return pl.pallas_call(
        _proj_kernel,
        grid=(rows // tm,),
        in_specs=[pl.BlockSpec((tm, d), lambda i: (i, 0)),
                  pl.BlockSpec((1, d), lambda i: (0, 0)),
                  pl.BlockSpec((d, n), lambda i: (0, 0))],
        out_specs=pl.BlockSpec((tm, n), lambda i: (i, 0)),
        out_shape=jax.ShapeDtypeStruct((rows, n), F32),
        compiler_params=_params(1),
        name="proj",
    )(x2d, gain, w)


def _ab_kernel(*refs, sb, L, unroll):
    if L > SUBLANES:
        (z_ref, cos_ref, sin_ref, bias_ref, lg_ref, gna_ref, gnb_ref, c0_ref, n0_ref, m0_ref, r0_ref, tri_ref,
         mxa_ref, mxb_ref, c_ref, n_ref, m_ref, r_ref) = refs
    else:
        (z_ref, cos_ref, sin_ref, bias_ref, lg_ref, gna_ref, gnb_ref, c0_ref, n0_ref, m0_ref, r0_ref,
         mxa_ref, mxb_ref, c_ref, n_ref, m_ref, r_ref) = refs
    chunk = pl.program_id(1)

    @pl.when(chunk == 0)
    def _():
        c_ref[...] = jnp.broadcast_to(c0_ref[...], c_ref.shape)
        n_ref[...] = jnp.broadcast_to(n0_ref[...], n_ref.shape)
        m_ref[...] = jnp.broadcast_to(m0_ref[...], m_ref.shape)
        r_ref[...] = jnp.broadcast_to(r0_ref[...], r_ref.shape)

    scale = HEAD_DIM ** -0.5
    row = lax.broadcasted_iota(jnp.int32, (L, L), 0)
    col = lax.broadcasted_iota(jnp.int32, (L, L), 1)
    causal = row >= col
    eye = row == col
    rel = jnp.where(causal, row - col, 0).astype(F32)
    tcol = lax.broadcasted_iota(jnp.int32, (L, 1), 0).astype(F32)
    cosv = cos_ref[...]
    sinv = sin_ref[...]
    bias = bias_ref[...]
    gate_col = 8 * H_AB * HEAD_DIM

    def one_seq(s):
        new_state = []
        pre = z_ref[s, :, gate_col:gate_col + LANES] + bias
        lf = jnp.minimum(pre, 0.0) - jnp.log1p(jnp.exp(-jnp.abs(pre)))
        lf = pltpu.roll(lf, LANES - H_AB, 1)
        if L > SUBLANES:
            b = _split_dot(tri_ref[...], lf)
        else:
            b = _cumsum_rows(lf)
        a = pre - b
        a_rows = jnp.transpose(a) if L == LANES else None

        for h in range(H_AB):
            def zcol(j, h=h):
                c0 = (j * H_AB + h) * HEAD_DIM
                return z_ref[s, :, c0:c0 + HEAD_DIM]
            hs = slice(h * HEAD_DIM, (h + 1) * HEAD_DIM)

            a_col = a[:, h:h + 1]
            b_col = b[:, h:h + 1]
            if a_rows is not None:
                a_row = a_rows[h:h + 1, :]
            else:
                a_row = jnp.sum(jnp.where(eye, a_col, 0.0), axis=0, keepdims=True)
            m0 = m_ref[s, h][:, :1]
            log_d = jnp.where(causal, b_col + a_row, -jnp.inf)
            log_inter = b_col + m0
            m_t = jnp.maximum(log_inter, jnp.max(log_d, axis=-1, keepdims=True))
            d = jnp.exp(log_d - m_t)
            inter = jnp.exp(log_inter - m_t)
            q = zcol(0)
            k = zcol(1) * scale
            v = zcol(2)
            sm = _dot_nt(q, k) * d
            c_prev = c_ref[s, h]
            n_prev = n_ref[s, h]
            num = _dot(sm, v) + inter * _dot(q, c_prev)
            den = jnp.sum(sm, axis=-1, keepdims=True) + inter * jnp.sum(q * n_prev, axis=-1, keepdims=True)
            hv = num / jnp.maximum(jnp.abs(den), jnp.exp(-m_t))
            m_new = m_t[L - 1:L, :]
            b_last = b_col[L - 1:L, :]
            w_col = jnp.exp(a_col + b_last - m_new)
            keep = jnp.exp(b_last + m0 - m_new)
            kw = k * w_col
            new_state.append((c_ref, h, keep * c_prev + _dot_tn(kw, v)))
            new_state.append((n_ref, h, keep * n_prev + jnp.sum(kw, axis=0, keepdims=True)))
            new_state.append((m_ref, h, jnp.broadcast_to(m_new, (1, LANES))))
            y = _layer_norm(hv, gna_ref[:, hs])
            mxa_ref[s, :, hs] = (_sigmoid(zcol(3)) * y).astype(mxa_ref.dtype)

            lg = lg_ref[h][:, :1]
            q2 = zcol(4)
            k2 = zcol(5)
            v2 = zcol(6)
            qr = q2 * cosv + pltpu.roll(q2, HEAD_DIM // 2, 1) * sinv
            kr = (k2 * cosv + pltpu.roll(k2, HEAD_DIM // 2, 1) * sinv) * scale
            decay = jnp.where(causal, jnp.exp(rel * lg), 0.0)
            s2 = _dot_nt(qr, kr) * decay
            r_prev = r_ref[s, h]
            o2 = _dot(s2, v2) + _dot(qr, r_prev) * jnp.exp((tcol + 1.0) * lg)
            new_state.append((r_ref, h, jnp.exp(L * lg) * r_prev + _dot_tn(kr * jnp.exp((L - 1.0 - tcol) * lg), v2)))
            y2 = _layer_norm(o2, gnb_ref[:, hs])
            gv = zcol(7)
            mxb_ref[s, :, hs] = (gv * _sigmoid(gv) * y2).astype(mxb_ref.dtype)
        return new_state

    def group(i, carry_):
        seqs = [i * unroll + j for j in range(unroll)]
        new = [one_seq(s) for s in seqs]
        for s, new_state in zip(seqs, new):
            for ref, h, val in new_state:
                ref[s, h] = val
        return carry_

    lax.fori_loop(0, sb // unroll, group, 0)


def _ab_mixer(z, cos_t, sin_t, bias, lg, gna, gnb, c0, n0, m0, r0, *, n_seq, seq_len, chunk, sb, unroll):
    n_chunks = seq_len // chunk
    width = z.shape[-1]
    hd = H_AB * HEAD_DIM
    bcast = c0.shape[0] == 1 and n_seq > 1
    sb0 = 1 if bcast else sb
    st = (lambda i: 0) if bcast else (lambda i: i)
    full2 = lambda a: pl.BlockSpec(a.shape, lambda i, c: (0, 0))
    mat_in = pl.BlockSpec((sb0, H_AB, HEAD_DIM, HEAD_DIM), lambda i, c: (st(i), 0, 0, 0))
    vec_in = pl.BlockSpec((sb0, H_AB, 1, LANES), lambda i, c: (st(i), 0, 0, 0))
    mat_out = pl.BlockSpec((sb, H_AB, HEAD_DIM, HEAD_DIM), lambda i, c: (i, 0, 0, 0))
    vec_out = pl.BlockSpec((sb, H_AB, 1, LANES), lambda i, c: (i, 0, 0, 0))
    mix_out = pl.BlockSpec((sb, chunk, hd), lambda i, c: (i, c, 0))
    tab = pl.BlockSpec((chunk, LANES), lambda i, c: (c, 0))
    in_specs = [pl.BlockSpec((sb, chunk, width), lambda i, c: (i, c, 0)), tab, tab, full2(bias),
                pl.BlockSpec(lg.shape, lambda i, c: (0, 0, 0)), full2(gna), full2(gnb),
                mat_in, vec_in, vec_in, mat_in]
    args = [z, cos_t, sin_t, bias, lg, gna, gnb, c0, n0, m0, r0]
    if chunk > SUBLANES:
        tri = jnp.asarray(np.tril(np.ones((chunk, chunk), np.float32)), BF16)
        in_specs.append(full2(tri))
        args.append(tri)
    out_shape = (jax.ShapeDtypeStruct((n_seq, seq_len, hd), BF16),
                 jax.ShapeDtypeStruct((n_seq, seq_len, hd), BF16),
                 jax.ShapeDtypeStruct((n_seq, H_AB, HEAD_DIM, HEAD_DIM), F32),
                 jax.ShapeDtypeStruct((n_seq, H_AB, 1, LANES), F32),
                 jax.ShapeDtypeStruct((n_seq, H_AB, 1, LANES), F32),
                 jax.ShapeDtypeStruct((n_seq, H_AB, HEAD_DIM, HEAD_DIM), F32))
    return pl.pallas_call(
        functools.partial(_ab_kernel, sb=sb, L=chunk, unroll=unroll),
        grid=(n_seq // sb, n_chunks),
        in_specs=in_specs,
        out_specs=(mix_out, mix_out, mat_out, vec_out, vec_out, mat_out),
        out_shape=out_shape,
        compiler_params=_params(2),
        name="ab_mixer",
    )(*args)


def _hgrn_level_tables(L):
    t = np.arange(L)[:, None]
    u = np.arange(L)[None, :]
    mats = [u <= t]
    lvl = np.full((L, L), -1, np.int32)
    size, j = 1, 0
    while size < L:
        same_seg = (t // size) == (u // size)
        upper = ((t // size) % 2) == 1
        mats.append(same_seg & np.where(upper, u <= t, u > t))
        lvl[((t // size) == (u // size) + 1) & upper] = j
        size *= 2
        j += 1
    return np.concatenate(mats, axis=0).astype(np.float32), lvl


def _hgrn_kernel(q_ref, f_ref, i_ref, g_ref, lbl_ref, gn_ref, s0_ref, *rest, sb, L, layer, unroll):
    if L > SUBLANES:
        mats_ref, lvl_ref, mx_ref, s_ref = rest
    else:
        mx_ref, s_ref = rest
    chunk = pl.program_id(2)

    @pl.when(chunk == 0)
    def _():
        s_ref[...] = jnp.broadcast_to(s0_ref[...], s_ref.shape)

    logits = lbl_ref[...]
    ex = jnp.exp(logits - jnp.max(logits, axis=0, keepdims=True))
    p = ex / jnp.sum(ex, axis=0, keepdims=True)
    cum = p[0:1, :]
    for r in range(1, layer + 1):
        cum = cum + p[r:r + 1, :]
    lb = cum - p[0:1, :]
    gain = gn_ref[0]

    row = lax.broadcasted_iota(jnp.int32, (L, LANES), 0)
    n_levels = L.bit_length() - 1

    def one_seq(s):
        qv = q_ref[s]
        fv = f_ref[s]
        iv = i_ref[s]
        e = jnp.exp(-jnp.abs(fv))
        rcp = 1.0 / (1.0 + e)
        pos = fv >= 0.0
        sig = jnp.where(pos, rcp, e * rcp)
        sig_neg = jnp.where(pos, e * rcp, rcp)
        log_f = jnp.log(lb + (1.0 - lb) * sig)
        k = (1.0 - lb) * sig_neg
        s_prev = s_ref[s, 0]
        o = jnp.sum(qv * k, axis=-1, keepdims=True) * iv

        if L > SUBLANES:
            expo = _split_dot(mats_ref[...], log_f)
            b = expo[:L]
            lvl = lvl_ref[...]
            acc = jnp.zeros((L, L), F32)
            for j in range(n_levels):
                upper = (row & (1 << j)) != 0
                xj = jnp.where(upper, qv, k) * jnp.exp(expo[(j + 1) * L:(j + 2) * L])
                acc = jnp.where(lvl == j, _dot_nt(xj, xj), acc)
            o = o + _dot(acc, iv)
        else:
            b = _cumsum_rows(log_f)
            for j in range(1, L):
                dec = jnp.exp(jnp.minimum(b - pltpu.roll(b, j, 0), 0.0))
                a = jnp.sum(jnp.where(row >= j, qv * pltpu.roll(k, j, 0) * dec, 0.0), axis=-1, keepdims=True)
                o = o + a * pltpu.roll(iv, j, 0)

        o = o + _dot(qv * jnp.exp(b), s_prev)
        b_last = b[L - 1:L, :]
        e_col = jnp.transpose(jnp.broadcast_to(jnp.exp(b_last), (HEAD_DIM, LANES)))
        s_new = e_col * s_prev + _dot_tn(k * jnp.exp(b_last - b), iv)
        y = o * lax.rsqrt(jnp.mean(o * o, axis=-1, keepdims=True) + EPS) * gain
        gv = g_ref[s]
        mx_ref[s] = (y * (gv * _sigmoid(gv))).astype(mx_ref.dtype)
        return s_new

    def group(i, carry_):
        seqs = [i * unroll + j for j in range(unroll)]
        new = [one_seq(s) for s in seqs]
        for s, s_new in zip(seqs, new):
            s_ref[s, 0] = s_new
        return carry_

    lax.fori_loop(0, sb // unroll, group, 0)


def _hgrn_mixer(z, lb_logits, gain, s0, *, n_seq, seq_len, chunk, sb, layer, unroll):
    n_heads = s0.shape[1]
    n_chunks = seq_len // chunk
    bcast = s0.shape[0] == 1 and n_seq > 1
    sb0 = 1 if bcast else sb
    st = (lambda i: 0) if bcast else (lambda i: i)
    depth = lb_logits.shape[0]

    def zspec(off):
        return pl.BlockSpec((sb, chunk, LANES), lambda i, h, c, off=off: (i, c, off + h))

    in_specs = [zspec(0), zspec(n_heads), zspec(2 * n_heads), zspec(3 * n_heads),
                pl.BlockSpec((depth, LANES), lambda i, h, c: (0, h)),
                pl.BlockSpec((1, 1, LANES), lambda i, h, c: (h, 0, 0)),
                pl.BlockSpec((sb0, 1, HEAD_DIM, HEAD_DIM), lambda i, h, c: (st(i), h, 0, 0))]
    out_specs = (pl.BlockSpec((sb, chunk, LANES), lambda i, h, c: (i, c, h)),
                 pl.BlockSpec((sb, 1, HEAD_DIM, HEAD_DIM), lambda i, h, c: (i, h, 0, 0)))
    out_shape = (jax.ShapeDtypeStruct((n_seq, seq_len, n_heads * HEAD_DIM), BF16),
                 jax.ShapeDtypeStruct((n_seq, n_heads, HEAD_DIM, HEAD_DIM), F32))
    args = [z, z, z, z, lb_logits, gain, s0]
    if chunk > SUBLANES:
        mats, lvl = _hgrn_level_tables(chunk)
        in_specs += [pl.BlockSpec(mats.shape, lambda i, h, c: (0, 0)),
                     pl.BlockSpec(lvl.shape, lambda i, h, c: (0, 0))]
        args += [jnp.asarray(mats, BF16), jnp.asarray(lvl)]
    return pl.pallas_call(
        functools.partial(_hgrn_kernel, sb=sb, L=chunk, layer=layer, unroll=unroll),
        grid=(n_seq // sb, n_heads, n_chunks),
        in_specs=in_specs,
        out_specs=out_specs,
        out_shape=out_shape,
        compiler_params=_params(3),
        name="hgrn_mixer",
    )(*args)


def _ffn_kernel(*refs, n_mixed, s_blk, L, d_ff, final_norm):
    x_ref = refs[0]
    mix_refs = refs[1:1 + n_mixed]
    wo_refs = refs[1 + n_mixed:1 + 2 * n_mixed]
    (gn_ref, win_ref, cw_ref, cb_ref, wout_ref, buf_ref) = refs[1 + 2 * n_mixed:7 + 2 * n_mixed]
    pos = 7 + 2 * n_mixed
    if final_norm:
        gfin_ref = refs[pos]
        pos += 1
    xo_ref, bufo_ref, carry_ref = refs[pos:pos + 3]
    tile = pl.program_id(1)
    tm = s_blk * L

    @pl.when(tile == 0)
    def _():
        carry_ref[...] = jnp.broadcast_to(buf_ref[...], carry_ref.shape)

    x1 = x_ref[...]
    for m_ref, w_ref in zip(mix_refs, wo_refs):
        x1 = x1 + jnp.dot(m_ref[...], w_ref[...], preferred_element_type=F32)
    xn = _rms_norm(x1, gn_ref[...])
    ug = _dot(xn, win_ref[...])
    u = ug[:, :d_ff]
    gate = ug[:, d_ff:]

    t = lax.broadcasted_iota(jnp.int32, (tm, 1), 0) & (L - 1)
    if s_blk == 1:
        p0 = carry_ref[0, 0:1, :]
        p1 = carry_ref[0, 1:2, :]
    else:
        cr = carry_ref[...]
        p0 = jnp.broadcast_to(cr[:, 0:1, :], (s_blk, L, d_ff)).reshape(tm, d_ff)
        p1 = jnp.broadcast_to(cr[:, 1:2, :], (s_blk, L, d_ff)).reshape(tm, d_ff)
    prev1 = jnp.where(t == 0, p1, pltpu.roll(u, 1, 0))
    prev2 = jnp.where(t == 0, p0, jnp.where(t == 1, p1, pltpu.roll(u, 2, 0)))
    cw = cw_ref[...]
    conv = cb_ref[...] + (prev2 * cw[0:1, :] + prev1 * cw[1:2, :] + u * cw[2:3, :])
    hid = conv * _sigmoid(conv) * gate
    x2 = x1 + _dot(hid, wout_ref[...])
    if final_norm:
        xo_ref[...] = _rms_norm(x2, gfin_ref[...])
    else:
        xo_ref[...] = x2

    if s_blk == 1:
        carry_ref[0] = u[tm - (CONV_W - 1):tm, :]
    else:
        carry_ref[...] = u.reshape(s_blk, L, d_ff)[:, L - (CONV_W - 1):L, :]
    bufo_ref[...] = carry_ref[...]


def _ffn(x2d, mixed, wo_parts, gn, w_in, cw, cb, w_out, buf, gfin, *, n_seq, seq_len, tm):
    rows, d = x2d.shape
    d_ff = w_out.shape[0]
    if tm <= seq_len:
        s_blk, L = 1, tm
        tiles = seq_len // tm
        grid = (n_seq, tiles)
        rowmap = lambda s, j: (s * tiles + j, 0)
    else:
        s_blk, L = tm // seq_len, seq_len
        grid = (n_seq // s_blk, 1)
        rowmap = lambda s, j: (s, 0)
    bcast = buf.shape[0] == 1 and n_seq > 1
    bufmap = (lambda s, j: (0, 0, 0)) if bcast else (lambda s, j: (s, 0, 0))
    buf_blk = 1 if bcast else s_blk
    const = lambda s, j: (0, 0)
    n_mixed = len(mixed)
    in_specs = [pl.BlockSpec((tm, d), rowmap)]
    in_specs += [pl.BlockSpec((tm, m.shape[1]), rowmap) for m in mixed]
    in_specs += [pl.BlockSpec(w.shape, const) for w in wo_parts]
    in_specs += [pl.BlockSpec((1, d), const),
                 pl.BlockSpec(w_in.shape, const),
                 pl.BlockSpec(cw.shape, const),
                 pl.BlockSpec((1, d_ff), const),
                 pl.BlockSpec(w_out.shape, const),
                 pl.BlockSpec((buf_blk, CONV_W - 1, d_ff), bufmap)]
    args = [x2d, *mixed, *wo_parts, gn, w_in, cw, cb, w_out, buf]
    if gfin is not None:
        in_specs.append(pl.BlockSpec((1, d), const))
        args.append(gfin)
    out_specs = (pl.BlockSpec((tm, d), rowmap),
                 pl.BlockSpec((s_blk, CONV_W - 1, d_ff), lambda s, j: (s, 0, 0)))
    out_shape = (jax.ShapeDtypeStruct((rows, d), F32),
                 jax.ShapeDtypeStruct((n_seq, CONV_W - 1, d_ff), F32))
    return pl.pallas_call(
        functools.partial(_ffn_kernel, n_mixed=n_mixed, s_blk=s_blk, L=L, d_ff=d_ff,
                          final_norm=gfin is not None),
        grid=grid,
        in_specs=in_specs,
        out_specs=out_specs,
        out_shape=out_shape,
        scratch_shapes=[pltpu.VMEM((s_blk, CONV_W - 1, d_ff), F32)],
        compiler_params=_params(2),
        name="ffn",
    )(*args)


def _tile_rows(n_seq, seq_len, target):
    rows = n_seq * seq_len
    if seq_len >= target:
        return target
    return min(rows, target)


def _trunk(x, pos, st, w, *, chunk_ab, chunk_c, sb_ab, sb_c, unroll_ab, unroll_c):
    n_seq, seq_len, d = x.shape
    rows = n_seq * seq_len
    tm = _tile_rows(n_seq, seq_len, 256)
    x2d = x.reshape(rows, d)

    half = HEAD_DIM // 2
    inv = 1.0 / (ROPE_BASE ** jnp.linspace(0.0, 1.0, half, dtype=F32))
    ang = pos[:, None] * inv[None, :]
    cos_t = jnp.concatenate([jnp.cos(ang), jnp.cos(ang)], axis=-1)
    sin_t = jnp.concatenate([-jnp.sin(ang), jnp.sin(ang)], axis=-1)

    z = _proj(x2d, w["norm_mix"][0], w["w_in_ab"], tm).reshape(n_seq, seq_len, -1)
    mxa, mxb, c_new, n_new, m_new, r_new = _ab_mixer(
        z, cos_t, sin_t, w["gate_bias"], w["lg"], w["gn_a"], w["gn_b"],
        st["c"], st["n"], st["m"], st["r"], n_seq=n_seq, seq_len=seq_len, chunk=chunk_ab, sb=sb_ab,
        unroll=unroll_ab)
    hd = H_AB * HEAD_DIM
    x2d, buf0 = _ffn(x2d, [mxa.reshape(rows, hd), mxb.reshape(rows, hd)], w["w_out_ab"],
                     w["norm_ffn"][0], w["w_ffn_in"][0], w["conv_w"][0], w["conv_b"][0], w["w_ffn_out"][0],
                     st["conv"][0], None, n_seq=n_seq, seq_len=seq_len, tm=tm)

    z = _proj(x2d, w["norm_mix"][1], w["w_in_c"], tm).reshape(n_seq, seq_len, -1)
    mx, s_new = _hgrn_mixer(z, w["lb_logits"], w["gn_c"], st["s"],
                            n_seq=n_seq, seq_len=seq_len, chunk=chunk_c, sb=sb_c, layer=1, unroll=unroll_c)
    y2d, buf1 = _ffn(x2d, [mx.reshape(rows, -1)], w["w_out_c"],
                     w["norm_ffn"][1], w["w_ffn_in"][1], w["conv_w"][1], w["conv_b"][1], w["w_ffn_out"][1],
                     st["conv"][1], w["norm_final"], n_seq=n_seq, seq_len=seq_len, tm=tm)
    new_st = {"c": c_new, "n": n_new, "m": m_new, "r": r_new, "s": s_new, "conv": (buf0, buf1)}
    return y2d.reshape(n_seq, seq_len, d), new_st


def _lane_rep(v):
    return jnp.broadcast_to(v[..., None, None], v.shape + (1, LANES))


def kernel(x_prompt, x_sample, state_mlstm_C, state_mlstm_n, state_mlstm_m, state_ret_S, state_hgrn_S,
           state_ffn_conv, meta_tokens, norm_mix, w_in_ab, b_igate, b_fgate, gn_mlstm, gn_ret, w_out_ab,
           lb_logits, w_in_c, gn_hgrn, w_out_c, norm_ffn, w_ffn_in, conv_w, conv_b, w_ffn_out, norm_final):
    bp, seq, d = x_prompt.shape
    bs, dec_seq, _ = x_sample.shape
    hd = H_AB * HEAD_DIM
    n_hc = state_hgrn_S.shape[2]
    d_ff = w_ffn_out.shape[1]
    assert w_in_ab.shape[0] == 1 and w_in_c.shape[0] == 1 and norm_mix.shape[0] == 2

    wab = w_in_ab[0]
    g0 = 4 * hd
    w_ab = jnp.concatenate([wab[:, :g0], wab[:, g0 + 2 * H_AB:], wab[:, g0:g0 + 2 * H_AB],
                            jnp.zeros((d, LANES - 2 * H_AB), wab.dtype)], axis=1).astype(BF16)
    wo_ab = w_out_ab[0].astype(BF16)
    log_gamma = jnp.log1p(-jnp.exp2(-5.0 - jnp.arange(H_AB, dtype=F32)))
    gate_bias = jnp.concatenate([b_igate[0], b_fgate[0], jnp.zeros((LANES - 2 * H_AB,), F32)])[None, :]
    w = {
        "norm_mix": norm_mix[:, None, :],
        "norm_ffn": norm_ffn[:, None, :],
        "norm_final": norm_final[None, :],
        "w_in_ab": w_ab,
        "gate_bias": gate_bias,
        "lg": _lane_rep(log_gamma),
        "gn_a": gn_mlstm[0][None, :],
        "gn_b": gn_ret[0][None, :],
        "w_out_ab": [wo_ab[:hd], wo_ab[hd:]],
        "lb_logits": lb_logits,
        "w_in_c": w_in_c[0].astype(BF16),
        "gn_c": gn_hgrn[0].reshape(n_hc, 1, HEAD_DIM),
        "w_out_c": [w_out_c[0].astype(BF16)],
        "w_ffn_in": w_ffn_in.astype(BF16),
        "conv_w": conv_w,
        "conv_b": conv_b[:, None, :],
        "w_ffn_out": w_ffn_out.astype(BF16),
    }

    def zero_state(n):
        return {"c": jnp.zeros((n, H_AB, HEAD_DIM, HEAD_DIM), F32),
                "n": jnp.zeros((n, H_AB, 1, LANES), F32),
                "m": jnp.zeros((n, H_AB, 1, LANES), F32),
                "r": jnp.zeros((n, H_AB, HEAD_DIM, HEAD_DIM), F32),
                "s": jnp.zeros((n, n_hc, HEAD_DIM, HEAD_DIM), F32),
                "conv": (jnp.zeros((n, CONV_W - 1, d_ff), F32), jnp.zeros((n, CONV_W - 1, d_ff), F32))}

    _, st_meta = _trunk(meta_tokens[None].astype(F32), jnp.arange(N_META, dtype=F32), zero_state(1), w,
                        chunk_ab=N_META, chunk_c=N_META, sb_ab=1, sb_c=1, unroll_ab=1, unroll_c=1)
    pos_p = N_META + jnp.arange(seq, dtype=F32)
    y_prompt, st_p = _trunk(x_prompt, pos_p, st_meta, w, chunk_ab=128, chunk_c=128,
                            sb_ab=min(bp, 4), sb_c=min(bp, 8), unroll_ab=1, unroll_c=2)
    st_s = {"c": state_mlstm_C[0], "n": state_mlstm_n[0][:, :, None, :], "m": _lane_rep(state_mlstm_m[0]),
            "r": state_ret_S[0], "s": state_hgrn_S[0], "conv": (state_ffn_conv[0], state_ffn_conv[1])}
    pos_s = PAST_LEN + jnp.arange(dec_seq, dtype=F32)
    y_sample, st_s = _trunk(x_sample, pos_s, st_s, w, chunk_ab=dec_seq, chunk_c=dec_seq,
                            sb_ab=min(bs, 8), sb_c=min(bs, 32), unroll_ab=2, unroll_c=8)

    def outs(s):
        return (s["c"][None], s["n"][:, :, 0, :][None], s["m"][:, :, 0, 0][None], s["r"][None], s["s"][None],
                jnp.stack(s["conv"]))

    cp, n_p, mp, rp, sp, convp = outs(st_p)
    cs, n_s, ms, rs, ss, convs = outs(st_s)
    return (y_prompt, y_sample, cp, cs, n_p, n_s, mp, ms, rp, rs, sp, ss, convp, convs)
```

```python
import functools

import numpy as np

import jax
import jax.numpy as jnp
from jax import lax
from jax.experimental import pallas as pl
from jax.experimental.pallas import tpu as pltpu

EPS = 1e-6
N_META = 16
PAST_LEN = 16384
ROPE_BASE = 10000.0
HEAD_DIM = 128
H_AB = 4
CONV_W = 3
LANES = 128
SUBLANES = 8
VMEM_LIMIT = 56 * 1024 * 1024
F32 = jnp.float32
BF16 = jnp.bfloat16


def _bf(x):
    return x.astype(BF16)


def _dot(a, b):
    return jnp.dot(_bf(a), _bf(b), preferred_element_type=F32)


def _dot_nt(a, b):
    return lax.dot_general(_bf(a), _bf(b), (((1,), (1,)), ((), ())), preferred_element_type=F32)


def _dot_tn(a, b):
    return lax.dot_general(_bf(a), _bf(b), (((0,), (0,)), ((), ())), preferred_element_type=F32)


def _rms_norm(x, gain):
    y = x * lax.rsqrt(jnp.mean(x * x, axis=-1, keepdims=True) + EPS)
    return y * gain


def _sigmoid(x):
    return 1.0 / (1.0 + jnp.exp(-x))


def _layer_norm(x, gain):
    mu = jnp.mean(x, axis=-1, keepdims=True)
    cen = x - mu
    return cen * lax.rsqrt(jnp.mean(cen * cen, axis=-1, keepdims=True) + EPS) * gain


def _cumsum_rows(x):
    n = x.shape[0]
    row = lax.broadcasted_iota(jnp.int32, x.shape, 0)
    k = 1
    while k < n:
        x = x + jnp.where(row >= k, pltpu.roll(x, k, 0), 0.0)
        k *= 2
    return x


def _split_dot(mat_bf16, x):
    hi = _bf(x)
    lo = _bf(x - hi.astype(F32))
    n = x.shape[1]
    r = jnp.dot(mat_bf16, jnp.concatenate([hi, lo], axis=1), preferred_element_type=F32)
    return r[:, :n] + r[:, n:]


def _params(n_grid):
    return pltpu.CompilerParams(dimension_semantics=("arbitrary",) * n_grid,
                                vmem_limit_bytes=VMEM_LIMIT)


def _proj_kernel(x_ref, g_ref, w_ref, o_ref):
    xn = _rms_norm(x_ref[...], g_ref[...])
    o_ref[...] = _dot(xn, w_ref[...])


def _proj(x2d, gain, w, tm):
    rows, d = x2d.shape
    n = w.shape[1]
    return pl.pallas_call(
        _proj_kernel,
        grid=(rows // tm,),
        in_specs=[pl.BlockSpec((tm, d), lambda i: (i, 0)),
                  pl.BlockSpec((1, d), lambda i: (0, 0)),
                  pl.BlockSpec((d, n), lambda i: (0, 0))],
        out_specs=pl.BlockSpec((tm, n), lambda i: (i, 0)),
        out_shape=jax.ShapeDtypeStruct((rows, n), F32),
        compiler_params=_params(1),
        name="proj",
    )(x2d, gain, w)


def _ab_kernel(*refs, sb, L, unroll):
    assert sb % unroll == 0
    if L > SUBLANES:
        (z_ref, cos_ref, sin_ref, bias_ref, lg_ref, gna_ref, gnb_ref, c0_ref, n0_ref, m0_ref, r0_ref, tri_ref,
         mxa_ref, mxb_ref, c_ref, n_ref, m_ref, r_ref) = refs
    else:
        (z_ref, cos_ref, sin_ref, bias_ref, lg_ref, gna_ref, gnb_ref, c0_ref, n0_ref, m0_ref, r0_ref,
         mxa_ref, mxb_ref, c_ref, n_ref, m_ref, r_ref) = refs
    chunk = pl.program_id(1)

    @pl.when(chunk == 0)
    def _():
        c_ref[...] = jnp.broadcast_to(c0_ref[...], c_ref.shape)
        n_ref[...] = jnp.broadcast_to(n0_ref[...], n_ref.shape)
        m_ref[...] = jnp.broadcast_to(m0_ref[...], m_ref.shape)
        r_ref[...] = jnp.broadcast_to(r0_ref[...], r_ref.shape)

    scale = HEAD_DIM ** -0.5
    row = lax.broadcasted_iota(jnp.int32, (L, L), 0)
    col = lax.broadcasted_iota(jnp.int32, (L, L), 1)
    causal = row >= col
    eye = row == col
    rel = jnp.where(causal, row - col, 0).astype(F32)
    tcol = lax.broadcasted_iota(jnp.int32, (L, 1), 0).astype(F32)
    cosv = cos_ref[...]
    sinv = sin_ref[...]
    bias = bias_ref[...]
    gate_col = 8 * H_AB * HEAD_DIM
    ones_bf = jnp.ones((L, LANES), BF16)
    ret_tabs = []
    for h in range(H_AB):
        lg = lg_ref[h][:, :1]
        ret_tabs.append((jnp.where(causal, jnp.exp(rel * lg), 0.0),
                         jnp.broadcast_to(jnp.exp((tcol + 1.0) * lg), (L, LANES)),
                         jnp.broadcast_to(jnp.exp((L - 1.0 - tcol) * lg), (L, LANES)),
                         jnp.exp(L * lg)))

    def one_seq(s):
        new_state = []
        pre = z_ref[s, :, gate_col:gate_col + LANES] + bias
        lf = jnp.minimum(pre, 0.0) - jnp.log1p(jnp.exp(-jnp.abs(pre)))
        lf = pltpu.roll(lf, LANES - H_AB, 1)
        if L > SUBLANES:
            b = _split_dot(tri_ref[...], lf)
        else:
            b = _cumsum_rows(lf)
        a = pre - b
        a_rows = jnp.transpose(a) if L == LANES else None

        for h in range(H_AB):
            def zcol(j, h=h):
                c0 = (j * H_AB + h) * HEAD_DIM
                return z_ref[s, :, c0:c0 + HEAD_DIM]
            hs = slice(h * HEAD_DIM, (h + 1) * HEAD_DIM)

            a_col = a[:, h:h + 1]
            b_col = b[:, h:h + 1]
            if a_rows is not None:
                a_row = a_rows[h:h + 1, :]
            else:
                a_row = jnp.sum(jnp.where(eye, a_col, 0.0), axis=0, keepdims=True)
            m0 = m_ref[s, h][:, :1]
            a_caus = jnp.where(causal, a_row, -jnp.inf)
            mx = jnp.maximum(m0, jnp.max(a_caus, axis=-1, keepdims=True))
            m_t = b_col + mx
            d = jnp.exp(a_caus - mx)
            inter = jnp.broadcast_to(jnp.exp(m0 - mx), (L, LANES))
            q_f32 = zcol(0)
            q = _bf(q_f32)
            k = zcol(1) * scale
            v = _bf(zcol(2))
            sm_f32 = _dot_nt(q, k) * d
            sm = _bf(sm_f32)
            c_prev = c_ref[s, h]
            n_prev = n_ref[s, h]
            if L > SUBLANES:
                nd = jnp.dot(sm, jnp.concatenate([v, ones_bf], axis=1), preferred_element_type=F32)
                num = nd[:, :LANES] + inter * _dot(q, c_prev)
                den = nd[:, LANES:] + inter * _dot_nt(q, jnp.broadcast_to(n_prev, (HEAD_DIM, LANES)))
            else:
                num = _dot(sm, v) + inter * _dot(q, c_prev)
                den = (jnp.sum(sm_f32, axis=-1, keepdims=True)
                       + inter * jnp.sum(q_f32 * n_prev, axis=-1, keepdims=True))
            hv = num / jnp.maximum(jnp.abs(den), jnp.exp(-m_t))
            m_new = m_t[L - 1:L, :]
            b_last = b_col[L - 1:L, :]
            w_col = jnp.exp(a_col + b_last - m_new)
            keep = jnp.exp(b_last + m0 - m_new)
            kw = k * w_col
            new_state.append((c_ref, h, keep * c_prev + _dot_tn(kw, v)))
            new_state.append((n_ref, h, keep * n_prev + jnp.sum(kw, axis=0, keepdims=True)))
            new_state.append((m_ref, h, jnp.broadcast_to(m_new, (1, LANES))))
            y = _layer_norm(hv, gna_ref[:, hs])
            mxa_ref[s, :, hs] = (_sigmoid(zcol(3)) * y).astype(mxa_ref.dtype)

            decay, inner, tail, g_pow = ret_tabs[h]
            q2 = zcol(4)
            k2 = zcol(5)
            v2 = zcol(6)
            qr = q2 * cosv + pltpu.roll(q2, HEAD_DIM // 2, 1) * sinv
            kr = (k2 * cosv + pltpu.roll(k2, HEAD_DIM // 2, 1) * sinv) * scale
            s2 = _dot_nt(qr, kr) * decay
            r_prev = r_ref[s, h]
            o2 = _dot(s2, v2) + _dot(qr, r_prev) * inner
            new_state.append((r_ref, h, g_pow * r_prev + _dot_tn(kr * tail, v2)))
            y2 = _layer_norm(o2, gnb_ref[:, hs])
            gv = zcol(7)
            mxb_ref[s, :, hs] = (gv * _sigmoid(gv) * y2).astype(mxb_ref.dtype)
        return new_state

    def group(i, carry_):
        seqs = [i * unroll + j for j in range(unroll)]
        new = [one_seq(s) for s in seqs]
        for s, new_state in zip(seqs, new):
            for ref, h, val in new_state:
                ref[s, h] = val
        return carry_

    lax.fori_loop(0, sb // unroll, group, 0)


def _ab_mixer(z, cos_t, sin_t, bias, lg, gna, gnb, c0, n0, m0, r0, *, n_seq, seq_len, chunk, sb, unroll):
    n_chunks = seq_len // chunk
    width = z.shape[-1]
    hd = H_AB * HEAD_DIM
    bcast = c0.shape[0] == 1 and n_seq > 1
    sb0 = 1 if bcast else sb
    st = (lambda i: 0) if bcast else (lambda i: i)
    full2 = lambda a: pl.BlockSpec(a.shape, lambda i, c: (0, 0))
    mat_in = pl.BlockSpec((sb0, H_AB, HEAD_DIM, HEAD_DIM), lambda i, c: (st(i), 0, 0, 0))
    vec_in = pl.BlockSpec((sb0, H_AB, 1, LANES), lambda i, c: (st(i), 0, 0, 0))
    mat_out = pl.BlockSpec((sb, H_AB, HEAD_DIM, HEAD_DIM), lambda i, c: (i, 0, 0, 0))
    vec_out = pl.BlockSpec((sb, H_AB, 1, LANES), lambda i, c: (i, 0, 0, 0))
    mix_out = pl.BlockSpec((sb, chunk, hd), lambda i, c: (i, c, 0))
    tab = pl.BlockSpec((chunk, LANES), lambda i, c: (c, 0))
    in_specs = [pl.BlockSpec((sb, chunk, width), lambda i, c: (i, c, 0)), tab, tab, full2(bias),
                pl.BlockSpec(lg.shape, lambda i, c: (0, 0, 0)), full2(gna), full2(gnb),
                mat_in, vec_in, vec_in, mat_in]
    args = [z, cos_t, sin_t, bias, lg, gna, gnb, c0, n0, m0, r0]
    if chunk > SUBLANES:
        tri = jnp.asarray(np.tril(np.ones((chunk, chunk), np.float32)), BF16)
        in_specs.append(full2(tri))
        args.append(tri)
    out_shape = (jax.ShapeDtypeStruct((n_seq, seq_len, hd), BF16),
                 jax.ShapeDtypeStruct((n_seq, seq_len, hd), BF16),
                 jax.ShapeDtypeStruct((n_seq, H_AB, HEAD_DIM, HEAD_DIM), F32),
                 jax.ShapeDtypeStruct((n_seq, H_AB, 1, LANES), F32),
                 jax.ShapeDtypeStruct((n_seq, H_AB, 1, LANES), F32),
                 jax.ShapeDtypeStruct((n_seq, H_AB, HEAD_DIM, HEAD_DIM), F32))
    return pl.pallas_call(
        functools.partial(_ab_kernel, sb=sb, L=chunk, unroll=unroll),
        grid=(n_seq // sb, n_chunks),
        in_specs=in_specs,
        out_specs=(mix_out, mix_out, mat_out, vec_out, vec_out, mat_out),
        out_shape=out_shape,
        compiler_params=_params(2),
        name="ab_mixer",
    )(*args)


def _hgrn_level_tables(L):
    t = np.arange(L)[:, None]
    u = np.arange(L)[None, :]
    lvl = np.full((L, L), -1, np.int32)
    size, j = 1, 0
    while size < L:
        upper = ((t // size) % 2) == 1
        lvl[((t // size) == (u // size) + 1) & upper] = j
        size *= 2
        j += 1
    return (u <= t).astype(np.float32), lvl


def _hgrn_small_level_factors(f, r):
    one = jnp.ones_like(f)
    prev = lambda x, d: pltpu.roll(x, d, 1)
    nxt = lambda x, d: pltpu.roll(x, SUBLANES - d, 1)
    f_n1 = nxt(f, 1)
    p1 = f * prev(f, 1)
    sel = lambda idx, *vals: functools.reduce(
        lambda acc, iv: jnp.where(idx == iv[0], iv[1], acc), list(enumerate(vals))[:-1], vals[-1])
    g0 = jnp.where((r & 1) == 1, f, one)
    g1 = sel(r & 3, f_n1, one, f, p1)
    g2 = sel(r, f_n1 * nxt(p1, 3), nxt(p1, 2), f_n1, one, f, p1, p1 * prev(f, 2), p1 * prev(p1, 2))
    return [g0, g1, g2]


def _hgrn_kernel(q_ref, f_ref, i_ref, g_ref, lbl_ref, gn_ref, s0_ref, *rest, sb, L, layer, unroll):
    assert sb % unroll == 0
    if L > SUBLANES:
        tri_ref, lvl_ref, mx_ref, s_ref = rest
    else:
        mx_ref, s_ref = rest
    chunk = pl.program_id(2)

    @pl.when(chunk == 0)
    def _():
        s_ref[...] = jnp.broadcast_to(s0_ref[...], s_ref.shape)

    logits = lbl_ref[...]
    ex = jnp.exp(logits - jnp.max(logits, axis=0, keepdims=True))
    p = ex / jnp.sum(ex, axis=0, keepdims=True)
    cum = p[0:1, :]
    for r in range(1, layer + 1):
        cum = cum + p[r:r + 1, :]
    lb = cum - p[0:1, :]
    gain = gn_ref[0]

    row = lax.broadcasted_iota(jnp.int32, (L, LANES), 0)
    n_levels = L.bit_length() - 1

    def one_seq(s):
        qv = q_ref[s]
        fv = f_ref[s]
        iv = i_ref[s]
        e = jnp.exp(-jnp.abs(fv))
        rcp = 1.0 / (1.0 + e)
        pos = fv >= 0.0
        sig = jnp.where(pos, rcp, e * rcp)
        sig_neg = jnp.where(pos, e * rcp, rcp)
        f_gate = lb + (1.0 - lb) * sig
        log_f = jnp.log(f_gate)
        k = (1.0 - lb) * sig_neg
        s_prev = s_ref[s, 0]
        o = jnp.sum(qv * k, axis=-1, keepdims=True) * iv

        if L > SUBLANES:
            b = _split_dot(tri_ref[...], log_f)
            nt = L // SUBLANES
            tile3 = lambda x: x.reshape(nt, SUBLANES, x.shape[-1])
            q3, k3, b3 = tile3(qv), tile3(k), tile3(b)
            r3 = lax.broadcasted_iota(jnp.int32, (1, SUBLANES, LANES), 1)
            small = _hgrn_small_level_factors(tile3(f_gate), r3)
            acc = [jnp.zeros((SUBLANES, L), F32)] * nt
            for j in range(n_levels):
                size = 1 << j
                if size < SUBLANES:
                    xj = (jnp.where((r3 & size) != 0, q3, k3) * small[j]).reshape(L, LANES)
                    up = list(range(nt))
                    a_j = _dot_nt(xj, xj)
                else:
                    s8 = size // SUBLANES
                    up = [i for i in range(nt) if i & s8]
                    mids = {i: jnp.broadcast_to(b3[i, SUBLANES - 1:, :], (SUBLANES, LANES))
                            for i in range(s8 - 1, nt, 2 * s8)}
                    b_mid = jnp.stack([mids[(i // (2 * s8)) * 2 * s8 + s8 - 1] for i in range(nt)])
                    fac = jnp.exp(-jnp.abs(b3 - b_mid))
                    x3 = jnp.stack([q3[i] if i & s8 else k3[i] for i in range(nt)]) * fac
                    xj = x3.reshape(L, LANES)
                    a_j = _dot_nt(jnp.concatenate([x3[i] for i in up], axis=0), xj)
                for n, i in enumerate(up):
                    keep = lvl_ref[i * SUBLANES:(i + 1) * SUBLANES, :] == j
                    acc[i] = jnp.where(keep, a_j[n * SUBLANES:(n + 1) * SUBLANES, :], acc[i])
            o = o + _dot(jnp.concatenate(acc, axis=0), iv)
        else:
            b = _cumsum_rows(log_f)
            for j in range(1, L):
                dec = jnp.exp(jnp.minimum(b - pltpu.roll(b, j, 0), 0.0))
                a = jnp.sum(jnp.where(row >= j, qv * pltpu.roll(k, j, 0) * dec, 0.0), axis=-1, keepdims=True)
                o = o + a * pltpu.roll(iv, j, 0)

        o = o + _dot(qv * jnp.exp(b), s_prev)
        b_last = b[L - 1:L, :]
        e_col = jnp.transpose(jnp.broadcast_to(jnp.exp(b_last), (HEAD_DIM, LANES)))
        s_new = e_col * s_prev + _dot_tn(k * jnp.exp(b_last - b), iv)
        y = o * lax.rsqrt(jnp.mean(o * o, axis=-1, keepdims=True) + EPS) * gain
        gv = g_ref[s]
        mx_ref[s] = (y * (gv * _sigmoid(gv))).astype(mx_ref.dtype)
        return s_new

    def group(i, carry_):
        seqs = [i * unroll + j for j in range(unroll)]
        new = [one_seq(s) for s in seqs]
        for s, s_new in zip(seqs, new):
            s_ref[s, 0] = s_new
        return carry_

    lax.fori_loop(0, sb // unroll, group, 0)


def _hgrn_mixer(z, lb_logits, gain, s0, *, n_seq, seq_len, chunk, sb, layer, unroll):
    n_heads = s0.shape[1]
    n_chunks = seq_len // chunk
    bcast = s0.shape[0] == 1 and n_seq > 1
    sb0 = 1 if bcast else sb
    st = (lambda i: 0) if bcast else (lambda i: i)
    depth = lb_logits.shape[0]

    def zspec(off):
        return pl.BlockSpec((sb, chunk, LANES), lambda i, h, c, off=off: (i, c, off + h))

    in_specs = [zspec(0), zspec(n_heads), zspec(2 * n_heads), zspec(3 * n_heads),
                pl.BlockSpec((depth, LANES), lambda i, h, c: (0, h)),
                pl.BlockSpec((1, 1, LANES), lambda i, h, c: (h, 0, 0)),
                pl.BlockSpec((sb0, 1, HEAD_DIM, HEAD_DIM), lambda i, h, c: (st(i), h, 0, 0))]
    out_specs = (pl.BlockSpec((sb, chunk, LANES), lambda i, h, c: (i, c, h)),
                 pl.BlockSpec((sb, 1, HEAD_DIM, HEAD_DIM), lambda i, h, c: (i, h, 0, 0)))
    out_shape = (jax.ShapeDtypeStruct((n_seq, seq_len, n_heads * HEAD_DIM), BF16),
                 jax.ShapeDtypeStruct((n_seq, n_heads, HEAD_DIM, HEAD_DIM), F32))
    args = [z, z, z, z, lb_logits, gain, s0]
    if chunk > SUBLANES:
        tri, lvl = _hgrn_level_tables(chunk)
        in_specs += [pl.BlockSpec(tri.shape, lambda i, h, c: (0, 0)),
                     pl.BlockSpec(lvl.shape, lambda i, h, c: (0, 0))]
        args += [jnp.asarray(tri, BF16), jnp.asarray(lvl)]
    return pl.pallas_call(
        functools.partial(_hgrn_kernel, sb=sb, L=chunk, layer=layer, unroll=unroll),
        grid=(n_seq // sb, n_heads, n_chunks),
        in_specs=in_specs,
        out_specs=out_specs,
        out_shape=out_shape,
        compiler_params=_params(3),
        name="hgrn_mixer",
    )(*args)


def _ffn_kernel(*refs, n_mixed, s_blk, L, d_ff, final_norm):
    x_ref = refs[0]
    mix_refs = refs[1:1 + n_mixed]
    wo_refs = refs[1 + n_mixed:1 + 2 * n_mixed]
    (gn_ref, win_ref, cw_ref, cb_ref, wout_ref, buf_ref) = refs[1 + 2 * n_mixed:7 + 2 * n_mixed]
    pos = 7 + 2 * n_mixed
    if final_norm:
        gfin_ref = refs[pos]
        pos += 1
    xo_ref, bufo_ref, carry_ref = refs[pos:pos + 3]
    tile = pl.program_id(1)
    tm = s_blk * L

    @pl.when(tile == 0)
    def _():
        carry_ref[...] = jnp.broadcast_to(buf_ref[...], carry_ref.shape)

    x1 = x_ref[...]
    for m_ref, w_ref in zip(mix_refs, wo_refs):
        x1 = x1 + jnp.dot(m_ref[...], w_ref[...], preferred_element_type=F32)
    xn = _rms_norm(x1, gn_ref[...])
    ug = _dot(xn, win_ref[...])
    u = ug[:, :d_ff]
    gate = ug[:, d_ff:]

    t = lax.broadcasted_iota(jnp.int32, (tm, 1), 0) & (L - 1)
    if s_blk == 1:
        p0 = carry_ref[0, 0:1, :]
        p1 = carry_ref[0, 1:2, :]
    else:
        cr = carry_ref[...]
        p0 = jnp.broadcast_to(cr[:, 0:1, :], (s_blk, L, d_ff)).reshape(tm, d_ff)
        p1 = jnp.broadcast_to(cr[:, 1:2, :], (s_blk, L, d_ff)).reshape(tm, d_ff)
    prev1 = jnp.where(t == 0, p1, pltpu.roll(u, 1, 0))
    prev2 = jnp.where(t == 0, p0, jnp.where(t == 1, p1, pltpu.roll(u, 2, 0)))
    cw = cw_ref[...]
    conv = cb_ref[...] + (prev2 * cw[0:1, :] + prev1 * cw[1:2, :] + u * cw[2:3, :])
    hid = conv * _sigmoid(conv) * gate
    x2 = x1 + _dot(hid, wout_ref[...])
    if final_norm:
        xo_ref[...] = _rms_norm(x2, gfin_ref[...])
    else:
        xo_ref[...] = x2

    if s_blk == 1:
        carry_ref[0] = u[tm - (CONV_W - 1):tm, :]
    else:
        carry_ref[...] = u.reshape(s_blk, L, d_ff)[:, L - (CONV_W - 1):L, :]
    bufo_ref[...] = carry_ref[...]


def _ffn(x2d, mixed, wo_parts, gn, w_in, cw, cb, w_out, buf, gfin, *, n_seq, seq_len, tm):
    rows, d = x2d.shape
    d_ff = w_out.shape[0]
    if tm <= seq_len:
        s_blk, L = 1, tm
        tiles = seq_len // tm
        grid = (n_seq, tiles)
        rowmap = lambda s, j: (s * tiles + j, 0)
    else:
        s_blk, L = tm // seq_len, seq_len
        grid = (n_seq // s_blk, 1)
        rowmap = lambda s, j: (s, 0)
    bcast = buf.shape[0] == 1 and n_seq > 1
    bufmap = (lambda s, j: (0, 0, 0)) if bcast else (lambda s, j: (s, 0, 0))
    buf_blk = 1 if bcast else s_blk
    const = lambda s, j: (0, 0)
    n_mixed = len(mixed)
    in_specs = [pl.BlockSpec((tm, d), rowmap)]
    in_specs += [pl.BlockSpec((tm, m.shape[1]), rowmap) for m in mixed]
    in_specs += [pl.BlockSpec(w.shape, const) for w in wo_parts]
    in_specs += [pl.BlockSpec((1, d), const),
                 pl.BlockSpec(w_in.shape, const),
                 pl.BlockSpec(cw.shape, const),
                 pl.BlockSpec((1, d_ff), const),
                 pl.BlockSpec(w_out.shape, const),
                 pl.BlockSpec((buf_blk, CONV_W - 1, d_ff), bufmap)]
    args = [x2d, *mixed, *wo_parts, gn, w_in, cw, cb, w_out, buf]
    if gfin is not None:
        in_specs.append(pl.BlockSpec((1, d), const))
        args.append(gfin)
    out_specs = (pl.BlockSpec((tm, d), rowmap),
                 pl.BlockSpec((s_blk, CONV_W - 1, d_ff), lambda s, j: (s, 0, 0)))
    out_shape = (jax.ShapeDtypeStruct((rows, d), F32),
                 jax.ShapeDtypeStruct((n_seq, CONV_W - 1, d_ff), F32))
    return pl.pallas_call(
        functools.partial(_ffn_kernel, n_mixed=n_mixed, s_blk=s_blk, L=L, d_ff=d_ff,
                          final_norm=gfin is not None),
        grid=grid,
        in_specs=in_specs,
        out_specs=out_specs,
        out_shape=out_shape,
        scratch_shapes=[pltpu.VMEM((s_blk, CONV_W - 1, d_ff), F32)],
        compiler_params=_params(2),
        name="ffn",
    )(*args)


def _tile_rows(n_seq, seq_len, target):
    rows = n_seq * seq_len
    if seq_len >= target:
        return target
    return min(rows, target)


def _trunk(x, pos, st, w, *, chunk_ab, chunk_c, sb_ab, sb_c, unroll_ab, unroll_c):
    n_seq, seq_len, d = x.shape
    rows = n_seq * seq_len
    tm = _tile_rows(n_seq, seq_len, 256)
    x2d = x.reshape(rows, d)

    half = HEAD_DIM // 2
    inv = 1.0 / (ROPE_BASE ** jnp.linspace(0.0, 1.0, half, dtype=F32))
    ang = pos[:, None] * inv[None, :]
    cos_t = jnp.concatenate([jnp.cos(ang), jnp.cos(ang)], axis=-1)
    sin_t = jnp.concatenate([-jnp.sin(ang), jnp.sin(ang)], axis=-1)

    z = _proj(x2d, w["norm_mix"][0], w["w_in_ab"], tm).reshape(n_seq, seq_len, -1)
    mxa, mxb, c_new, n_new, m_new, r_new = _ab_mixer(
        z, cos_t, sin_t, w["gate_bias"], w["lg"], w["gn_a"], w["gn_b"],
        st["c"], st["n"], st["m"], st["r"], n_seq=n_seq, seq_len=seq_len, chunk=chunk_ab, sb=sb_ab,
        unroll=unroll_ab)
    hd = H_AB * HEAD_DIM
    x2d, buf0 = _ffn(x2d, [mxa.reshape(rows, hd), mxb.reshape(rows, hd)], w["w_out_ab"],
                     w["norm_ffn"][0], w["w_ffn_in"][0], w["conv_w"][0], w["conv_b"][0], w["w_ffn_out"][0],
                     st["conv"][0], None, n_seq=n_seq, seq_len=seq_len, tm=tm)

    z = _proj(x2d, w["norm_mix"][1], w["w_in_c"], tm).reshape(n_seq, seq_len, -1)
    mx, s_new = _hgrn_mixer(z, w["lb_logits"], w["gn_c"], st["s"],
                            n_seq=n_seq, seq_len=seq_len, chunk=chunk_c, sb=sb_c, layer=1, unroll=unroll_c)
    y2d, buf1 = _ffn(x2d, [mx.reshape(rows, -1)], w["w_out_c"],
                     w["norm_ffn"][1], w["w_ffn_in"][1], w["conv_w"][1], w["conv_b"][1], w["w_ffn_out"][1],
                     st["conv"][1], w["norm_final"], n_seq=n_seq, seq_len=seq_len, tm=tm)
    new_st = {"c": c_new, "n": n_new, "m": m_new, "r": r_new, "s": s_new, "conv": (buf0, buf1)}
    return y2d.reshape(n_seq, seq_len, d), new_st


def _lane_rep(v):
    return jnp.broadcast_to(v[..., None, None], v.shape + (1, LANES))


def kernel(x_prompt, x_sample, state_mlstm_C, state_mlstm_n, state_mlstm_m, state_ret_S, state_hgrn_S,
           state_ffn_conv, meta_tokens, norm_mix, w_in_ab, b_igate, b_fgate, gn_mlstm, gn_ret, w_out_ab,
           lb_logits, w_in_c, gn_hgrn, w_out_c, norm_ffn, w_ffn_in, conv_w, conv_b, w_ffn_out, norm_final):
    bp, seq, d = x_prompt.shape
    bs, dec_seq, _ = x_sample.shape
    hd = H_AB * HEAD_DIM
    n_hc = state_hgrn_S.shape[2]
    d_ff = w_ffn_out.shape[1]
    assert w_in_ab.shape[0] == 1 and w_in_c.shape[0] == 1 and norm_mix.shape[0] == 2

    wab = w_in_ab[0]
    g0 = 4 * hd
    w_ab = jnp.concatenate([wab[:, :g0], wab[:, g0 + 2 * H_AB:], wab[:, g0:g0 + 2 * H_AB],
                            jnp.zeros((d, LANES - 2 * H_AB), wab.dtype)], axis=1).astype(BF16)
    wo_ab = w_out_ab[0].astype(BF16)
    log_gamma = jnp.log1p(-jnp.exp2(-5.0 - jnp.arange(H_AB, dtype=F32)))
    gate_bias = jnp.concatenate([b_igate[0], b_fgate[0], jnp.zeros((LANES - 2 * H_AB,), F32)])[None, :]
    w = {
        "norm_mix": norm_mix[:, None, :],
        "norm_ffn": norm_ffn[:, None, :],
        "norm_final": norm_final[None, :],
        "w_in_ab": w_ab,
        "gate_bias": gate_bias,
        "lg": _lane_rep(log_gamma),
        "gn_a": gn_mlstm[0][None, :],
        "gn_b": gn_ret[0][None, :],
        "w_out_ab": [wo_ab[:hd], wo_ab[hd:]],
        "lb_logits": lb_logits,
        "w_in_c": w_in_c[0].astype(BF16),
        "gn_c": gn_hgrn[0].reshape(n_hc, 1, HEAD_DIM),
        "w_out_c": [w_out_c[0].astype(BF16)],
        "w_ffn_in": w_ffn_in.astype(BF16),
        "conv_w": conv_w,
        "conv_b": conv_b[:, None, :],
        "w_ffn_out": w_ffn_out.astype(BF16),
    }

    def zero_state(n):
        return {"c": jnp.zeros((n, H_AB, HEAD_DIM, HEAD_DIM), F32),
                "n": jnp.zeros((n, H_AB, 1, LANES), F32),
                "m": jnp.zeros((n, H_AB, 1, LANES), F32),
                "r": jnp.zeros((n, H_AB, HEAD_DIM, HEAD_DIM), F32),
                "s": jnp.zeros((n, n_hc, HEAD_DIM, HEAD_DIM), F32),
                "conv": (jnp.zeros((n, CONV_W - 1, d_ff), F32), jnp.zeros((n, CONV_W - 1, d_ff), F32))}

    _, st_meta = _trunk(meta_tokens[None].astype(F32), jnp.arange(N_META, dtype=F32), zero_state(1), w,
                        chunk_ab=N_META, chunk_c=N_META, sb_ab=1, sb_c=1, unroll_ab=1, unroll_c=1)
    pos_p = N_META + jnp.arange(seq, dtype=F32)
    y_prompt, st_p = _trunk(x_prompt, pos_p, st_meta, w, chunk_ab=128, chunk_c=128,
                            sb_ab=min(bp, 4), sb_c=min(bp, 8), unroll_ab=1, unroll_c=min(bp, 4))
    st_s = {"c": state_mlstm_C[0], "n": state_mlstm_n[0][:, :, None, :], "m": _lane_rep(state_mlstm_m[0]),
            "r": state_ret_S[0], "s": state_hgrn_S[0], "conv": (state_ffn_conv[0], state_ffn_conv[1])}
    pos_s = PAST_LEN + jnp.arange(dec_seq, dtype=F32)
    y_sample, st_s = _trunk(x_sample, pos_s, st_s, w, chunk_ab=dec_seq, chunk_c=dec_seq,
                            sb_ab=min(bs, 8), sb_c=min(bs, 32), unroll_ab=2, unroll_c=8)

    def outs(s):
        return (s["c"][None], s["n"][:, :, 0, :][None], s["m"][:, :, 0, 0][None], s["r"][None], s["s"][None],
                jnp.stack(s["conv"]))

    cp, n_p, mp, rp, sp, convp = outs(st_p)
    cs, n_s, ms, rs, ss, convs = outs(st_s)
    return (y_prompt, y_sample, cp, cs, n_p, n_s, mp, ms, rp, rs, sp, ss, convp, convs)
```

```python
import functools

import numpy as np

import jax
import jax.numpy as jnp
from jax import lax
from jax.experimental import pallas as pl
from jax.experimental.pallas import tpu as pltpu

EPS = 1e-6
N_META = 16
PAST_LEN = 16384
ROPE_BASE = 10000.0
HEAD_DIM = 128
H_AB = 4
CONV_W = 3
LANES = 128
SUBLANES = 8
VMEM_LIMIT = 56 * 1024 * 1024
F32 = jnp.float32
BF16 = jnp.bfloat16


def _bf(x):
    return x.astype(BF16)


def _dot(a, b):
    return jnp.dot(_bf(a), _bf(b), preferred_element_type=F32)


def _dot_nt(a, b):
    return lax.dot_general(_bf(a), _bf(b), (((1,), (1,)), ((), ())), preferred_element_type=F32)


def _dot_tn(a, b):
    return lax.dot_general(_bf(a), _bf(b), (((0,), (0,)), ((), ())), preferred_element_type=F32)


def _rms_norm(x, gain):
    y = x * lax.rsqrt(jnp.mean(x * x, axis=-1, keepdims=True) + EPS)
    return y * gain


def _sigmoid(x):
    return 1.0 / (1.0 + jnp.exp(-x))


def _layer_norm(x, gain, mean_mat):
    if mean_mat is None:
        mu = jnp.mean(x, axis=-1, keepdims=True)
        cen = x - mu
        var = jnp.mean(cen * cen, axis=-1, keepdims=True)
    else:
        mu = jnp.dot(_bf(x), mean_mat, preferred_element_type=F32)
        yield
        cen = x - mu
        var = jnp.dot(_bf(cen * cen), mean_mat, preferred_element_type=F32)
        yield
    return cen * lax.rsqrt(var + EPS) * gain


def _cumsum_rows(x):
    n = x.shape[0]
    row = lax.broadcasted_iota(jnp.int32, x.shape, 0)
    k = 1
    while k < n:
        x = x + jnp.where(row >= k, pltpu.roll(x, k, 0), 0.0)
        k *= 2
    return x


def _split_dot(mat_bf16, x):
    hi = _bf(x)
    lo = _bf(x - hi.astype(F32))
    n = x.shape[1]
    r = jnp.dot(mat_bf16, jnp.concatenate([hi, lo], axis=1), preferred_element_type=F32)
    return r[:, :n] + r[:, n:]


def _lock_step(gens):
    results = [None] * len(gens)
    live = list(enumerate(gens))
    while live:
        still = []
        for idx, g in live:
            try:
                next(g)
                still.append((idx, g))
            except StopIteration as stop:
                results[idx] = stop.value
        live = still
    return results


def _params(n_grid):
    return pltpu.CompilerParams(dimension_semantics=("arbitrary",) * n_grid,
                                vmem_limit_bytes=VMEM_LIMIT)


def _proj_kernel(x_ref, g_ref, w_ref, o_ref):
    xn = _rms_norm(x_ref[...], g_ref[...])
    o_ref[...] = _dot(xn, w_ref[...])


def _proj(x2d, gain, w, tm):
    rows, d = x2d.shape
    n = w.shape[1]
    return pl.pallas_call(
        _proj_kernel,
        grid=(rows // tm,),
        in_specs=[pl.BlockSpec((tm, d), lambda i: (i, 0)),
                  pl.BlockSpec((1, d), lambda i: (0, 0)),
                  pl.BlockSpec((d, n), lambda i: (0, 0))],
        out_specs=pl.BlockSpec((tm, n), lambda i: (i, 0)),
        out_shape=jax.ShapeDtypeStruct((rows, n), F32),
        compiler_params=_params(1),
        name="proj",
    )(x2d, gain, w)


def _ab_kernel(*refs, sb, L, unroll):
    assert sb % unroll == 0
    if L > SUBLANES:
        (z_ref, cos_ref, sin_ref, bias_ref, lg_ref, gna_ref, gnb_ref, c0_ref, n0_ref, m0_ref, r0_ref, tri_ref,
         mxa_ref, mxb_ref, c_ref, n_ref, m_ref, r_ref) = refs
    else:
        (z_ref, cos_ref, sin_ref, bias_ref, lg_ref, gna_ref, gnb_ref, c0_ref, n0_ref, m0_ref, r0_ref,
         mxa_ref, mxb_ref, c_ref, n_ref, m_ref, r_ref) = refs
    chunk = pl.program_id(1)

    @pl.when(chunk == 0)
    def _():
        c_ref[...] = jnp.broadcast_to(c0_ref[...], c_ref.shape)
        n_ref[...] = jnp.broadcast_to(n0_ref[...], n_ref.shape)
        m_ref[...] = jnp.broadcast_to(m0_ref[...], m_ref.shape)
        r_ref[...] = jnp.broadcast_to(r0_ref[...], r_ref.shape)

    scale = HEAD_DIM ** -0.5
    row = lax.broadcasted_iota(jnp.int32, (L, L), 0)
    col = lax.broadcasted_iota(jnp.int32, (L, L), 1)
    causal = row >= col
    eye = row == col
    rel = jnp.where(causal, row - col, 0).astype(F32)
    tcol = lax.broadcasted_iota(jnp.int32, (L, 1), 0).astype(F32)
    cosv = cos_ref[...]
    sinv = sin_ref[...]
    bias = bias_ref[...]
    gate_col = 8 * H_AB * HEAD_DIM
    ones_bf = jnp.ones((L, LANES), BF16)
    mean_mat = jnp.full((HEAD_DIM, LANES), 1.0 / HEAD_DIM, BF16) if L > SUBLANES else None
    ret_tabs = []
    for h in range(H_AB):
        lg = lg_ref[h][:, :1]
        ret_tabs.append((jnp.where(causal, jnp.exp(rel * lg), 0.0),
                         jnp.broadcast_to(jnp.exp((tcol + 1.0) * lg), (L, LANES)),
                         jnp.broadcast_to(jnp.exp((L - 1.0 - tcol) * lg), (L, LANES)),
                         jnp.exp(L * lg)))

    def gates(s):
        pre = z_ref[s, :, gate_col:gate_col + LANES] + bias
        lf = jnp.minimum(pre, 0.0) - jnp.log1p(jnp.exp(-jnp.abs(pre)))
        lf = pltpu.roll(lf, LANES - H_AB, 1)
        if L > SUBLANES:
            b = _split_dot(tri_ref[...], lf)
            yield
        else:
            b = _cumsum_rows(lf)
        a = pre - b
        a_rows = jnp.transpose(a) if L == LANES else None
        return a, b, a_rows

    def zcol(s, j, h):
        c0 = (j * H_AB + h) * HEAD_DIM
        return z_ref[s, :, c0:c0 + HEAD_DIM]

    def mlstm_head(s, h, a, b, a_rows):
        hs = slice(h * HEAD_DIM, (h + 1) * HEAD_DIM)
        a_col = a[:, h:h + 1]
        b_col = b[:, h:h + 1]
        if a_rows is not None:
            a_row = a_rows[h:h + 1, :]
        else:
            a_row = jnp.sum(jnp.where(eye, a_col, 0.0), axis=0, keepdims=True)
        m0 = m_ref[s, h][:, :1]
        a_caus = jnp.where(causal, a_row, -jnp.inf)
        a_max = jnp.max(a_caus, axis=-1, keepdims=True)
        if L > SUBLANES:
            yield
        mx = jnp.maximum(m0, a_max)
        m_t = b_col + mx
        d = jnp.exp(a_caus - mx)
        inter = jnp.broadcast_to(jnp.exp(m0 - mx), (L, LANES))
        q_f32 = zcol(s, 0, h)
        q = _bf(q_f32)
        k = zcol(s, 1, h) * scale
        v = _bf(zcol(s, 2, h))
        c_prev = c_ref[s, h]
        n_prev = n_ref[s, h]
        qk = _dot_nt(q, k)
        qc = _dot(q, c_prev)
        yield
        sm_f32 = qk * d
        sm = _bf(sm_f32)
        m_new = m_t[L - 1:L, :]
        b_last = b_col[L - 1:L, :]
        w_col = jnp.exp(a_col + b_last - m_new)
        keep = jnp.exp(b_last + m0 - m_new)
        kw = k * w_col
        if L > SUBLANES:
            nd = jnp.dot(sm, jnp.concatenate([v, ones_bf], axis=1), preferred_element_type=F32)
            qn = _dot_nt(q, jnp.broadcast_to(n_prev, (HEAD_DIM, LANES)))
            kv = _dot_tn(kw, v)
            yield
            num = nd[:, :LANES] + inter * qc
            den = nd[:, LANES:] + inter * qn
        else:
            sv = _dot(sm, v)
            kv = _dot_tn(kw, v)
            yield
            num = sv + inter * qc
            den = (jnp.sum(sm_f32, axis=-1, keepdims=True)
                   + inter * jnp.sum(q_f32 * n_prev, axis=-1, keepdims=True))
        hv = num / jnp.maximum(jnp.abs(den), jnp.exp(-m_t))
        y = yield from _layer_norm(hv, gna_ref[:, hs], mean_mat)
        mxa_ref[s, :, hs] = (_sigmoid(zcol(s, 3, h)) * y).astype(mxa_ref.dtype)
        return [(c_ref, h, keep * c_prev + kv),
                (n_ref, h, keep * n_prev + jnp.sum(kw, axis=0, keepdims=True)),
                (m_ref, h, jnp.broadcast_to(m_new, (1, LANES)))]

    def ret_head(s, h):
        hs = slice(h * HEAD_DIM, (h + 1) * HEAD_DIM)
        decay, inner, tail, g_pow = ret_tabs[h]
        q2 = zcol(s, 4, h)
        k2 = zcol(s, 5, h)
        v2 = _bf(zcol(s, 6, h))
        qr = _bf(q2 * cosv + pltpu.roll(q2, HEAD_DIM // 2, 1) * sinv)
        kr = (k2 * cosv + pltpu.roll(k2, HEAD_DIM // 2, 1) * sinv) * scale
        r_prev = r_ref[s, h]
        qk = _dot_nt(qr, kr)
        qs = _dot(qr, r_prev)
        kv = _dot_tn(kr * tail, v2)
        yield
        o2 = _dot(qk * decay, v2)
        yield
        o2 = o2 + qs * inner
        y2 = yield from _layer_norm(o2, gnb_ref[:, hs], mean_mat)
        gv = zcol(s, 7, h)
        mxb_ref[s, :, hs] = (gv * _sigmoid(gv) * y2).astype(mxb_ref.dtype)
        return [(r_ref, h, g_pow * r_prev + kv)]

    def group(i, carry_):
        seqs = [i * unroll + j for j in range(unroll)]
        prep = _lock_step([gates(s) for s in seqs])
        chains = []
        for s, (a, b, a_rows) in zip(seqs, prep):
            for h in range(H_AB):
                chains.append((s, mlstm_head(s, h, a, b, a_rows)))
                chains.append((s, ret_head(s, h)))
        new = _lock_step([g for _, g in chains])
        for (s, _), new_state in zip(chains, new):
            for ref, h, val in new_state:
                ref[s, h] = val
        return carry_

    lax.fori_loop(0, sb // unroll, group, 0)


def _ab_mixer(z, cos_t, sin_t, bias, lg, gna, gnb, c0, n0, m0, r0, *, n_seq, seq_len, chunk, sb, unroll):
    n_chunks = seq_len // chunk
    width = z.shape[-1]
    hd = H_AB * HEAD_DIM
    bcast = c0.shape[0] == 1 and n_seq > 1
    sb0 = 1 if bcast else sb
    st = (lambda i: 0) if bcast else (lambda i: i)
    full2 = lambda a: pl.BlockSpec(a.shape, lambda i, c: (0, 0))
    mat_in = pl.BlockSpec((sb0, H_AB, HEAD_DIM, HEAD_DIM), lambda i, c: (st(i), 0, 0, 0))
    vec_in = pl.BlockSpec((sb0, H_AB, 1, LANES), lambda i, c: (st(i), 0, 0, 0))
    mat_out = pl.BlockSpec((sb, H_AB, HEAD_DIM, HEAD_DIM), lambda i, c: (i, 0, 0, 0))
    vec_out = pl.BlockSpec((sb, H_AB, 1, LANES), lambda i, c: (i, 0, 0, 0))
    mix_out = pl.BlockSpec((sb, chunk, hd), lambda i, c: (i, c, 0))
    tab = pl.BlockSpec((chunk, LANES), lambda i, c: (c, 0))
    in_specs = [pl.BlockSpec((sb, chunk, width), lambda i, c: (i, c, 0)), tab, tab, full2(bias),
                pl.BlockSpec(lg.shape, lambda i, c: (0, 0, 0)), full2(gna), full2(gnb),
                mat_in, vec_in, vec_in, mat_in]
    args = [z, cos_t, sin_t, bias, lg, gna, gnb, c0, n0, m0, r0]
    if chunk > SUBLANES:
        tri = jnp.asarray(np.tril(np.ones((chunk, chunk), np.float32)), BF16)
        in_specs.append(full2(tri))
        args.append(tri)
    out_shape = (jax.ShapeDtypeStruct((n_seq, seq_len, hd), BF16),
                 jax.ShapeDtypeStruct((n_seq, seq_len, hd), BF16),
                 jax.ShapeDtypeStruct((n_seq, H_AB, HEAD_DIM, HEAD_DIM), F32),
                 jax.ShapeDtypeStruct((n_seq, H_AB, 1, LANES), F32),
                 jax.ShapeDtypeStruct((n_seq, H_AB, 1, LANES), F32),
                 jax.ShapeDtypeStruct((n_seq, H_AB, HEAD_DIM, HEAD_DIM), F32))
    return pl.pallas_call(
        functools.partial(_ab_kernel, sb=sb, L=chunk, unroll=unroll),
        grid=(n_seq // sb, n_chunks),
        in_specs=in_specs,
        out_specs=(mix_out, mix_out, mat_out, vec_out, vec_out, mat_out),
        out_shape=out_shape,
        compiler_params=_params(2),
        name="ab_mixer",
    )(*args)


def _hgrn_level_tables(L):
    t = np.arange(L)[:, None]
    u = np.arange(L)[None, :]
    lvl = np.full((L, L), -1, np.int32)
    size, j = 1, 0
    while size < L:
        upper = ((t // size) % 2) == 1
        lvl[((t // size) == (u // size) + 1) & upper] = j
        size *= 2
        j += 1
    return (u <= t).astype(np.float32), lvl


def _hgrn_small_level_factors(f, r):
    one = jnp.ones_like(f)
    prev = lambda x, d: pltpu.roll(x, d, 1)
    nxt = lambda x, d: pltpu.roll(x, SUBLANES - d, 1)
    f_n1 = nxt(f, 1)
    p1 = f * prev(f, 1)
    sel = lambda idx, *vals: functools.reduce(
        lambda acc, iv: jnp.where(idx == iv[0], iv[1], acc), list(enumerate(vals))[:-1], vals[-1])
    g0 = jnp.where((r & 1) == 1, f, one)
    g1 = sel(r & 3, f_n1, one, f, p1)
    g2 = sel(r, f_n1 * nxt(p1, 3), nxt(p1, 2), f_n1, one, f, p1, p1 * prev(f, 2), p1 * prev(p1, 2))
    return [g0, g1, g2]


def _hgrn_kernel(q_ref, f_ref, i_ref, g_ref, lbl_ref, gn_ref, s0_ref, *rest, sb, L, layer, unroll):
    assert sb % unroll == 0
    if L > SUBLANES:
        tri_ref, lvl_ref, mx_ref, s_ref = rest
    else:
        mx_ref, s_ref = rest
    chunk = pl.program_id(2)

    @pl.when(chunk == 0)
    def _():
        s_ref[...] = jnp.broadcast_to(s0_ref[...], s_ref.shape)

    logits = lbl_ref[...]
    ex = jnp.exp(logits - jnp.max(logits, axis=0, keepdims=True))
    p = ex / jnp.sum(ex, axis=0, keepdims=True)
    cum = p[0:1, :]
    for r in range(1, layer + 1):
        cum = cum + p[r:r + 1, :]
    lb = cum - p[0:1, :]
    gain = gn_ref[0]

    row = lax.broadcasted_iota(jnp.int32, (L, LANES), 0)
    n_levels = L.bit_length() - 1

    def one_seq(s):
        qv = q_ref[s]
        fv = f_ref[s]
        iv = i_ref[s]
        e = jnp.exp(-jnp.abs(fv))
        rcp = 1.0 / (1.0 + e)
        pos = fv >= 0.0
        sig = jnp.where(pos, rcp, e * rcp)
        sig_neg = jnp.where(pos, e * rcp, rcp)
        f_gate = lb + (1.0 - lb) * sig
        log_f = jnp.log(f_gate)
        k = (1.0 - lb) * sig_neg
        s_prev = s_ref[s, 0]
        diag = jnp.sum(qv * k, axis=-1, keepdims=True)

        if L > SUBLANES:
            b = _split_dot(tri_ref[...], log_f)
            yield
            nt = L // SUBLANES
            tile3 = lambda x: x.reshape(nt, SUBLANES, x.shape[-1])
            q3, k3, b3 = tile3(qv), tile3(k), tile3(b)
            r3 = lax.broadcasted_iota(jnp.int32, (1, SUBLANES, LANES), 1)
            small = _hgrn_small_level_factors(tile3(f_gate), r3)
            acc = [jnp.zeros((SUBLANES, L), F32)] * nt
            for j in range(n_levels):
                size = 1 << j
                if size < SUBLANES:
                    xj = (jnp.where((r3 & size) != 0, q3, k3) * small[j]).reshape(L, LANES)
                    up = list(range(nt))
                    a_j = _dot_nt(xj, xj)
                else:
                    s8 = size // SUBLANES
                    up = [i for i in range(nt) if i & s8]
                    mids = {i: jnp.broadcast_to(b3[i, SUBLANES - 1:, :], (SUBLANES, LANES))
                            for i in range(s8 - 1, nt, 2 * s8)}
                    b_mid = jnp.stack([mids[(i // (2 * s8)) * 2 * s8 + s8 - 1] for i in range(nt)])
                    fac = jnp.exp(-jnp.abs(b3 - b_mid))
                    x3 = jnp.stack([q3[i] if i & s8 else k3[i] for i in range(nt)]) * fac
                    xj = x3.reshape(L, LANES)
                    a_j = _dot_nt(jnp.concatenate([x3[i] for i in up], axis=0), xj)
                yield
                for n, i in enumerate(up):
                    keep = lvl_ref[i * SUBLANES:(i + 1) * SUBLANES, :] == j
                    acc[i] = jnp.where(keep, a_j[n * SUBLANES:(n + 1) * SUBLANES, :], acc[i])
            o = diag * iv + _dot(jnp.concatenate(acc, axis=0), iv)
            yield
        else:
            b = _cumsum_rows(log_f)
            pair = []
            for j in range(1, L):
                dec = jnp.exp(jnp.minimum(b - pltpu.roll(b, j, 0), 0.0))
                pair.append(jnp.sum(jnp.where(row >= j, qv * pltpu.roll(k, j, 0) * dec, 0.0),
                                    axis=-1, keepdims=True))
            yield
            o = diag * iv
            for j, a in enumerate(pair, start=1):
                o = o + a * pltpu.roll(iv, j, 0)

        o = o + _dot(qv * jnp.exp(b), s_prev)
        yield
        b_last = b[L - 1:L, :]
        e_col = jnp.transpose(jnp.broadcast_to(jnp.exp(b_last), (HEAD_DIM, LANES)))
        s_new = e_col * s_prev + _dot_tn(k * jnp.exp(b_last - b), iv)
        ms = jnp.mean(o * o, axis=-1, keepdims=True)
        yield
        y = o * lax.rsqrt(ms + EPS) * gain
        gv = g_ref[s]
        mx_ref[s] = (y * (gv * _sigmoid(gv))).astype(mx_ref.dtype)
        return s_new

    def group(i, carry_):
        seqs = [i * unroll + j for j in range(unroll)]
        new = _lock_step([one_seq(s) for s in seqs])
        for s, s_new in zip(seqs, new):
            s_ref[s, 0] = s_new
        return carry_

    lax.fori_loop(0, sb // unroll, group, 0)


def _hgrn_mixer(z, lb_logits, gain, s0, *, n_seq, seq_len, chunk, sb, layer, unroll):
    n_heads = s0.shape[1]
    n_chunks = seq_len // chunk
    bcast = s0.shape[0] == 1 and n_seq > 1
    sb0 = 1 if bcast else sb
    st = (lambda i: 0) if bcast else (lambda i: i)
    depth = lb_logits.shape[0]

    def zspec(off):
        return pl.BlockSpec((sb, chunk, LANES), lambda i, h, c, off=off: (i, c, off + h))

    in_specs = [zspec(0), zspec(n_heads), zspec(2 * n_heads), zspec(3 * n_heads),
                pl.BlockSpec((depth, LANES), lambda i, h, c: (0, h)),
                pl.BlockSpec((1, 1, LANES), lambda i, h, c: (h, 0, 0)),
                pl.BlockSpec((sb0, 1, HEAD_DIM, HEAD_DIM), lambda i, h, c: (st(i), h, 0, 0))]
    out_specs = (pl.BlockSpec((sb, chunk, LANES), lambda i, h, c: (i, c, h)),
                 pl.BlockSpec((sb, 1, HEAD_DIM, HEAD_DIM), lambda i, h, c: (i, h, 0, 0)))
    out_shape = (jax.ShapeDtypeStruct((n_seq, seq_len, n_heads * HEAD_DIM), BF16),
                 jax.ShapeDtypeStruct((n_seq, n_heads, HEAD_DIM, HEAD_DIM), F32))
    args = [z, z, z, z, lb_logits, gain, s0]
    if chunk > SUBLANES:
        tri, lvl = _hgrn_level_tables(chunk)
        in_specs += [pl.BlockSpec(tri.shape, lambda i, h, c: (0, 0)),
                     pl.BlockSpec(lvl.shape, lambda i, h, c: (0, 0))]
        args += [jnp.asarray(tri, BF16), jnp.asarray(lvl)]
    return pl.pallas_call(
        functools.partial(_hgrn_kernel, sb=sb, L=chunk, layer=layer, unroll=unroll),
        grid=(n_seq // sb, n_heads, n_chunks),
        in_specs=in_specs,
        out_specs=out_specs,
        out_shape=out_shape,
        compiler_params=_params(3),
        name="hgrn_mixer",
    )(*args)


def _ffn_kernel(*refs, n_mixed, s_blk, L, d_ff, final_norm):
    x_ref = refs[0]
    mix_refs = refs[1:1 + n_mixed]
    wo_refs = refs[1 + n_mixed:1 + 2 * n_mixed]
    (gn_ref, win_ref, cw_ref, cb_ref, wout_ref, buf_ref) = refs[1 + 2 * n_mixed:7 + 2 * n_mixed]
    pos = 7 + 2 * n_mixed
    if final_norm:
        gfin_ref = refs[pos]
        pos += 1
    xo_ref, bufo_ref, carry_ref = refs[pos:pos + 3]
    tile = pl.program_id(1)
    tm = s_blk * L

    @pl.when(tile == 0)
    def _():
        carry_ref[...] = jnp.broadcast_to(buf_ref[...], carry_ref.shape)

    x1 = x_ref[...]
    for m_ref, w_ref in zip(mix_refs, wo_refs):
        x1 = x1 + jnp.dot(m_ref[...], w_ref[...], preferred_element_type=F32)
    xn = _rms_norm(x1, gn_ref[...])
    ug = _dot(xn, win_ref[...])
    u = ug[:, :d_ff]
    gate = ug[:, d_ff:]

    t = lax.broadcasted_iota(jnp.int32, (tm, 1), 0) & (L - 1)
    if s_blk == 1:
        p0 = carry_ref[0, 0:1, :]
        p1 = carry_ref[0, 1:2, :]
    else:
        cr = carry_ref[...]
        p0 = jnp.broadcast_to(cr[:, 0:1, :], (s_blk, L, d_ff)).reshape(tm, d_ff)
        p1 = jnp.broadcast_to(cr[:, 1:2, :], (s_blk, L, d_ff)).reshape(tm, d_ff)
    prev1 = jnp.where(t == 0, p1, pltpu.roll(u, 1, 0))
    prev2 = jnp.where(t == 0, p0, jnp.where(t == 1, p1, pltpu.roll(u, 2, 0)))
    cw = cw_ref[...]
    conv = cb_ref[...] + (prev2 * cw[0:1, :] + prev1 * cw[1:2, :] + u * cw[2:3, :])
    hid = conv * _sigmoid(conv) * gate
    x2 = x1 + _dot(hid, wout_ref[...])
    if final_norm:
        xo_ref[...] = _rms_norm(x2, gfin_ref[...])
    else:
        xo_ref[...] = x2

    if s_blk == 1:
        carry_ref[0] = u[tm - (CONV_W - 1):tm, :]
    else:
        carry_ref[...] = u.reshape(s_blk, L, d_ff)[:, L - (CONV_W - 1):L, :]
    bufo_ref[...] = carry_ref[...]


def _ffn(x2d, mixed, wo_parts, gn, w_in, cw, cb, w_out, buf, gfin, *, n_seq, seq_len, tm):
    rows, d = x2d.shape
    d_ff = w_out.shape[0]
    if tm <= seq_len:
        s_blk, L = 1, tm
        tiles = seq_len // tm
        grid = (n_seq, tiles)
        rowmap = lambda s, j: (s * tiles + j, 0)
    else:
        s_blk, L = tm // seq_len, seq_len
        grid = (n_seq // s_blk, 1)
        rowmap = lambda s, j: (s, 0)
    bcast = buf.shape[0] == 1 and n_seq > 1
    bufmap = (lambda s, j: (0, 0, 0)) if bcast else (lambda s, j: (s, 0, 0))
    buf_blk = 1 if bcast else s_blk
    const = lambda s, j: (0, 0)
    n_mixed = len(mixed)
    in_specs = [pl.BlockSpec((tm, d), rowmap)]
    in_specs += [pl.BlockSpec((tm, m.shape[1]), rowmap) for m in mixed]
    in_specs += [pl.BlockSpec(w.shape, const) for w in wo_parts]
    in_specs += [pl.BlockSpec((1, d), const),
                 pl.BlockSpec(w_in.shape, const),
                 pl.BlockSpec(cw.shape, const),
                 pl.BlockSpec((1, d_ff), const),
                 pl.BlockSpec(w_out.shape, const),
                 pl.BlockSpec((buf_blk, CONV_W - 1, d_ff), bufmap)]
    args = [x2d, *mixed, *wo_parts, gn, w_in, cw, cb, w_out, buf]
    if gfin is not None:
        in_specs.append(pl.BlockSpec((1, d), const))
        args.append(gfin)
    out_specs = (pl.BlockSpec((tm, d), rowmap),
                 pl.BlockSpec((s_blk, CONV_W - 1, d_ff), lambda s, j: (s, 0, 0)))
    out_shape = (jax.ShapeDtypeStruct((rows, d), F32),
                 jax.ShapeDtypeStruct((n_seq, CONV_W - 1, d_ff), F32))
    return pl.pallas_call(
        functools.partial(_ffn_kernel, n_mixed=n_mixed, s_blk=s_blk, L=L, d_ff=d_ff,
                          final_norm=gfin is not None),
        grid=grid,
        in_specs=in_specs,
        out_specs=out_specs,
        out_shape=out_shape,
        scratch_shapes=[pltpu.VMEM((s_blk, CONV_W - 1, d_ff), F32)],
        compiler_params=_params(2),
        name="ffn",
    )(*args)


def _tile_rows(n_seq, seq_len, target):
    rows = n_seq * seq_len
    if seq_len >= target:
        return target
    return min(rows, target)


def _trunk(x, pos, st, w, *, chunk_ab, chunk_c, sb_ab, sb_c, unroll_ab, unroll_c):
    n_seq, seq_len, d = x.shape
    rows = n_seq * seq_len
    tm = _tile_rows(n_seq, seq_len, 256)
    x2d = x.reshape(rows, d)

    half = HEAD_DIM // 2
    inv = 1.0 / (ROPE_BASE ** jnp.linspace(0.0, 1.0, half, dtype=F32))
    ang = pos[:, None] * inv[None, :]
    cos_t = jnp.concatenate([jnp.cos(ang), jnp.cos(ang)], axis=-1)
    sin_t = jnp.concatenate([-jnp.sin(ang), jnp.sin(ang)], axis=-1)

    z = _proj(x2d, w["norm_mix"][0], w["w_in_ab"], tm).reshape(n_seq, seq_len, -1)
    mxa, mxb, c_new, n_new, m_new, r_new = _ab_mixer(
        z, cos_t, sin_t, w["gate_bias"], w["lg"], w["gn_a"], w["gn_b"],
        st["c"], st["n"], st["m"], st["r"], n_seq=n_seq, seq_len=seq_len, chunk=chunk_ab, sb=sb_ab,
        unroll=unroll_ab)
    hd = H_AB * HEAD_DIM
    x2d, buf0 = _ffn(x2d, [mxa.reshape(rows, hd), mxb.reshape(rows, hd)], w["w_out_ab"],
                     w["norm_ffn"][0], w["w_ffn_in"][0], w["conv_w"][0], w["conv_b"][0], w["w_ffn_out"][0],
                     st["conv"][0], None, n_seq=n_seq, seq_len=seq_len, tm=tm)

    z = _proj(x2d, w["norm_mix"][1], w["w_in_c"], tm).reshape(n_seq, seq_len, -1)
    mx, s_new = _hgrn_mixer(z, w["lb_logits"], w["gn_c"], st["s"],
                            n_seq=n_seq, seq_len=seq_len, chunk=chunk_c, sb=sb_c, layer=1, unroll=unroll_c)
    y2d, buf1 = _ffn(x2d, [mx.reshape(rows, -1)], w["w_out_c"],
                     w["norm_ffn"][1], w["w_ffn_in"][1], w["conv_w"][1], w["conv_b"][1], w["w_ffn_out"][1],
                     st["conv"][1], w["norm_final"], n_seq=n_seq, seq_len=seq_len, tm=tm)
    new_st = {"c": c_new, "n": n_new, "m": m_new, "r": r_new, "s": s_new, "conv": (buf0, buf1)}
    return y2d.reshape(n_seq, seq_len, d), new_st


def _lane_rep(v):
    return jnp.broadcast_to(v[..., None, None], v.shape + (1, LANES))


def kernel(x_prompt, x_sample, state_mlstm_C, state_mlstm_n, state_mlstm_m, state_ret_S, state_hgrn_S,
           state_ffn_conv, meta_tokens, norm_mix, w_in_ab, b_igate, b_fgate, gn_mlstm, gn_ret, w_out_ab,
           lb_logits, w_in_c, gn_hgrn, w_out_c, norm_ffn, w_ffn_in, conv_w, conv_b, w_ffn_out, norm_final):
    bp, seq, d = x_prompt.shape
    bs, dec_seq, _ = x_sample.shape
    hd = H_AB * HEAD_DIM
    n_hc = state_hgrn_S.shape[2]
    d_ff = w_ffn_out.shape[1]
    assert w_in_ab.shape[0] == 1 and w_in_c.shape[0] == 1 and norm_mix.shape[0] == 2

    wab = w_in_ab[0]
    g0 = 4 * hd
    w_ab = jnp.concatenate([wab[:, :g0], wab[:, g0 + 2 * H_AB:], wab[:, g0:g0 + 2 * H_AB],
                            jnp.zeros((d, LANES - 2 * H_AB), wab.dtype)], axis=1).astype(BF16)
    wo_ab = w_out_ab[0].astype(BF16)
    log_gamma = jnp.log1p(-jnp.exp2(-5.0 - jnp.arange(H_AB, dtype=F32)))
    gate_bias = jnp.concatenate([b_igate[0], b_fgate[0], jnp.zeros((LANES - 2 * H_AB,), F32)])[None, :]
    w = {
        "norm_mix": norm_mix[:, None, :],
        "norm_ffn": norm_ffn[:, None, :],
        "norm_final": norm_final[None, :],
        "w_in_ab": w_ab,
        "gate_bias": gate_bias,
        "lg": _lane_rep(log_gamma),
        "gn_a": gn_mlstm[0][None, :],
        "gn_b": gn_ret[0][None, :],
        "w_out_ab": [wo_ab[:hd], wo_ab[hd:]],
        "lb_logits": lb_logits,
        "w_in_c": w_in_c[0].astype(BF16),
        "gn_c": gn_hgrn[0].reshape(n_hc, 1, HEAD_DIM),
        "w_out_c": [w_out_c[0].astype(BF16)],
        "w_ffn_in": w_ffn_in.astype(BF16),
        "conv_w": conv_w,
        "conv_b": conv_b[:, None, :],
        "w_ffn_out": w_ffn_out.astype(BF16),
    }

    def zero_state(n):
        return {"c": jnp.zeros((n, H_AB, HEAD_DIM, HEAD_DIM), F32),
                "n": jnp.zeros((n, H_AB, 1, LANES), F32),
                "m": jnp.zeros((n, H_AB, 1, LANES), F32),
                "r": jnp.zeros((n, H_AB, HEAD_DIM, HEAD_DIM), F32),
                "s": jnp.zeros((n, n_hc, HEAD_DIM, HEAD_DIM), F32),
                "conv": (jnp.zeros((n, CONV_W - 1, d_ff), F32), jnp.zeros((n, CONV_W - 1, d_ff), F32))}

    _, st_meta = _trunk(meta_tokens[None].astype(F32), jnp.arange(N_META, dtype=F32), zero_state(1), w,
                        chunk_ab=N_META, chunk_c=N_META, sb_ab=1, sb_c=1, unroll_ab=1, unroll_c=1)
    pos_p = N_META + jnp.arange(seq, dtype=F32)
    y_prompt, st_p = _trunk(x_prompt, pos_p, st_meta, w, chunk_ab=128, chunk_c=128,
                            sb_ab=min(bp, 4), sb_c=min(bp, 8), unroll_ab=min(bp, 2), unroll_c=min(bp, 4))
    st_s = {"c": state_mlstm_C[0], "n": state_mlstm_n[0][:, :, None, :], "m": _lane_rep(state_mlstm_m[0]),
            "r": state_ret_S[0], "s": state_hgrn_S[0], "conv": (state_ffn_conv[0], state_ffn_conv[1])}
    pos_s = PAST_LEN + jnp.arange(dec_seq, dtype=F32)
    y_sample, st_s = _trunk(x_sample, pos_s, st_s, w, chunk_ab=dec_seq, chunk_c=dec_seq,
                            sb_ab=min(bs, 8), sb_c=min(bs, 32), unroll_ab=min(bs, 2), unroll_c=8)

    def outs(s):
        return (s["c"][None], s["n"][:, :, 0, :][None], s["m"][:, :, 0, 0][None], s["r"][None], s["s"][None],
                jnp.stack(s["conv"]))

    cp, n_p, mp, rp, sp, convp = outs(st_p)
    cs, n_s, ms, rs, ss, convs = outs(st_s)
    return (y_prompt, y_sample, cp, cs, n_p, n_s, mp, ms, rp, rs, sp, ss, convp, convs)
```

```python
import functools

import numpy as np

import jax
import jax.numpy as jnp
from jax import lax
from jax.experimental import pallas as pl
from jax.experimental.pallas import tpu as pltpu

EPS = 1e-6
N_META = 16
PAST_LEN = 16384
ROPE_BASE = 10000.0
HEAD_DIM = 128
H_AB = 4
CONV_W = 3
LANES = 128
SUBLANES = 8
VMEM_LIMIT = 56 * 1024 * 1024
F32 = jnp.float32
BF16 = jnp.bfloat16


def _bf(x):
    return x.astype(BF16)


def _dot(a, b):
    return jnp.dot(_bf(a), _bf(b), preferred_element_type=F32)


def _dot_nt(a, b):
    return lax.dot_general(_bf(a), _bf(b), (((1,), (1,)), ((), ())), preferred_element_type=F32)


def _dot_tn(a, b):
    return lax.dot_general(_bf(a), _bf(b), (((0,), (0,)), ((), ())), preferred_element_type=F32)


def _rms_norm(x, gain):
    y = x * lax.rsqrt(jnp.mean(x * x, axis=-1, keepdims=True) + EPS)
    return y * gain


def _sigmoid(x):
    return 1.0 / (1.0 + jnp.exp(-x))


def _layer_norm(x, gain, mean_mat):
    if mean_mat is None:
        mu = jnp.mean(x, axis=-1, keepdims=True)
        cen = x - mu
        var = jnp.mean(cen * cen, axis=-1, keepdims=True)
    else:
        mu = jnp.dot(_bf(x), mean_mat, preferred_element_type=F32)
        yield
        cen = x - mu
        var = jnp.dot(_bf(cen * cen), mean_mat, preferred_element_type=F32)
        yield
    return cen * lax.rsqrt(var + EPS) * gain


def _cumsum_rows(x):
    n = x.shape[0]
    row = lax.broadcasted_iota(jnp.int32, x.shape, 0)
    k = 1
    while k < n:
        x = x + jnp.where(row >= k, pltpu.roll(x, k, 0), 0.0)
        k *= 2
    return x


def _split_dot(mat_bf16, x):
    hi = _bf(x)
    lo = _bf(x - hi.astype(F32))
    n = x.shape[1]
    r = jnp.dot(mat_bf16, jnp.concatenate([hi, lo], axis=1), preferred_element_type=F32)
    return r[:, :n] + r[:, n:]


def _lock_step(gens):
    results = [None] * len(gens)
    live = list(enumerate(gens))
    while live:
        still = []
        for idx, g in live:
            try:
                next(g)
                still.append((idx, g))
            except StopIteration as stop:
                results[idx] = stop.value
        live = still
    return results


def _params(n_grid):
    return pltpu.CompilerParams(dimension_semantics=("arbitrary",) * n_grid,
                                vmem_limit_bytes=VMEM_LIMIT)


def _ab_kernel(*refs, sb, L, unroll, n_chunks):
    assert sb % unroll == 0
    refs = list(refs)
    xc_ref = refs.pop(0)
    xnext_ref = refs.pop(0) if n_chunks > 1 else None
    (gnm_ref, w_ref, cos_ref, sin_ref, bias_ref, lg_ref, gna_ref, gnb_ref,
     c0_ref, n0_ref, m0_ref, r0_ref) = [refs.pop(0) for _ in range(12)]
    if L > SUBLANES:
        tri_ref = refs.pop(0)
    mxa_ref, mxb_ref, c_ref, n_ref, m_ref, r_ref, za_ref = [refs.pop(0) for _ in range(7)]
    if n_chunks > 1:
        zb_ref, xnb_ref = refs
    chunk = pl.program_id(1)
    rows_all = sb * L
    d_model = xc_ref.shape[-1]

    @pl.when(chunk == 0)
    def _():
        c_ref[...] = jnp.broadcast_to(c0_ref[...], c_ref.shape)
        n_ref[...] = jnp.broadcast_to(n0_ref[...], n_ref.shape)
        m_ref[...] = jnp.broadcast_to(m0_ref[...], m_ref.shape)
        r_ref[...] = jnp.broadcast_to(r0_ref[...], r_ref.shape)
        xn0 = _rms_norm(xc_ref[...].reshape(rows_all, d_model), gnm_ref[...])
        za_ref[...] = _dot(xn0, w_ref[...])

    if n_chunks > 1:
        xnb_ref[...] = _bf(_rms_norm(xnext_ref[...].reshape(rows_all, d_model), gnm_ref[...]))
    cur = {}

    def seq_rows(s):
        return pl.ds(pl.multiple_of(s * L, L), L)

    def project_next(s, c0, c1):
        if cur["zn"] is not None:
            rows = seq_rows(s)
            cur["zn"][rows, c0:c1] = jnp.dot(xnb_ref[rows, :], w_ref[:, c0:c1], preferred_element_type=F32)

    scale = HEAD_DIM ** -0.5
    row = lax.broadcasted_iota(jnp.int32, (L, L), 0)
    col = lax.broadcasted_iota(jnp.int32, (L, L), 1)
    causal = row >= col
    eye = row == col
    rel = jnp.where(causal, row - col, 0).astype(F32)
    tcol = lax.broadcasted_iota(jnp.int32, (L, 1), 0).astype(F32)
    cosv = cos_ref[...]
    sinv = sin_ref[...]
    bias = bias_ref[...]
    gate_col = 8 * H_AB * HEAD_DIM
    ones_bf = jnp.ones((L, LANES), BF16)
    mean_mat = jnp.full((HEAD_DIM, LANES), 1.0 / HEAD_DIM, BF16) if L > SUBLANES else None
    ret_tabs = []
    for h in range(H_AB):
        lg = lg_ref[h][:, :1]
        ret_tabs.append((jnp.where(causal, jnp.exp(rel * lg), 0.0),
                         jnp.broadcast_to(jnp.exp((tcol + 1.0) * lg), (L, LANES)),
                         jnp.broadcast_to(jnp.exp((L - 1.0 - tcol) * lg), (L, LANES)),
                         jnp.exp(L * lg)))

    def gates(s):
        project_next(s, gate_col, gate_col + LANES)
        pre = cur["zc"][seq_rows(s), gate_col:gate_col + LANES] + bias
        lf = jnp.minimum(pre, 0.0) - jnp.log1p(jnp.exp(-jnp.abs(pre)))
        lf = pltpu.roll(lf, LANES - H_AB, 1)
        if L > SUBLANES:
            b = _split_dot(tri_ref[...], lf)
            yield
        else:
            b = _cumsum_rows(lf)
        a = pre - b
        a_rows = jnp.transpose(a) if L == LANES else None
        return a, b, a_rows

    def zcol(s, j, h):
        c0 = (h * 8 + j) * HEAD_DIM
        return cur["zc"][seq_rows(s), c0:c0 + HEAD_DIM]

    def mlstm_head(s, h, a, b, a_rows):
        project_next(s, h * 8 * HEAD_DIM, (h * 8 + 4) * HEAD_DIM)
        if cur["zn"] is not None:
            yield
        hs = slice(h * HEAD_DIM, (h + 1) * HEAD_DIM)
        a_col = a[:, h:h + 1]
        b_col = b[:, h:h + 1]
        if a_rows is not None:
            a_row = a_rows[h:h + 1, :]
        else:
            a_row = jnp.sum(jnp.where(eye, a_col, 0.0), axis=0, keepdims=True)
        m0 = m_ref[s, h][:, :1]
        a_caus = jnp.where(causal, a_row, -jnp.inf)
        a_max = jnp.max(a_caus, axis=-1, keepdims=True)
        if L > SUBLANES:
            yield
        mx = jnp.maximum(m0, a_max)
        m_t = b_col + mx
        d = jnp.exp(a_caus - mx)
        inter = jnp.broadcast_to(jnp.exp(m0 - mx), (L, LANES))
        q_f32 = zcol(s, 0, h)
        q = _bf(q_f32)
        k = zcol(s, 1, h) * scale
        v = _bf(zcol(s, 2, h))
        c_prev = c_ref[s, h]
        n_prev = n_ref[s, h]
        qk = _dot_nt(q, k)
        qc = _dot(q, c_prev)
        yield
        sm_f32 = qk * d
        sm = _bf(sm_f32)
        m_new = m_t[L - 1:L, :]
        b_last = b_col[L - 1:L, :]
        w_col = jnp.exp(a_col + b_last - m_new)
        keep = jnp.exp(b_last + m0 - m_new)
        kw = k * w_col
        if L > SUBLANES:
            nd = jnp.dot(sm, jnp.concatenate([v, ones_bf], axis=1), preferred_element_type=F32)
            qn = _dot_nt(q, jnp.broadcast_to(n_prev, (HEAD_DIM, LANES)))
            kv = _dot_tn(kw, v)
            yield
            num = nd[:, :LANES] + inter * qc
            den = nd[:, LANES:] + inter * qn
        else:
            sv = _dot(sm, v)
            kv = _dot_tn(kw, v)
            yield
            num = sv + inter * qc
            den = (jnp.sum(sm_f32, axis=-1, keepdims=True)
                   + inter * jnp.sum(q_f32 * n_prev, axis=-1, keepdims=True))
        hv = num / jnp.maximum(jnp.abs(den), jnp.exp(-m_t))
        y = yield from _layer_norm(hv, gna_ref[:, hs], mean_mat)
        mxa_ref[s, :, hs] = (_sigmoid(zcol(s, 3, h)) * y).astype(mxa_ref.dtype)
        return [(c_ref, h, keep * c_prev + kv),
                (n_ref, h, keep * n_prev + jnp.sum(kw, axis=0, keepdims=True)),
                (m_ref, h, jnp.broadcast_to(m_new, (1, LANES)))]

    def ret_head(s, h):
        project_next(s, (h * 8 + 4) * HEAD_DIM, (h + 1) * 8 * HEAD_DIM)
        if cur["zn"] is not None:
            yield
        hs = slice(h * HEAD_DIM, (h + 1) * HEAD_DIM)
        decay, inner, tail, g_pow = ret_tabs[h]
        q2 = zcol(s, 4, h)
        k2 = zcol(s, 5, h)
        v2 = _bf(zcol(s, 6, h))
        qr = _bf(q2 * cosv + pltpu.roll(q2, HEAD_DIM // 2, 1) * sinv)
        kr = (k2 * cosv + pltpu.roll(k2, HEAD_DIM // 2, 1) * sinv) * scale
        r_prev = r_ref[s, h]
        qk = _dot_nt(qr, kr)
        qs = _dot(qr, r_prev)
        kv = _dot_tn(kr * tail, v2)
        yield
        o2 = _dot(qk * decay, v2)
        yield
        o2 = o2 + qs * inner
        y2 = yield from _layer_norm(o2, gnb_ref[:, hs], mean_mat)
        gv = zcol(s, 7, h)
        mxb_ref[s, :, hs] = (gv * _sigmoid(gv) * y2).astype(mxb_ref.dtype)
        return [(r_ref, h, g_pow * r_prev + kv)]

    def group(i, carry_):
        seqs = [i * unroll + j for j in range(unroll)]
        prep = _lock_step([gates(s) for s in seqs])
        chains = []
        for s, (a, b, a_rows) in zip(seqs, prep):
            for h in range(H_AB):
                chains.append((s, mlstm_head(s, h, a, b, a_rows)))
                chains.append((s, ret_head(s, h)))
        new = _lock_step([g for _, g in chains])
        for (s, _), new_state in zip(chains, new):
            for ref, h, val in new_state:
                ref[s, h] = val
        return carry_

    def step(zc_ref, zn_ref):
        cur["zc"], cur["zn"] = zc_ref, zn_ref
        lax.fori_loop(0, sb // unroll, lambda i, carry_: group(i, carry_), 0)

    if n_chunks > 1:
        @pl.when(chunk % 2 == 0)
        def _():
            step(za_ref, zb_ref)

        @pl.when(chunk % 2 == 1)
        def _():
            step(zb_ref, za_ref)
    else:
        step(za_ref, None)


def _ab_mixer(x, gnm, w, cos_t, sin_t, bias, lg, gna, gnb, c0, n0, m0, r0, *, chunk, sb, unroll):
    n_seq, seq_len, d = x.shape
    n_chunks = seq_len // chunk
    width = w.shape[1]
    hd = H_AB * HEAD_DIM
    bcast = c0.shape[0] == 1 and n_seq > 1
    sb0 = 1 if bcast else sb
    st = (lambda i: 0) if bcast else (lambda i: i)
    full2 = lambda a: pl.BlockSpec(a.shape, lambda i, c: (0, 0))
    mat_in = pl.BlockSpec((sb0, H_AB, HEAD_DIM, HEAD_DIM), lambda i, c: (st(i), 0, 0, 0))
    vec_in = pl.BlockSpec((sb0, H_AB, 1, LANES), lambda i, c: (st(i), 0, 0, 0))
    mat_out = pl.BlockSpec((sb, H_AB, HEAD_DIM, HEAD_DIM), lambda i, c: (i, 0, 0, 0))
    vec_out = pl.BlockSpec((sb, H_AB, 1, LANES), lambda i, c: (i, 0, 0, 0))
    mix_out = pl.BlockSpec((sb, chunk, hd), lambda i, c: (i, c, 0))
    tab = pl.BlockSpec((chunk, LANES), lambda i, c: (c, 0))
    in_specs = [pl.BlockSpec((sb, chunk, d), lambda i, c: (i, c, 0))]
    args = [x]
    if n_chunks > 1:
        in_specs.append(pl.BlockSpec((sb, chunk, d), lambda i, c: (i, jnp.minimum(c + 1, n_chunks - 1), 0)))
        args.append(x)
    w_spec = pl.BlockSpec(w.shape, lambda i, c: (0, 0), pipeline_mode=pl.Buffered(1))
    in_specs += [full2(gnm), w_spec, tab, tab, full2(bias),
                 pl.BlockSpec(lg.shape, lambda i, c: (0, 0, 0)), full2(gna), full2(gnb),
                 mat_in, vec_in, vec_in, mat_in]
    args += [gnm, w, cos_t, sin_t, bias, lg, gna, gnb, c0, n0, m0, r0]
    if chunk > SUBLANES:
        tri = jnp.asarray(np.tril(np.ones((chunk, chunk), np.float32)), BF16)
        in_specs.append(full2(tri))
        args.append(tri)
    scratch = [pltpu.VMEM((sb * chunk, width), F32)]
    if n_chunks > 1:
        scratch += [pltpu.VMEM((sb * chunk, width), F32), pltpu.VMEM((sb * chunk, d), BF16)]
    out_shape = (jax.ShapeDtypeStruct((n_seq, seq_len, hd), BF16),
                 jax.ShapeDtypeStruct((n_seq, seq_len, hd), BF16),
                 jax.ShapeDtypeStruct((n_seq, H_AB, HEAD_DIM, HEAD_DIM), F32),
                 jax.ShapeDtypeStruct((n_seq, H_AB, 1, LANES), F32),
                 jax.ShapeDtypeStruct((n_seq, H_AB, 1, LANES), F32),
                 jax.ShapeDtypeStruct((n_seq, H_AB, HEAD_DIM, HEAD_DIM), F32))
    return pl.pallas_call(
        functools.partial(_ab_kernel, sb=sb, L=chunk, unroll=unroll, n_chunks=n_chunks),
        grid=(n_seq // sb, n_chunks),
        in_specs=in_specs,
        out_specs=(mix_out, mix_out, mat_out, vec_out, vec_out, mat_out),
        out_shape=out_shape,
        scratch_shapes=scratch,
        compiler_params=_params(2),
        name="ab_mixer",
    )(*args)


def _hgrn_level_tables(L):
    t = np.arange(L)[:, None]
    u = np.arange(L)[None, :]
    lvl = np.full((L, L), -1, np.int32)
    size, j = 1, 0
    while size < L:
        upper = ((t // size) % 2) == 1
        lvl[((t // size) == (u // size) + 1) & upper] = j
        size *= 2
        j += 1
    return (u <= t).astype(np.float32), lvl


def _hgrn_small_level_factors(f, r):
    one = jnp.ones_like(f)
    prev = lambda x, d: pltpu.roll(x, d, 1)
    nxt = lambda x, d: pltpu.roll(x, SUBLANES - d, 1)
    f_n1 = nxt(f, 1)
    p1 = f * prev(f, 1)
    sel = lambda idx, *vals: functools.reduce(
        lambda acc, iv: jnp.where(idx == iv[0], iv[1], acc), list(enumerate(vals))[:-1], vals[-1])
    g0 = jnp.where((r & 1) == 1, f, one)
    g1 = sel(r & 3, f_n1, one, f, p1)
    g2 = sel(r, f_n1 * nxt(p1, 3), nxt(p1, 2), f_n1, one, f, p1, p1 * prev(f, 2), p1 * prev(p1, 2))
    return [g0, g1, g2]


def _hgrn_kernel(*refs, sb, L, layer, n_heads, n_chunks, group_heads):
    refs = list(refs)
    xc_ref = refs.pop(0)
    xnext_ref = refs.pop(0) if n_chunks > 1 else None
    gnm_ref, w_ref, lbl_ref, gn_ref, s0_ref = [refs.pop(0) for _ in range(5)]
    if L > SUBLANES:
        tri_ref, lvl_ref = refs.pop(0), refs.pop(0)
    mx_ref, s_ref, za_ref = refs.pop(0), refs.pop(0), refs.pop(0)
    if n_chunks > 1:
        zb_ref, xnb_ref = refs
    chunk = pl.program_id(1)
    rows_all = sb * L
    d_model = xc_ref.shape[-1]

    @pl.when(chunk == 0)
    def _():
        s_ref[...] = jnp.broadcast_to(s0_ref[...], s_ref.shape)
        xn0 = _rms_norm(xc_ref[...].reshape(rows_all, d_model), gnm_ref[...])
        za_ref[...] = _dot(xn0, w_ref[...])

    if n_chunks > 1:
        xnb_ref[...] = _bf(_rms_norm(xnext_ref[...].reshape(rows_all, d_model), gnm_ref[...]))

    logits = lbl_ref[...]
    ex = jnp.exp(logits - jnp.max(logits, axis=0, keepdims=True))
    p = ex / jnp.sum(ex, axis=0, keepdims=True)
    cum = p[0:1, :]
    for r in range(1, layer + 1):
        cum = cum + p[r:r + 1, :]
    lb_all = cum - p[0:1, :]

    row = lax.broadcasted_iota(jnp.int32, (L, LANES), 0)
    n_levels = L.bit_length() - 1
    unit_w = 4 * HEAD_DIM

    def unit(s, h, zc_ref, zn_ref):
        rows = pl.ds(pl.multiple_of(s * L, L), L)
        base = h * unit_w
        hs = slice(h * HEAD_DIM, (h + 1) * HEAD_DIM)
        if zn_ref is not None:
            zn_ref[rows, base:base + unit_w] = jnp.dot(xnb_ref[rows, :], w_ref[:, base:base + unit_w],
                                                      preferred_element_type=F32)
            yield
        lb = lb_all[:, hs]
        qv = zc_ref[rows, base:base + HEAD_DIM]
        fv = zc_ref[rows, base + HEAD_DIM:base + 2 * HEAD_DIM]
        iv = zc_ref[rows, base + 2 * HEAD_DIM:base + 3 * HEAD_DIM]
        e = jnp.exp(-jnp.abs(fv))
        rcp = 1.0 / (1.0 + e)
        pos = fv >= 0.0
        sig = jnp.where(pos, rcp, e * rcp)
        sig_neg = jnp.where(pos, e * rcp, rcp)
        f_gate = lb + (1.0 - lb) * sig
        log_f = jnp.log(f_gate)
        k = (1.0 - lb) * sig_neg
        s_prev = s_ref[s, h]
        diag = jnp.sum(qv * k, axis=-1, keepdims=True)

        if L > SUBLANES:
            b = _split_dot(tri_ref[...], log_f)
            yield
            nt = L // SUBLANES
            tile3 = lambda x: x.reshape(nt, SUBLANES, x.shape[-1])
            q3, k3, b3 = tile3(qv), tile3(k), tile3(b)
            r3 = lax.broadcasted_iota(jnp.int32, (1, SUBLANES, LANES), 1)
            small = _hgrn_small_level_factors(tile3(f_gate), r3)
            acc = [jnp.zeros((SUBLANES, L), F32)] * nt
            for j in range(n_levels):
                size = 1 << j
                if size < SUBLANES:
                    xj = (jnp.where((r3 & size) != 0, q3, k3) * small[j]).reshape(L, LANES)
                    up = list(range(nt))
                    a_j = _dot_nt(xj, xj)
                else:
                    s8 = size // SUBLANES
                    up = [i for i in range(nt) if i & s8]
                    mids = {i: jnp.broadcast_to(b3[i, SUBLANES - 1:, :], (SUBLANES, LANES))
                            for i in range(s8 - 1, nt, 2 * s8)}
                    b_mid = jnp.stack([mids[(i // (2 * s8)) * 2 * s8 + s8 - 1] for i in range(nt)])
                    fac = jnp.exp(-jnp.abs(b3 - b_mid))
                    x3 = jnp.stack([q3[i] if i & s8 else k3[i] for i in range(nt)]) * fac
                    xj = x3.reshape(L, LANES)
                    a_j = _dot_nt(jnp.concatenate([x3[i] for i in up], axis=0), xj)
                yield
                for n, i in enumerate(up):
                    keep = lvl_ref[i * SUBLANES:(i + 1) * SUBLANES, :] == j
                    acc[i] = jnp.where(keep, a_j[n * SUBLANES:(n + 1) * SUBLANES, :], acc[i])
            o = diag * iv + _dot(jnp.concatenate(acc, axis=0), iv)
            yield
        else:
            b = _cumsum_rows(log_f)
            pair = []
            for j in range(1, L):
                dec = jnp.exp(jnp.minimum(b - pltpu.roll(b, j, 0), 0.0))
                pair.append(jnp.sum(jnp.where(row >= j, qv * pltpu.roll(k, j, 0) * dec, 0.0),
                                    axis=-1, keepdims=True))
            yield
            o = diag * iv
            for j, a in enumerate(pair, start=1):
                o = o + a * pltpu.roll(iv, j, 0)

        o = o + _dot(qv * jnp.exp(b), s_prev)
        yield
        b_last = b[L - 1:L, :]
        e_col = jnp.transpose(jnp.broadcast_to(jnp.exp(b_last), (HEAD_DIM, LANES)))
        s_new = e_col * s_prev + _dot_tn(k * jnp.exp(b_last - b), iv)
        ms = jnp.mean(o * o, axis=-1, keepdims=True)
        yield
        y = o * lax.rsqrt(ms + EPS) * gn_ref[:, hs]
        gv = zc_ref[rows, base + 3 * HEAD_DIM:base + 4 * HEAD_DIM]
        mx_ref[s, :, hs] = (y * (gv * _sigmoid(gv))).astype(mx_ref.dtype)
        return s_new

    def step(zc_ref, zn_ref):
        def per_seq(s, carry_):
            for h0 in range(0, n_heads, group_heads):
                heads = list(range(h0, h0 + group_heads))
                new = _lock_step([unit(s, h, zc_ref, zn_ref) for h in heads])
                for h, s_new in zip(heads, new):
                    s_ref[s, h] = s_new
            return carry_
        lax.fori_loop(0, sb, per_seq, 0)

    if n_chunks > 1:
        @pl.when(chunk % 2 == 0)
        def _():
            step(za_ref, zb_ref)

        @pl.when(chunk % 2 == 1)
        def _():
            step(zb_ref, za_ref)
    else:
        step(za_ref, None)


def _hgrn_mixer(x, gnm, w, lb_logits, gain, s0, *, chunk, sb, layer, group_heads):
    n_seq, seq_len, d = x.shape
    n_heads = s0.shape[1]
    n_chunks = seq_len // chunk
    width = n_heads * HEAD_DIM
    bcast = s0.shape[0] == 1 and n_seq > 1
    sb0 = 1 if bcast else sb
    st = (lambda i: 0) if bcast else (lambda i: i)
    full2 = lambda a: pl.BlockSpec(a.shape, lambda i, c: (0, 0))
    in_specs = [pl.BlockSpec((sb, chunk, d), lambda i, c: (i, c, 0))]
    args = [x]
    if n_chunks > 1:
        in_specs.append(pl.BlockSpec((sb, chunk, d), lambda i, c: (i, jnp.minimum(c + 1, n_chunks - 1), 0)))
        args.append(x)
    w_spec = pl.BlockSpec(w.shape, lambda i, c: (0, 0), pipeline_mode=pl.Buffered(1))
    in_specs += [full2(gnm), w_spec, full2(lb_logits), full2(gain),
                 pl.BlockSpec((sb0, n_heads, HEAD_DIM, HEAD_DIM), lambda i, c: (st(i), 0, 0, 0))]
    args += [gnm, w, lb_logits, gain, s0]
    if chunk > SUBLANES:
        tri, lvl = _hgrn_level_tables(chunk)
        in_specs += [pl.BlockSpec(tri.shape, lambda i, c: (0, 0)), pl.BlockSpec(lvl.shape, lambda i, c: (0, 0))]
        args += [jnp.asarray(tri, BF16), jnp.asarray(lvl)]
    out_specs = (pl.BlockSpec((sb, chunk, width), lambda i, c: (i, c, 0)),
                 pl.BlockSpec((sb, n_heads, HEAD_DIM, HEAD_DIM), lambda i, c: (i, 0, 0, 0)))
    out_shape = (jax.ShapeDtypeStruct((n_seq, seq_len, width), BF16),
                 jax.ShapeDtypeStruct((n_seq, n_heads, HEAD_DIM, HEAD_DIM), F32))
    scratch = [pltpu.VMEM((sb * chunk, w.shape[1]), F32)]
    if n_chunks > 1:
        scratch += [pltpu.VMEM((sb * chunk, w.shape[1]), F32), pltpu.VMEM((sb * chunk, d), BF16)]
    return pl.pallas_call(
        functools.partial(_hgrn_kernel, sb=sb, L=chunk, layer=layer, n_heads=n_heads, n_chunks=n_chunks,
                          group_heads=group_heads),
        grid=(n_seq // sb, n_chunks),
        in_specs=in_specs,
        out_specs=out_specs,
        out_shape=out_shape,
        scratch_shapes=scratch,
        compiler_params=_params(2),
        name="hgrn_mixer",
    )(*args)


def _ffn_kernel(*refs, n_mixed, s_blk, L, d_ff, final_norm):
    x_ref = refs[0]
    mix_refs = refs[1:1 + n_mixed]
    wo_refs = refs[1 + n_mixed:1 + 2 * n_mixed]
    (gn_ref, win_ref, cw_ref, cb_ref, wout_ref, buf_ref) = refs[1 + 2 * n_mixed:7 + 2 * n_mixed]
    pos = 7 + 2 * n_mixed
    if final_norm:
        gfin_ref = refs[pos]
        pos += 1
    xo_ref, bufo_ref, carry_ref = refs[pos:pos + 3]
    tile = pl.program_id(1)
    tm = s_blk * L

    @pl.when(tile == 0)
    def _():
        carry_ref[...] = jnp.broadcast_to(buf_ref[...], carry_ref.shape)

    x1 = x_ref[...]
    for m_ref, w_ref in zip(mix_refs, wo_refs):
        x1 = x1 + jnp.dot(m_ref[...], w_ref[...], preferred_element_type=F32)
    xn = _rms_norm(x1, gn_ref[...])
    ug = _dot(xn, win_ref[...])
    u = ug[:, :d_ff]
    gate = ug[:, d_ff:]

    t = lax.broadcasted_iota(jnp.int32, (tm, 1), 0) & (L - 1)
    if s_blk == 1:
        p0 = carry_ref[0, 0:1, :]
        p1 = carry_ref[0, 1:2, :]
    else:
        cr = carry_ref[...]
        p0 = jnp.broadcast_to(cr[:, 0:1, :], (s_blk, L, d_ff)).reshape(tm, d_ff)
        p1 = jnp.broadcast_to(cr[:, 1:2, :], (s_blk, L, d_ff)).reshape(tm, d_ff)
    prev1 = jnp.where(t == 0, p1, pltpu.roll(u, 1, 0))
    prev2 = jnp.where(t == 0, p0, jnp.where(t == 1, p1, pltpu.roll(u, 2, 0)))
    cw = cw_ref[...]
    conv = cb_ref[...] + (prev2 * cw[0:1, :] + prev1 * cw[1:2, :] + u * cw[2:3, :])
    hid = conv * _sigmoid(conv) * gate
    x2 = x1 + _dot(hid, wout_ref[...])
    if final_norm:
        xo_ref[...] = _rms_norm(x2, gfin_ref[...])
    else:
        xo_ref[...] = x2

    if s_blk == 1:
        carry_ref[0] = u[tm - (CONV_W - 1):tm, :]
    else:
        carry_ref[...] = u.reshape(s_blk, L, d_ff)[:, L - (CONV_W - 1):L, :]
    bufo_ref[...] = carry_ref[...]


def _ffn(x2d, mixed, wo_parts, gn, w_in, cw, cb, w_out, buf, gfin, *, n_seq, seq_len, tm):
    rows, d = x2d.shape
    d_ff = w_out.shape[0]
    if tm <= seq_len:
        s_blk, L = 1, tm
        tiles = seq_len // tm
        grid = (n_seq, tiles)
        rowmap = lambda s, j: (s * tiles + j, 0)
    else:
        s_blk, L = tm // seq_len, seq_len
        grid = (n_seq // s_blk, 1)
        rowmap = lambda s, j: (s, 0)
    bcast = buf.shape[0] == 1 and n_seq > 1
    bufmap = (lambda s, j: (0, 0, 0)) if bcast else (lambda s, j: (s, 0, 0))
    buf_blk = 1 if bcast else s_blk
    const = lambda s, j: (0, 0)
    n_mixed = len(mixed)
    in_specs = [pl.BlockSpec((tm, d), rowmap)]
    in_specs += [pl.BlockSpec((tm, m.shape[1]), rowmap) for m in mixed]
    in_specs += [pl.BlockSpec(w.shape, const) for w in wo_parts]
    in_specs += [pl.BlockSpec((1, d), const),
                 pl.BlockSpec(w_in.shape, const),
                 pl.BlockSpec(cw.shape, const),
                 pl.BlockSpec((1, d_ff), const),
                 pl.BlockSpec(w_out.shape, const),
                 pl.BlockSpec((buf_blk, CONV_W - 1, d_ff), bufmap)]
    args = [x2d, *mixed, *wo_parts, gn, w_in, cw, cb, w_out, buf]
    if gfin is not None:
        in_specs.append(pl.BlockSpec((1, d), const))
        args.append(gfin)
    out_specs = (pl.BlockSpec((tm, d), rowmap),
                 pl.BlockSpec((s_blk, CONV_W - 1, d_ff), lambda s, j: (s, 0, 0)))
    out_shape = (jax.ShapeDtypeStruct((rows, d), F32),
                 jax.ShapeDtypeStruct((n_seq, CONV_W - 1, d_ff), F32))
    return pl.pallas_call(
        functools.partial(_ffn_kernel, n_mixed=n_mixed, s_blk=s_blk, L=L, d_ff=d_ff,
                          final_norm=gfin is not None),
        grid=grid,
        in_specs=in_specs,
        out_specs=out_specs,
        out_shape=out_shape,
        scratch_shapes=[pltpu.VMEM((s_blk, CONV_W - 1, d_ff), F32)],
        compiler_params=_params(2),
        name="ffn",
    )(*args)


def _tile_rows(n_seq, seq_len, target):
    rows = n_seq * seq_len
    if seq_len >= target:
        return target
    return min(rows, target)


def _trunk(x, pos, st, w, *, chunk_ab, chunk_c, sb_ab, sb_c, unroll_ab, group_c):
    n_seq, seq_len, d = x.shape
    rows = n_seq * seq_len
    tm = _tile_rows(n_seq, seq_len, 256)
    x2d = x.reshape(rows, d)

    half = HEAD_DIM // 2
    inv = 1.0 / (ROPE_BASE ** jnp.linspace(0.0, 1.0, half, dtype=F32))
    ang = pos[:, None] * inv[None, :]
    cos_t = jnp.concatenate([jnp.cos(ang), jnp.cos(ang)], axis=-1)
    sin_t = jnp.concatenate([-jnp.sin(ang), jnp.sin(ang)], axis=-1)

    mxa, mxb, c_new, n_new, m_new, r_new = _ab_mixer(
        x, w["norm_mix"][0], w["w_in_ab"], cos_t, sin_t, w["gate_bias"], w["lg"], w["gn_a"], w["gn_b"],
        st["c"], st["n"], st["m"], st["r"], chunk=chunk_ab, sb=sb_ab, unroll=unroll_ab)
    hd = H_AB * HEAD_DIM
    x2d, buf0 = _ffn(x2d, [mxa.reshape(rows, hd), mxb.reshape(rows, hd)], w["w_out_ab"],
                     w["norm_ffn"][0], w["w_ffn_in"][0], w["conv_w"][0], w["conv_b"][0], w["w_ffn_out"][0],
                     st["conv"][0], None, n_seq=n_seq, seq_len=seq_len, tm=tm)

    mx, s_new = _hgrn_mixer(x2d.reshape(n_seq, seq_len, d), w["norm_mix"][1], w["w_in_c"], w["lb_logits"],
                            w["gn_c"], st["s"], chunk=chunk_c, sb=sb_c, layer=1, group_heads=group_c)
    y2d, buf1 = _ffn(x2d, [mx.reshape(rows, -1)], w["w_out_c"],
                     w["norm_ffn"][1], w["w_ffn_in"][1], w["conv_w"][1], w["conv_b"][1], w["w_ffn_out"][1],
                     st["conv"][1], w["norm_final"], n_seq=n_seq, seq_len=seq_len, tm=tm)
    new_st = {"c": c_new, "n": n_new, "m": m_new, "r": r_new, "s": s_new, "conv": (buf0, buf1)}
    return y2d.reshape(n_seq, seq_len, d), new_st


def _lane_rep(v):
    return jnp.broadcast_to(v[..., None, None], v.shape + (1, LANES))


def kernel(x_prompt, x_sample, state_mlstm_C, state_mlstm_n, state_mlstm_m, state_ret_S, state_hgrn_S,
           state_ffn_conv, meta_tokens, norm_mix, w_in_ab, b_igate, b_fgate, gn_mlstm, gn_ret, w_out_ab,
           lb_logits, w_in_c, gn_hgrn, w_out_c, norm_ffn, w_ffn_in, conv_w, conv_b, w_ffn_out, norm_final):
    bp, seq, d = x_prompt.shape
    bs, dec_seq, _ = x_sample.shape
    hd = H_AB * HEAD_DIM
    n_hc = state_hgrn_S.shape[2]
    d_ff = w_ffn_out.shape[1]
    assert w_in_ab.shape[0] == 1 and w_in_c.shape[0] == 1 and norm_mix.shape[0] == 2

    wab = w_in_ab[0].astype(BF16)
    g0 = 4 * hd
    blocks = jnp.concatenate([wab[:, :g0], wab[:, g0 + 2 * H_AB:]], axis=1)
    blocks = blocks.reshape(d, 8, H_AB, HEAD_DIM).transpose(0, 2, 1, 3).reshape(d, 8 * hd)
    w_ab = jnp.concatenate([blocks, wab[:, g0:g0 + 2 * H_AB], jnp.zeros((d, LANES - 2 * H_AB), BF16)], axis=1)
    wo_ab = w_out_ab[0].astype(BF16)
    log_gamma = jnp.log1p(-jnp.exp2(-5.0 - jnp.arange(H_AB, dtype=F32)))
    gate_bias = jnp.concatenate([b_igate[0], b_fgate[0], jnp.zeros((LANES - 2 * H_AB,), F32)])[None, :]
    w = {
        "norm_mix": norm_mix[:, None, :],
        "norm_ffn": norm_ffn[:, None, :],
        "norm_final": norm_final[None, :],
        "w_in_ab": w_ab,
        "gate_bias": gate_bias,
        "lg": _lane_rep(log_gamma),
        "gn_a": gn_mlstm[0][None, :],
        "gn_b": gn_ret[0][None, :],
        "w_out_ab": [wo_ab[:hd], wo_ab[hd:]],
        "lb_logits": lb_logits,
        "w_in_c": w_in_c[0].astype(BF16).reshape(d, 4, n_hc, HEAD_DIM).transpose(0, 2, 1, 3).reshape(d, -1),
        "gn_c": gn_hgrn[0][None, :],
        "w_out_c": [w_out_c[0].astype(BF16)],
        "w_ffn_in": w_ffn_in.astype(BF16),
        "conv_w": conv_w,
        "conv_b": conv_b[:, None, :],
        "w_ffn_out": w_ffn_out.astype(BF16),
    }

    def zero_state(n):
        return {"c": jnp.zeros((n, H_AB, HEAD_DIM, HEAD_DIM), F32),
                "n": jnp.zeros((n, H_AB, 1, LANES), F32),
                "m": jnp.zeros((n, H_AB, 1, LANES), F32),
                "r": jnp.zeros((n, H_AB, HEAD_DIM, HEAD_DIM), F32),
                "s": jnp.zeros((n, n_hc, HEAD_DIM, HEAD_DIM), F32),
                "conv": (jnp.zeros((n, CONV_W - 1, d_ff), F32), jnp.zeros((n, CONV_W - 1, d_ff), F32))}

    _, st_meta = _trunk(meta_tokens[None].astype(F32), jnp.arange(N_META, dtype=F32), zero_state(1), w,
                        chunk_ab=N_META, chunk_c=N_META, sb_ab=1, sb_c=1, unroll_ab=1, group_c=4)
    pos_p = N_META + jnp.arange(seq, dtype=F32)
    y_prompt, st_p = _trunk(x_prompt, pos_p, st_meta, w, chunk_ab=128, chunk_c=128,
                            sb_ab=min(bp, 2), sb_c=min(bp, 2), unroll_ab=min(bp, 2), group_c=4)
    st_s = {"c": state_mlstm_C[0], "n": state_mlstm_n[0][:, :, None, :], "m": _lane_rep(state_mlstm_m[0]),
            "r": state_ret_S[0], "s": state_hgrn_S[0], "conv": (state_ffn_conv[0], state_ffn_conv[1])}
    pos_s = PAST_LEN + jnp.arange(dec_seq, dtype=F32)
    y_sample, st_s = _trunk(x_sample, pos_s, st_s, w, chunk_ab=dec_seq, chunk_c=dec_seq,
                            sb_ab=min(bs, 16), sb_c=min(bs, 16), unroll_ab=min(bs, 2), group_c=n_hc)

    def outs(s):
        return (s["c"][None], s["n"][:, :, 0, :][None], s["m"][:, :, 0, 0][None], s["r"][None], s["s"][None],
                jnp.stack(s["conv"]))

    cp, n_p, mp, rp, sp, convp = outs(st_p)
    cs, n_s, ms, rs, ss, convs = outs(st_s)
    return (y_prompt, y_sample, cp, cs, n_p, n_s, mp, ms, rp, rs, sp, ss, convp, convs)
```

```python
import functools

import numpy as np

import jax
import jax.numpy as jnp
from jax import lax
from jax.experimental import pallas as pl
from jax.experimental.pallas import tpu as pltpu

EPS = 1e-6
N_META = 16
PAST_LEN = 16384
ROPE_BASE = 10000.0
HEAD_DIM = 128
H_AB = 4
CONV_W = 3
LANES = 128
SUBLANES = 8
VMEM_LIMIT = 56 * 1024 * 1024
F32 = jnp.float32
BF16 = jnp.bfloat16


def _bf(x):
    return x.astype(BF16)


def _dot(a, b):
    return jnp.dot(_bf(a), _bf(b), preferred_element_type=F32)


def _dot_nt(a, b):
    return lax.dot_general(_bf(a), _bf(b), (((1,), (1,)), ((), ())), preferred_element_type=F32)


def _dot_tn(a, b):
    return lax.dot_general(_bf(a), _bf(b), (((0,), (0,)), ((), ())), preferred_element_type=F32)


def _rms_norm(x, gain):
    y = x * lax.rsqrt(jnp.mean(x * x, axis=-1, keepdims=True) + EPS)
    return y * gain


def _sigmoid(x):
    return 1.0 / (1.0 + jnp.exp(-x))


def _layer_norm(x, gain, mean_mat, tick):
    if mean_mat is None:
        mu = jnp.mean(x, axis=-1, keepdims=True)
        cen = x - mu
        var = jnp.mean(cen * cen, axis=-1, keepdims=True)
    else:
        mu = jnp.dot(_bf(x), mean_mat, preferred_element_type=F32)
        tick()
        yield
        cen = x - mu
        var = jnp.dot(_bf(cen * cen), mean_mat, preferred_element_type=F32)
        tick()
        yield
    return cen * lax.rsqrt(var + EPS) * gain


def _cumsum_rows(x):
    n = x.shape[0]
    row = lax.broadcasted_iota(jnp.int32, x.shape, 0)
    k = 1
    while k < n:
        x = x + jnp.where(row >= k, pltpu.roll(x, k, 0), 0.0)
        k *= 2
    return x


def _split_dot(mat_bf16, x):
    hi = _bf(x)
    lo = _bf(x - hi.astype(F32))
    n = x.shape[1]
    r = jnp.dot(mat_bf16, jnp.concatenate([hi, lo], axis=1), preferred_element_type=F32)
    return r[:, :n] + r[:, n:]


def _lock_step(gens):
    results = [None] * len(gens)
    live = list(enumerate(gens))
    while live:
        still = []
        for idx, g in live:
            try:
                next(g)
                still.append((idx, g))
            except StopIteration as stop:
                results[idx] = stop.value
        live = still
    return results


def _params(n_grid):
    return pltpu.CompilerParams(dimension_semantics=("arbitrary",) * n_grid,
                                vmem_limit_bytes=VMEM_LIMIT)


def _ab_kernel(*refs, sb, L, unroll, n_chunks):
    assert sb % unroll == 0
    refs = list(refs)
    xc_ref = refs.pop(0)
    xnext_ref = refs.pop(0) if n_chunks > 1 else None
    (gnm_ref, w_ref, cos_ref, sin_ref, bias_ref, lg_ref, gna_ref, gnb_ref,
     c0_ref, n0_ref, m0_ref, r0_ref) = [refs.pop(0) for _ in range(12)]
    if L > SUBLANES:
        tri_ref = refs.pop(0)
    mxa_ref, mxb_ref, c_ref, n_ref, m_ref, r_ref, za_ref = [refs.pop(0) for _ in range(7)]
    if n_chunks > 1:
        zb_ref, xnb_ref = refs
    chunk = pl.program_id(1)
    rows_all = sb * L
    d_model = xc_ref.shape[-1]

    @pl.when(chunk == 0)
    def _():
        c_ref[...] = jnp.broadcast_to(c0_ref[...], c_ref.shape)
        n_ref[...] = jnp.broadcast_to(n0_ref[...], n_ref.shape)
        m_ref[...] = jnp.broadcast_to(m0_ref[...], m_ref.shape)
        r_ref[...] = jnp.broadcast_to(r0_ref[...], r_ref.shape)
        xn0 = _rms_norm(xc_ref[...].reshape(rows_all, d_model), gnm_ref[...])
        za_ref[...] = _dot(xn0, w_ref[...])

    if n_chunks > 1:
        xnb_ref[...] = _bf(_rms_norm(xnext_ref[...].reshape(rows_all, d_model), gnm_ref[...]))
    cur = {}

    def seq_rows(s):
        return pl.ds(pl.multiple_of(s * L, L), L)

    def project_next(s, c0, c1):
        if cur["zn"] is not None:
            rows = seq_rows(s)
            cur["zn"][rows, c0:c1] = jnp.dot(xnb_ref[rows, :], w_ref[:, c0:c1], preferred_element_type=F32)

    def ticker(s, slab, first_round):
        half_w = 2 * HEAD_DIM
        pieces = {first_round: 0, first_round + 1: 1} if cur["zn"] is not None else {}
        rounds = [0]

        def tick():
            rounds[0] += 1
            half = pieces.pop(rounds[0], None)
            if half is not None:
                c0 = (2 * slab + half) * half_w
                project_next(s, c0, c0 + half_w)

        tick.pending = pieces
        return tick

    scale = HEAD_DIM ** -0.5
    row = lax.broadcasted_iota(jnp.int32, (L, L), 0)
    col = lax.broadcasted_iota(jnp.int32, (L, L), 1)
    causal = row >= col
    eye = row == col
    rel = jnp.where(causal, row - col, 0).astype(F32)
    tcol = lax.broadcasted_iota(jnp.int32, (L, 1), 0).astype(F32)
    cosv = cos_ref[...]
    sinv = sin_ref[...]
    bias = bias_ref[...]
    gate_col = 8 * H_AB * HEAD_DIM
    ones_bf = jnp.ones((L, LANES), BF16)
    mean_mat = jnp.full((HEAD_DIM, LANES), 1.0 / HEAD_DIM, BF16) if L > SUBLANES else None
    ret_tabs = []
    for h in range(H_AB):
        lg = lg_ref[h][:, :1]
        ret_tabs.append((jnp.where(causal, jnp.exp(rel * lg), 0.0),
                         jnp.broadcast_to(jnp.exp((tcol + 1.0) * lg), (L, LANES)),
                         jnp.broadcast_to(jnp.exp((L - 1.0 - tcol) * lg), (L, LANES)),
                         jnp.exp(L * lg)))

    def gates(s):
        project_next(s, gate_col, gate_col + LANES)
        pre = cur["zc"][seq_rows(s), gate_col:gate_col + LANES] + bias
        lf = jnp.minimum(pre, 0.0) - jnp.log1p(jnp.exp(-jnp.abs(pre)))
        lf = pltpu.roll(lf, LANES - H_AB, 1)
        if L > SUBLANES:
            b = _split_dot(tri_ref[...], lf)
            yield
        else:
            b = _cumsum_rows(lf)
        a = pre - b
        a_rows = jnp.transpose(a) if L == LANES else None
        return a, b, a_rows

    def zcol(s, j, h):
        c0 = (j * H_AB + h) * HEAD_DIM
        return cur["zc"][seq_rows(s), c0:c0 + HEAD_DIM]

    def mlstm_head(s, h, a, b, a_rows):
        tick = ticker(s, 2 * h, 1 + 2 * (h % 2))
        hs = slice(h * HEAD_DIM, (h + 1) * HEAD_DIM)
        a_col = a[:, h:h + 1]
        b_col = b[:, h:h + 1]
        if a_rows is not None:
            a_row = a_rows[h:h + 1, :]
        else:
            a_row = jnp.sum(jnp.where(eye, a_col, 0.0), axis=0, keepdims=True)
        m0 = m_ref[s, h][:, :1]
        a_caus = jnp.where(causal, a_row, -jnp.inf)
        a_max = jnp.max(a_caus, axis=-1, keepdims=True)
        if L > SUBLANES:
            tick()
            yield
        mx = jnp.maximum(m0, a_max)
        m_t = b_col + mx
        d = jnp.exp(a_caus - mx)
        inter = jnp.broadcast_to(jnp.exp(m0 - mx), (L, LANES))
        q_f32 = zcol(s, 0, h)
        q = _bf(q_f32)
        k = zcol(s, 1, h) * scale
        v = _bf(zcol(s, 2, h))
        c_prev = c_ref[s, h]
        n_prev = n_ref[s, h]
        qk = _dot_nt(q, k)
        qc = _dot(q, c_prev)
        tick()
        yield
        sm_f32 = qk * d
        sm = _bf(sm_f32)
        m_new = m_t[L - 1:L, :]
        b_last = b_col[L - 1:L, :]
        w_col = jnp.exp(a_col + b_last - m_new)
        keep = jnp.exp(b_last + m0 - m_new)
        kw = k * w_col
        if L > SUBLANES:
            nd = jnp.dot(sm, jnp.concatenate([v, ones_bf], axis=1), preferred_element_type=F32)
            qn = _dot_nt(q, jnp.broadcast_to(n_prev, (HEAD_DIM, LANES)))
            kv = _dot_tn(kw, v)
            tick()
            yield
            num = nd[:, :LANES] + inter * qc
            den = nd[:, LANES:] + inter * qn
        else:
            sv = _dot(sm, v)
            kv = _dot_tn(kw, v)
            tick()
            yield
            num = sv + inter * qc
            den = (jnp.sum(sm_f32, axis=-1, keepdims=True)
                   + inter * jnp.sum(q_f32 * n_prev, axis=-1, keepdims=True))
        hv = num / jnp.maximum(jnp.abs(den), jnp.exp(-m_t))
        y = yield from _layer_norm(hv, gna_ref[:, hs], mean_mat, tick)
        assert not tick.pending, "not enough lock-step rounds for the projection pieces"
        mxa_ref[s, :, hs] = (_sigmoid(zcol(s, 3, h)) * y).astype(mxa_ref.dtype)
        return [(c_ref, h, keep * c_prev + kv),
                (n_ref, h, keep * n_prev + jnp.sum(kw, axis=0, keepdims=True)),
                (m_ref, h, jnp.broadcast_to(m_new, (1, LANES)))]

    def ret_head(s, h):
        tick = ticker(s, 2 * h + 1, 3 - 2 * (h % 2))
        hs = slice(h * HEAD_DIM, (h + 1) * HEAD_DIM)
        decay, inner, tail, g_pow = ret_tabs[h]
        q2 = zcol(s, 4, h)
        k2 = zcol(s, 5, h)
        v2 = _bf(zcol(s, 6, h))
        qr = _bf(q2 * cosv + pltpu.roll(q2, HEAD_DIM // 2, 1) * sinv)
        kr = (k2 * cosv + pltpu.roll(k2, HEAD_DIM // 2, 1) * sinv) * scale
        r_prev = r_ref[s, h]
        qk = _dot_nt(qr, kr)
        qs = _dot(qr, r_prev)
        kv = _dot_tn(kr * tail, v2)
        tick()
        yield
        o2 = _dot(qk * decay, v2)
        tick()
        yield
        o2 = o2 + qs * inner
        y2 = yield from _layer_norm(o2, gnb_ref[:, hs], mean_mat, tick)
        assert not tick.pending, "not enough lock-step rounds for the projection pieces"
        gv = zcol(s, 7, h)
        mxb_ref[s, :, hs] = (gv * _sigmoid(gv) * y2).astype(mxb_ref.dtype)
        return [(r_ref, h, g_pow * r_prev + kv)]

    def group(i, carry_):
        seqs = [i * unroll + j for j in range(unroll)]
        prep = _lock_step([gates(s) for s in seqs])
        chains = []
        for s, (a, b, a_rows) in zip(seqs, prep):
            for h in range(H_AB):
                chains.append((s, mlstm_head(s, h, a, b, a_rows)))
                chains.append((s, ret_head(s, h)))
        new = _lock_step([g for _, g in chains])
        for (s, _), new_state in zip(chains, new):
            for ref, h, val in new_state:
                ref[s, h] = val
        return carry_

    def step(zc_ref, zn_ref):
        cur["zc"], cur["zn"] = zc_ref, zn_ref
        lax.fori_loop(0, sb // unroll, lambda i, carry_: group(i, carry_), 0)

    if n_chunks > 1:
        @pl.when(chunk % 2 == 0)
        def _():
            step(za_ref, zb_ref)

        @pl.when(chunk % 2 == 1)
        def _():
            step(zb_ref, za_ref)
    else:
        step(za_ref, None)


def _ab_mixer(x, gnm, w, cos_t, sin_t, bias, lg, gna, gnb, c0, n0, m0, r0, *, chunk, sb, unroll):
    n_seq, seq_len, d = x.shape
    n_chunks = seq_len // chunk
    width = w.shape[1]
    hd = H_AB * HEAD_DIM
    bcast = c0.shape[0] == 1 and n_seq > 1
    sb0 = 1 if bcast else sb
    st = (lambda i: 0) if bcast else (lambda i: i)
    full2 = lambda a: pl.BlockSpec(a.shape, lambda i, c: (0, 0))
    mat_in = pl.BlockSpec((sb0, H_AB, HEAD_DIM, HEAD_DIM), lambda i, c: (st(i), 0, 0, 0))
    vec_in = pl.BlockSpec((sb0, H_AB, 1, LANES), lambda i, c: (st(i), 0, 0, 0))
    mat_out = pl.BlockSpec((sb, H_AB, HEAD_DIM, HEAD_DIM), lambda i, c: (i, 0, 0, 0))
    vec_out = pl.BlockSpec((sb, H_AB, 1, LANES), lambda i, c: (i, 0, 0, 0))
    mix_out = pl.BlockSpec((sb, chunk, hd), lambda i, c: (i, c, 0))
    tab = pl.BlockSpec((chunk, LANES), lambda i, c: (c, 0))
    in_specs = [pl.BlockSpec((sb, chunk, d), lambda i, c: (i, c, 0))]
    args = [x]
    if n_chunks > 1:
        in_specs.append(pl.BlockSpec((sb, chunk, d), lambda i, c: (i, jnp.minimum(c + 1, n_chunks - 1), 0)))
        args.append(x)
    w_spec = pl.BlockSpec(w.shape, lambda i, c: (0, 0), pipeline_mode=pl.Buffered(1))
    in_specs += [full2(gnm), w_spec, tab, tab, full2(bias),
                 pl.BlockSpec(lg.shape, lambda i, c: (0, 0, 0)), full2(gna), full2(gnb),
                 mat_in, vec_in, vec_in, mat_in]
    args += [gnm, w, cos_t, sin_t, bias, lg, gna, gnb, c0, n0, m0, r0]
    if chunk > SUBLANES:
        tri = jnp.asarray(np.tril(np.ones((chunk, chunk), np.float32)), BF16)
        in_specs.append(full2(tri))
        args.append(tri)
    scratch = [pltpu.VMEM((sb * chunk, width), F32)]
    if n_chunks > 1:
        scratch += [pltpu.VMEM((sb * chunk, width), F32), pltpu.VMEM((sb * chunk, d), BF16)]
    out_shape = (jax.ShapeDtypeStruct((n_seq, seq_len, hd), BF16),
                 jax.ShapeDtypeStruct((n_seq, seq_len, hd), BF16),
                 jax.ShapeDtypeStruct((n_seq, H_AB, HEAD_DIM, HEAD_DIM), F32),
                 jax.ShapeDtypeStruct((n_seq, H_AB, 1, LANES), F32),
                 jax.ShapeDtypeStruct((n_seq, H_AB, 1, LANES), F32),
                 jax.ShapeDtypeStruct((n_seq, H_AB, HEAD_DIM, HEAD_DIM), F32))
    return pl.pallas_call(
        functools.partial(_ab_kernel, sb=sb, L=chunk, unroll=unroll, n_chunks=n_chunks),
        grid=(n_seq // sb, n_chunks),
        in_specs=in_specs,
        out_specs=(mix_out, mix_out, mat_out, vec_out, vec_out, mat_out),
        out_shape=out_shape,
        scratch_shapes=scratch,
        compiler_params=_params(2),
        name="ab_mixer",
    )(*args)


def _hgrn_level_tables(L):
    t = np.arange(L)[:, None]
    u = np.arange(L)[None, :]
    lvl = np.full((L, L), -1, np.int32)
    size, j = 1, 0
    while size < L:
        upper = ((t // size) % 2) == 1
        lvl[((t // size) == (u // size) + 1) & upper] = j
        size *= 2
        j += 1
    return (u <= t).astype(np.float32), lvl


def _hgrn_small_level_factors(f, r):
    one = jnp.ones_like(f)
    prev = lambda x, d: pltpu.roll(x, d, 1)
    nxt = lambda x, d: pltpu.roll(x, SUBLANES - d, 1)
    f_n1 = nxt(f, 1)
    p1 = f * prev(f, 1)
    sel = lambda idx, *vals: functools.reduce(
        lambda acc, iv: jnp.where(idx == iv[0], iv[1], acc), list(enumerate(vals))[:-1], vals[-1])
    g0 = jnp.where((r & 1) == 1, f, one)
    g1 = sel(r & 3, f_n1, one, f, p1)
    g2 = sel(r, f_n1 * nxt(p1, 3), nxt(p1, 2), f_n1, one, f, p1, p1 * prev(f, 2), p1 * prev(p1, 2))
    return [g0, g1, g2]


def _hgrn_kernel(*refs, sb, L, layer, n_heads, n_chunks, group_heads):
    refs = list(refs)
    xc_ref = refs.pop(0)
    xnext_ref = refs.pop(0) if n_chunks > 1 else None
    gnm_ref, w_ref, lbl_ref, gn_ref, s0_ref = [refs.pop(0) for _ in range(5)]
    if L > SUBLANES:
        tri_ref, lvl_ref = refs.pop(0), refs.pop(0)
    mx_ref, s_ref, za_ref = refs.pop(0), refs.pop(0), refs.pop(0)
    if n_chunks > 1:
        zb_ref, xnb_ref = refs
    chunk = pl.program_id(1)
    rows_all = sb * L
    d_model = xc_ref.shape[-1]

    @pl.when(chunk == 0)
    def _():
        s_ref[...] = jnp.broadcast_to(s0_ref[...], s_ref.shape)
        xn0 = _rms_norm(xc_ref[...].reshape(rows_all, d_model), gnm_ref[...])
        za_ref[...] = _dot(xn0, w_ref[...])

    if n_chunks > 1:
        xnb_ref[...] = _bf(_rms_norm(xnext_ref[...].reshape(rows_all, d_model), gnm_ref[...]))

    logits = lbl_ref[...]
    ex = jnp.exp(logits - jnp.max(logits, axis=0, keepdims=True))
    p = ex / jnp.sum(ex, axis=0, keepdims=True)
    cum = p[0:1, :]
    for r in range(1, layer + 1):
        cum = cum + p[r:r + 1, :]
    lb_all = cum - p[0:1, :]

    row = lax.broadcasted_iota(jnp.int32, (L, LANES), 0)
    n_levels = L.bit_length() - 1
    unit_w = 4 * HEAD_DIM
    width = n_heads * HEAD_DIM

    def unit(s, h, zc_ref, zn_ref, slot):
        rows = pl.ds(pl.multiple_of(s * L, L), L)
        base = h * unit_w
        hs = slice(h * HEAD_DIM, (h + 1) * HEAD_DIM)
        zblock = lambda j: zc_ref[rows, j * width + h * HEAD_DIM:j * width + (h + 1) * HEAD_DIM]
        pieces = {2 * slot + 1: 0, 2 * slot + 2: 1} if zn_ref is not None else {}
        rounds = [0]

        def tick():
            rounds[0] += 1
            half = pieces.pop(rounds[0], None)
            if half is not None:
                c0 = base + half * (unit_w // 2)
                zn_ref[rows, c0:c0 + unit_w // 2] = jnp.dot(xnb_ref[rows, :], w_ref[:, c0:c0 + unit_w // 2],
                                                            preferred_element_type=F32)

        lb = lb_all[:, hs]
        qv = zblock(0)
        fv = zblock(1)
        iv = zblock(2)
        e = jnp.exp(-jnp.abs(fv))
        rcp = 1.0 / (1.0 + e)
        pos = fv >= 0.0
        sig = jnp.where(pos, rcp, e * rcp)
        sig_neg = jnp.where(pos, e * rcp, rcp)
        f_gate = lb + (1.0 - lb) * sig
        log_f = jnp.log(f_gate)
        k = (1.0 - lb) * sig_neg
        s_prev = s_ref[s, h]
        diag = jnp.sum(qv * k, axis=-1, keepdims=True)

        if L > SUBLANES:
            b = _split_dot(tri_ref[...], log_f)
            tick()
            yield
            nt = L // SUBLANES
            tile3 = lambda x: x.reshape(nt, SUBLANES, x.shape[-1])
            q3, k3, b3 = tile3(qv), tile3(k), tile3(b)
            r3 = lax.broadcasted_iota(jnp.int32, (1, SUBLANES, LANES), 1)
            small = _hgrn_small_level_factors(tile3(f_gate), r3)
            acc = [jnp.zeros((SUBLANES, L), F32)] * nt
            for j in range(n_levels):
                size = 1 << j
                if size < SUBLANES:
                    xj = (jnp.where((r3 & size) != 0, q3, k3) * small[j]).reshape(L, LANES)
                    up = list(range(nt))
                    a_j = _dot_nt(xj, xj)
                else:
                    s8 = size // SUBLANES
                    up = [i for i in range(nt) if i & s8]
                    mids = {i: jnp.broadcast_to(b3[i, SUBLANES - 1:, :], (SUBLANES, LANES))
                            for i in range(s8 - 1, nt, 2 * s8)}
                    b_mid = jnp.stack([mids[(i // (2 * s8)) * 2 * s8 + s8 - 1] for i in range(nt)])
                    fac = jnp.exp(-jnp.abs(b3 - b_mid))
                    x3 = jnp.stack([q3[i] if i & s8 else k3[i] for i in range(nt)]) * fac
                    xj = x3.reshape(L, LANES)
                    a_j = _dot_nt(jnp.concatenate([x3[i] for i in up], axis=0), xj)
                tick()
                yield
                for n, i in enumerate(up):
                    keep = lvl_ref[i * SUBLANES:(i + 1) * SUBLANES, :] == j
                    acc[i] = jnp.where(keep, a_j[n * SUBLANES:(n + 1) * SUBLANES, :], acc[i])
            o = diag * iv + _dot(jnp.concatenate(acc, axis=0), iv)
            tick()
            yield
        else:
            b = _cumsum_rows(log_f)
            pair = []
            for j in range(1, L):
                dec = jnp.exp(jnp.minimum(b - pltpu.roll(b, j, 0), 0.0))
                pair.append(jnp.sum(jnp.where(row >= j, qv * pltpu.roll(k, j, 0) * dec, 0.0),
                                    axis=-1, keepdims=True))
            tick()
            yield
            o = diag * iv
            for j, a in enumerate(pair, start=1):
                o = o + a * pltpu.roll(iv, j, 0)

        o = o + _dot(qv * jnp.exp(b), s_prev)
        tick()
        yield
        b_last = b[L - 1:L, :]
        e_col = jnp.transpose(jnp.broadcast_to(jnp.exp(b_last), (HEAD_DIM, LANES)))
        s_new = e_col * s_prev + _dot_tn(k * jnp.exp(b_last - b), iv)
        ms = jnp.mean(o * o, axis=-1, keepdims=True)
        tick()
        yield
        assert not pieces, "not enough lock-step rounds for the projection pieces"
        y = o * lax.rsqrt(ms + EPS) * gn_ref[:, hs]
        gv = zblock(3)
        mx_ref[s, :, hs] = (y * (gv * _sigmoid(gv))).astype(mx_ref.dtype)
        return s_new

    def step(zc_ref, zn_ref):
        def per_seq(s, carry_):
            for h0 in range(0, n_heads, group_heads):
                heads = list(range(h0, h0 + group_heads))
                new = _lock_step([unit(s, h, zc_ref, zn_ref, slot) for slot, h in enumerate(heads)])
                for h, s_new in zip(heads, new):
                    s_ref[s, h] = s_new
            return carry_
        lax.fori_loop(0, sb, per_seq, 0)

    if n_chunks > 1:
        @pl.when(chunk % 2 == 0)
        def _():
            step(za_ref, zb_ref)

        @pl.when(chunk % 2 == 1)
        def _():
            step(zb_ref, za_ref)
    else:
        step(za_ref, None)


def _hgrn_mixer(x, gnm, w, lb_logits, gain, s0, *, chunk, sb, layer, group_heads):
    n_seq, seq_len, d = x.shape
    n_heads = s0.shape[1]
    n_chunks = seq_len // chunk
    width = n_heads * HEAD_DIM
    bcast = s0.shape[0] == 1 and n_seq > 1
    sb0 = 1 if bcast else sb
    st = (lambda i: 0) if bcast else (lambda i: i)
    full2 = lambda a: pl.BlockSpec(a.shape, lambda i, c: (0, 0))
    in_specs = [pl.BlockSpec((sb, chunk, d), lambda i, c: (i, c, 0))]
    args = [x]
    if n_chunks > 1:
        in_specs.append(pl.BlockSpec((sb, chunk, d), lambda i, c: (i, jnp.minimum(c + 1, n_chunks - 1), 0)))
        args.append(x)
    w_spec = pl.BlockSpec(w.shape, lambda i, c: (0, 0), pipeline_mode=pl.Buffered(1))
    in_specs += [full2(gnm), w_spec, full2(lb_logits), full2(gain),
                 pl.BlockSpec((sb0, n_heads, HEAD_DIM, HEAD_DIM), lambda i, c: (st(i), 0, 0, 0))]
    args += [gnm, w, lb_logits, gain, s0]
    if chunk > SUBLANES:
        tri, lvl = _hgrn_level_tables(chunk)
        in_specs += [pl.BlockSpec(tri.shape, lambda i, c: (0, 0)), pl.BlockSpec(lvl.shape, lambda i, c: (0, 0))]
        args += [jnp.asarray(tri, BF16), jnp.asarray(lvl)]
    out_specs = (pl.BlockSpec((sb, chunk, width), lambda i, c: (i, c, 0)),
                 pl.BlockSpec((sb, n_heads, HEAD_DIM, HEAD_DIM), lambda i, c: (i, 0, 0, 0)))
    out_shape = (jax.ShapeDtypeStruct((n_seq, seq_len, width), BF16),
                 jax.ShapeDtypeStruct((n_seq, n_heads, HEAD_DIM, HEAD_DIM), F32))
    scratch = [pltpu.VMEM((sb * chunk, w.shape[1]), F32)]
    if n_chunks > 1:
        scratch += [pltpu.VMEM((sb * chunk, w.shape[1]), F32), pltpu.VMEM((sb * chunk, d), BF16)]
    return pl.pallas_call(
        functools.partial(_hgrn_kernel, sb=sb, L=chunk, layer=layer, n_heads=n_heads, n_chunks=n_chunks,
                          group_heads=group_heads),
        grid=(n_seq // sb, n_chunks),
        in_specs=in_specs,
        out_specs=out_specs,
        out_shape=out_shape,
        scratch_shapes=scratch,
        compiler_params=_params(2),
        name="hgrn_mixer",
    )(*args)


def _ffn_kernel(*refs, n_mixed, s_blk, L, d_ff, final_norm):
    x_ref = refs[0]
    mix_refs = refs[1:1 + n_mixed]
    wo_refs = refs[1 + n_mixed:1 + 2 * n_mixed]
    (gn_ref, win_ref, cw_ref, cb_ref, wout_ref, buf_ref) = refs[1 + 2 * n_mixed:7 + 2 * n_mixed]
    pos = 7 + 2 * n_mixed
    if final_norm:
        gfin_ref = refs[pos]
        pos += 1
    xo_ref, bufo_ref, carry_ref = refs[pos:pos + 3]
    tile = pl.program_id(1)
    tm = s_blk * L

    @pl.when(tile == 0)
    def _():
        carry_ref[...] = jnp.broadcast_to(buf_ref[...], carry_ref.shape)

    x1 = x_ref[...]
    for m_ref, w_ref in zip(mix_refs, wo_refs):
        x1 = x1 + jnp.dot(m_ref[...], w_ref[...], preferred_element_type=F32)
    xn = _rms_norm(x1, gn_ref[...])
    ug = _dot(xn, win_ref[...])
    u = ug[:, :d_ff]
    gate = ug[:, d_ff:]

    t = lax.broadcasted_iota(jnp.int32, (tm, 1), 0) & (L - 1)
    if s_blk == 1:
        p0 = carry_ref[0, 0:1, :]
        p1 = carry_ref[0, 1:2, :]
    else:
        cr = carry_ref[...]
        p0 = jnp.broadcast_to(cr[:, 0:1, :], (s_blk, L, d_ff)).reshape(tm, d_ff)
        p1 = jnp.broadcast_to(cr[:, 1:2, :], (s_blk, L, d_ff)).reshape(tm, d_ff)
    prev1 = jnp.where(t == 0, p1, pltpu.roll(u, 1, 0))
    prev2 = jnp.where(t == 0, p0, jnp.where(t == 1, p1, pltpu.roll(u, 2, 0)))
    cw = cw_ref[...]
    conv = cb_ref[...] + (prev2 * cw[0:1, :] + prev1 * cw[1:2, :] + u * cw[2:3, :])
    hid = conv * _sigmoid(conv) * gate
    x2 = x1 + _dot(hid, wout_ref[...])
    if final_norm:
        xo_ref[...] = _rms_norm(x2, gfin_ref[...])
    else:
        xo_ref[...] = x2

    if s_blk == 1:
        carry_ref[0] = u[tm - (CONV_W - 1):tm, :]
    else:
        carry_ref[...] = u.reshape(s_blk, L, d_ff)[:, L - (CONV_W - 1):L, :]
    bufo_ref[...] = carry_ref[...]


def _ffn(x2d, mixed, wo_parts, gn, w_in, cw, cb, w_out, buf, gfin, *, n_seq, seq_len, tm):
    rows, d = x2d.shape
    d_ff = w_out.shape[0]
    if tm <= seq_len:
        s_blk, L = 1, tm
        tiles = seq_len // tm
        grid = (n_seq, tiles)
        rowmap = lambda s, j: (s * tiles + j, 0)
    else:
        s_blk, L = tm // seq_len, seq_len
        grid = (n_seq // s_blk, 1)
        rowmap = lambda s, j: (s, 0)
    bcast = buf.shape[0] == 1 and n_seq > 1
    bufmap = (lambda s, j: (0, 0, 0)) if bcast else (lambda s, j: (s, 0, 0))
    buf_blk = 1 if bcast else s_blk
    const = lambda s, j: (0, 0)
    n_mixed = len(mixed)
    in_specs = [pl.BlockSpec((tm, d), rowmap)]
    in_specs += [pl.BlockSpec((tm, m.shape[1]), rowmap) for m in mixed]
    in_specs += [pl.BlockSpec(w.shape, const) for w in wo_parts]
    in_specs += [pl.BlockSpec((1, d), const),
                 pl.BlockSpec(w_in.shape, const),
                 pl.BlockSpec(cw.shape, const),
                 pl.BlockSpec((1, d_ff), const),
                 pl.BlockSpec(w_out.shape, const),
                 pl.BlockSpec((buf_blk, CONV_W - 1, d_ff), bufmap)]
    args = [x2d, *mixed, *wo_parts, gn, w_in, cw, cb, w_out, buf]
    if gfin is not None:
        in_specs.append(pl.BlockSpec((1, d), const))
        args.append(gfin)
    out_specs = (pl.BlockSpec((tm, d), rowmap),
                 pl.BlockSpec((s_blk, CONV_W - 1, d_ff), lambda s, j: (s, 0, 0)))
    out_shape = (jax.ShapeDtypeStruct((rows, d), F32),
                 jax.ShapeDtypeStruct((n_seq, CONV_W - 1, d_ff), F32))
    return pl.pallas_call(
        functools.partial(_ffn_kernel, n_mixed=n_mixed, s_blk=s_blk, L=L, d_ff=d_ff,
                          final_norm=gfin is not None),
        grid=grid,
        in_specs=in_specs,
        out_specs=out_specs,
        out_shape=out_shape,
        scratch_shapes=[pltpu.VMEM((s_blk, CONV_W - 1, d_ff), F32)],
        compiler_params=_params(2),
        name="ffn",
    )(*args)


def _tile_rows(n_seq, seq_len, target):
    rows = n_seq * seq_len
    if seq_len >= target:
        return target
    return min(rows, target)


def _trunk(x, pos, st, w, *, chunk_ab, chunk_c, sb_ab, sb_c, unroll_ab, group_c):
    n_seq, seq_len, d = x.shape
    rows = n_seq * seq_len
    tm = _tile_rows(n_seq, seq_len, 256)
    x2d = x.reshape(rows, d)

    half = HEAD_DIM // 2
    inv = 1.0 / (ROPE_BASE ** jnp.linspace(0.0, 1.0, half, dtype=F32))
    ang = pos[:, None] * inv[None, :]
    cos_t = jnp.concatenate([jnp.cos(ang), jnp.cos(ang)], axis=-1)
    sin_t = jnp.concatenate([-jnp.sin(ang), jnp.sin(ang)], axis=-1)

    mxa, mxb, c_new, n_new, m_new, r_new = _ab_mixer(
        x, w["norm_mix"][0], w["w_in_ab"], cos_t, sin_t, w["gate_bias"], w["lg"], w["gn_a"], w["gn_b"],
        st["c"], st["n"], st["m"], st["r"], chunk=chunk_ab, sb=sb_ab, unroll=unroll_ab)
    hd = H_AB * HEAD_DIM
    x2d, buf0 = _ffn(x2d, [mxa.reshape(rows, hd), mxb.reshape(rows, hd)], w["w_out_ab"],
                     w["norm_ffn"][0], w["w_ffn_in"][0], w["conv_w"][0], w["conv_b"][0], w["w_ffn_out"][0],
                     st["conv"][0], None, n_seq=n_seq, seq_len=seq_len, tm=tm)

    mx, s_new = _hgrn_mixer(x2d.reshape(n_seq, seq_len, d), w["norm_mix"][1], w["w_in_c"], w["lb_logits"],
                            w["gn_c"], st["s"], chunk=chunk_c, sb=sb_c, layer=1, group_heads=group_c)
    y2d, buf1 = _ffn(x2d, [mx.reshape(rows, -1)], w["w_out_c"],
                     w["norm_ffn"][1], w["w_ffn_in"][1], w["conv_w"][1], w["conv_b"][1], w["w_ffn_out"][1],
                     st["conv"][1], w["norm_final"], n_seq=n_seq, seq_len=seq_len, tm=tm)
    new_st = {"c": c_new, "n": n_new, "m": m_new, "r": r_new, "s": s_new, "conv": (buf0, buf1)}
    return y2d.reshape(n_seq, seq_len, d), new_st


def _lane_rep(v):
    return jnp.broadcast_to(v[..., None, None], v.shape + (1, LANES))


def kernel(x_prompt, x_sample, state_mlstm_C, state_mlstm_n, state_mlstm_m, state_ret_S, state_hgrn_S,
           state_ffn_conv, meta_tokens, norm_mix, w_in_ab, b_igate, b_fgate, gn_mlstm, gn_ret, w_out_ab,
           lb_logits, w_in_c, gn_hgrn, w_out_c, norm_ffn, w_ffn_in, conv_w, conv_b, w_ffn_out, norm_final):
    bp, seq, d = x_prompt.shape
    bs, dec_seq, _ = x_sample.shape
    hd = H_AB * HEAD_DIM
    n_hc = state_hgrn_S.shape[2]
    d_ff = w_ffn_out.shape[1]
    assert w_in_ab.shape[0] == 1 and w_in_c.shape[0] == 1 and norm_mix.shape[0] == 2

    wab = w_in_ab[0].astype(BF16)
    g0 = 4 * hd
    w_ab = jnp.concatenate([wab[:, :g0], wab[:, g0 + 2 * H_AB:], wab[:, g0:g0 + 2 * H_AB],
                            jnp.zeros((d, LANES - 2 * H_AB), BF16)], axis=1)
    wo_ab = w_out_ab[0].astype(BF16)
    log_gamma = jnp.log1p(-jnp.exp2(-5.0 - jnp.arange(H_AB, dtype=F32)))
    gate_bias = jnp.concatenate([b_igate[0], b_fgate[0], jnp.zeros((LANES - 2 * H_AB,), F32)])[None, :]
    w = {
        "norm_mix": norm_mix[:, None, :],
        "norm_ffn": norm_ffn[:, None, :],
        "norm_final": norm_final[None, :],
        "w_in_ab": w_ab,
        "gate_bias": gate_bias,
        "lg": _lane_rep(log_gamma),
        "gn_a": gn_mlstm[0][None, :],
        "gn_b": gn_ret[0][None, :],
        "w_out_ab": [wo_ab[:hd], wo_ab[hd:]],
        "lb_logits": lb_logits,
        "w_in_c": w_in_c[0].astype(BF16),
        "gn_c": gn_hgrn[0][None, :],
        "w_out_c": [w_out_c[0].astype(BF16)],
        "w_ffn_in": w_ffn_in.astype(BF16),
        "conv_w": conv_w,
        "conv_b": conv_b[:, None, :],
        "w_ffn_out": w_ffn_out.astype(BF16),
    }

    def zero_state(n):
        return {"c": jnp.zeros((n, H_AB, HEAD_DIM, HEAD_DIM), F32),
                "n": jnp.zeros((n, H_AB, 1, LANES), F32),
                "m": jnp.zeros((n, H_AB, 1, LANES), F32),
                "r": jnp.zeros((n, H_AB, HEAD_DIM, HEAD_DIM), F32),
                "s": jnp.zeros((n, n_hc, HEAD_DIM, HEAD_DIM), F32),
                "conv": (jnp.zeros((n, CONV_W - 1, d_ff), F32), jnp.zeros((n, CONV_W - 1, d_ff), F32))}

    _, st_meta = _trunk(meta_tokens[None].astype(F32), jnp.arange(N_META, dtype=F32), zero_state(1), w,
                        chunk_ab=N_META, chunk_c=N_META, sb_ab=1, sb_c=1, unroll_ab=1, group_c=4)
    pos_p = N_META + jnp.arange(seq, dtype=F32)
    y_prompt, st_p = _trunk(x_prompt, pos_p, st_meta, w, chunk_ab=128, chunk_c=128,
                            sb_ab=min(bp, 2), sb_c=min(bp, 2), unroll_ab=min(bp, 2), group_c=4)
    st_s = {"c": state_mlstm_C[0], "n": state_mlstm_n[0][:, :, None, :], "m": _lane_rep(state_mlstm_m[0]),
            "r": state_ret_S[0], "s": state_hgrn_S[0], "conv": (state_ffn_conv[0], state_ffn_conv[1])}
    pos_s = PAST_LEN + jnp.arange(dec_seq, dtype=F32)
    y_sample, st_s = _trunk(x_sample, pos_s, st_s, w, chunk_ab=dec_seq, chunk_c=dec_seq,
                            sb_ab=min(bs, 16), sb_c=min(bs, 16), unroll_ab=min(bs, 2), group_c=n_hc)

    def outs(s):
        return (s["c"][None], s["n"][:, :, 0, :][None], s["m"][:, :, 0, 0][None], s["r"][None], s["s"][None],
                jnp.stack(s["conv"]))

    cp, n_p, mp, rp, sp, convp = outs(st_p)
    cs, n_s, ms, rs, ss, convs = outs(st_s)
    return (y_prompt, y_sample, cp, cs, n_p, n_s, mp, ms, rp, rs, sp, ss, convp, convs)
```

```python
import functools

import numpy as np

import jax
import jax.numpy as jnp
from jax import lax
from jax.experimental import pallas as pl
from jax.experimental.pallas import tpu as pltpu

EPS = 1e-6
N_META = 16
PAST_LEN = 16384
ROPE_BASE = 10000.0
HEAD_DIM = 128
H_AB = 4
CONV_W = 3
LANES = 128
SUBLANES = 8
VMEM_LIMIT = 56 * 1024 * 1024
F32 = jnp.float32
BF16 = jnp.bfloat16


def _bf(x):
    return x.astype(BF16)


def _dot(a, b):
    return jnp.dot(_bf(a), _bf(b), preferred_element_type=F32)


def _dot_nt(a, b):
    return lax.dot_general(_bf(a), _bf(b), (((1,), (1,)), ((), ())), preferred_element_type=F32)


def _dot_tn(a, b):
    return lax.dot_general(_bf(a), _bf(b), (((0,), (0,)), ((), ())), preferred_element_type=F32)


def _rms_norm(x, gain):
    y = x * lax.rsqrt(jnp.mean(x * x, axis=-1, keepdims=True) + EPS)
    return y * gain


def _sigmoid(x):
    return 1.0 / (1.0 + jnp.exp(-x))


def _layer_norm(x, gain, tick):
    mu = jnp.mean(x, axis=-1, keepdims=True)
    if x.shape[0] > SUBLANES:
        tick()
        yield
    cen = x - mu
    var = jnp.mean(cen * cen, axis=-1, keepdims=True)
    if x.shape[0] > SUBLANES:
        tick()
        yield
    return cen * lax.rsqrt(var + EPS) * gain


def _cumsum_rows(x):
    n = x.shape[0]
    row = lax.broadcasted_iota(jnp.int32, x.shape, 0)
    k = 1
    while k < n:
        x = x + jnp.where(row >= k, pltpu.roll(x, k, 0), 0.0)
        k *= 2
    return x


def _split_dot(mat_bf16, x):
    hi = _bf(x)
    lo = _bf(x - hi.astype(F32))
    n = x.shape[1]
    r = jnp.dot(mat_bf16, jnp.concatenate([hi, lo], axis=1), preferred_element_type=F32)
    return r[:, :n] + r[:, n:]


def _lock_step(gens):
    results = [None] * len(gens)
    live = list(enumerate(gens))
    while live:
        still = []
        for idx, g in live:
            try:
                next(g)
                still.append((idx, g))
            except StopIteration as stop:
                results[idx] = stop.value
        live = still
    return results


def _params(n_grid):
    return pltpu.CompilerParams(dimension_semantics=("arbitrary",) * n_grid,
                                vmem_limit_bytes=VMEM_LIMIT)


def _ab_kernel(*refs, sb, L, unroll, n_chunks):
    assert sb % unroll == 0
    refs = list(refs)
    xc_ref = refs.pop(0)
    xnext_ref = refs.pop(0) if n_chunks > 1 else None
    (gnm_ref, w_ref, cos_ref, sin_ref, bias_ref, lg_ref, gna_ref, gnb_ref,
     c0_ref, n0_ref, m0_ref, r0_ref) = [refs.pop(0) for _ in range(12)]
    if L > SUBLANES:
        tri_ref = refs.pop(0)
    mxa_ref, mxb_ref, c_ref, n_ref, m_ref, r_ref, za_ref = [refs.pop(0) for _ in range(7)]
    if n_chunks > 1:
        zb_ref, xnb_ref = refs
    chunk = pl.program_id(1)
    rows_all = sb * L
    d_model = xc_ref.shape[-1]

    @pl.when(chunk == 0)
    def _():
        c_ref[...] = jnp.broadcast_to(c0_ref[...], c_ref.shape)
        n_ref[...] = jnp.broadcast_to(n0_ref[...], n_ref.shape)
        m_ref[...] = jnp.broadcast_to(m0_ref[...], m_ref.shape)
        r_ref[...] = jnp.broadcast_to(r0_ref[...], r_ref.shape)
        xn0 = _rms_norm(xc_ref[...].reshape(rows_all, d_model), gnm_ref[...])
        za_ref[...] = _dot(xn0, w_ref[...])

    if n_chunks > 1:
        xnb_ref[...] = _bf(_rms_norm(xnext_ref[...].reshape(rows_all, d_model), gnm_ref[...]))
    cur = {}

    def seq_rows(s):
        return pl.ds(pl.multiple_of(s * L, L), L)

    def project_next(s, c0, c1):
        if cur["zn"] is not None:
            rows = seq_rows(s)
            cur["zn"][rows, c0:c1] = jnp.dot(xnb_ref[rows, :], w_ref[:, c0:c1], preferred_element_type=F32)

    def ticker(s, slab, first_round):
        half_w = 2 * HEAD_DIM
        pieces = {first_round: 0, first_round + 1: 1} if cur["zn"] is not None else {}
        rounds = [0]

        def tick():
            rounds[0] += 1
            half = pieces.pop(rounds[0], None)
            if half is not None:
                c0 = (2 * slab + half) * half_w
                project_next(s, c0, c0 + half_w)

        tick.pending = pieces
        return tick

    scale = HEAD_DIM ** -0.5
    row = lax.broadcasted_iota(jnp.int32, (L, L), 0)
    col = lax.broadcasted_iota(jnp.int32, (L, L), 1)
    causal = row >= col
    eye = row == col
    rel = jnp.where(causal, row - col, 0).astype(F32)
    tcol = lax.broadcasted_iota(jnp.int32, (L, 1), 0).astype(F32)
    cosv = cos_ref[...]
    sinv = sin_ref[...]
    bias = bias_ref[...]
    gate_col = 8 * H_AB * HEAD_DIM
    ones_bf = jnp.ones((L, LANES), BF16)
    ret_tabs = []
    for h in range(H_AB):
        lg = lg_ref[h][:, :1]
        ret_tabs.append((jnp.where(causal, jnp.exp(rel * lg), 0.0),
                         jnp.broadcast_to(jnp.exp((tcol + 1.0) * lg), (L, LANES)),
                         jnp.broadcast_to(jnp.exp((L - 1.0 - tcol) * lg), (L, LANES)),
                         jnp.exp(L * lg)))

    def gates(s):
        project_next(s, gate_col, gate_col + LANES)
        pre = cur["zc"][seq_rows(s), gate_col:gate_col + LANES] + bias
        lf = jnp.minimum(pre, 0.0) - jnp.log1p(jnp.exp(-jnp.abs(pre)))
        lf = pltpu.roll(lf, LANES - H_AB, 1)
        if L > SUBLANES:
            b = _split_dot(tri_ref[...], lf)
            yield
        else:
            b = _cumsum_rows(lf)
        a = pre - b
        a_rows = jnp.transpose(a) if L == LANES else None
        return a, b, a_rows

    def zcol(s, j, h):
        c0 = (j * H_AB + h) * HEAD_DIM
        return cur["zc"][seq_rows(s), c0:c0 + HEAD_DIM]

    def mlstm_head(s, h, a, b, a_rows):
        tick = ticker(s, 2 * h, 1 + 2 * (h % 2))
        hs = slice(h * HEAD_DIM, (h + 1) * HEAD_DIM)
        a_col = a[:, h:h + 1]
        b_col = b[:, h:h + 1]
        if a_rows is not None:
            a_row = a_rows[h:h + 1, :]
        else:
            a_row = jnp.sum(jnp.where(eye, a_col, 0.0), axis=0, keepdims=True)
        m0 = m_ref[s, h][:, :1]
        a_caus = jnp.where(causal, a_row, -jnp.inf)
        a_max = jnp.max(a_caus, axis=-1, keepdims=True)
        if L > SUBLANES:
            tick()
            yield
        mx = jnp.maximum(m0, a_max)
        m_t = b_col + mx
        d = jnp.exp(a_caus - mx)
        inter = jnp.broadcast_to(jnp.exp(m0 - mx), (L, LANES))
        q_f32 = zcol(s, 0, h)
        q = _bf(q_f32)
        k = zcol(s, 1, h) * scale
        v = _bf(zcol(s, 2, h))
        c_prev = c_ref[s, h]
        n_prev = n_ref[s, h]
        qk = _dot_nt(q, k)
        qc = _dot(q, c_prev)
        tick()
        yield
        sm_f32 = qk * d
        sm = _bf(sm_f32)
        m_new = m_t[L - 1:L, :]
        b_last = b_col[L - 1:L, :]
        w_col = jnp.exp(a_col + b_last - m_new)
        keep = jnp.exp(b_last + m0 - m_new)
        kw = k * w_col
        if L > SUBLANES:
            nd = jnp.dot(sm, jnp.concatenate([v, ones_bf], axis=1), preferred_element_type=F32)
            qn = _dot_nt(q, jnp.broadcast_to(n_prev, (HEAD_DIM, LANES)))
            kv = _dot_tn(kw, v)
            tick()
            yield
            num = nd[:, :LANES] + inter * qc
            den = nd[:, LANES:] + inter * qn
        else:
            sv = _dot(sm, v)
            kv = _dot_tn(kw, v)
            tick()
            yield
            num = sv + inter * qc
            den = (jnp.sum(sm_f32, axis=-1, keepdims=True)
                   + inter * jnp.sum(q_f32 * n_prev, axis=-1, keepdims=True))
        hv = num / jnp.maximum(jnp.abs(den), jnp.exp(-m_t))
        y = yield from _layer_norm(hv, gna_ref[:, hs], tick)
        assert not tick.pending, "not enough lock-step rounds for the projection pieces"
        mxa_ref[s, :, hs] = (_sigmoid(zcol(s, 3, h)) * y).astype(mxa_ref.dtype)
        return [(c_ref, h, keep * c_prev + kv),
                (n_ref, h, keep * n_prev + jnp.sum(kw, axis=0, keepdims=True)),
                (m_ref, h, jnp.broadcast_to(m_new, (1, LANES)))]

    def ret_head(s, h):
        tick = ticker(s, 2 * h + 1, 3 - 2 * (h % 2))
        hs = slice(h * HEAD_DIM, (h + 1) * HEAD_DIM)
        decay, inner, tail, g_pow = ret_tabs[h]
        q2 = zcol(s, 4, h)
        k2 = zcol(s, 5, h)
        v2 = _bf(zcol(s, 6, h))
        qr = _bf(q2 * cosv + pltpu.roll(q2, HEAD_DIM // 2, 1) * sinv)
        kr = (k2 * cosv + pltpu.roll(k2, HEAD_DIM // 2, 1) * sinv) * scale
        r_prev = r_ref[s, h]
        qk = _dot_nt(qr, kr)
        qs = _dot(qr, r_prev)
        kv = _dot_tn(kr * tail, v2)
        tick()
        yield
        o2 = _dot(qk * decay, v2)
        tick()
        yield
        o2 = o2 + qs * inner
        y2 = yield from _layer_norm(o2, gnb_ref[:, hs], tick)
        assert not tick.pending, "not enough lock-step rounds for the projection pieces"
        gv = zcol(s, 7, h)
        mxb_ref[s, :, hs] = (gv * _sigmoid(gv) * y2).astype(mxb_ref.dtype)
        return [(r_ref, h, g_pow * r_prev + kv)]

    def group(i, carry_):
        seqs = [i * unroll + j for j in range(unroll)]
        prep = _lock_step([gates(s) for s in seqs])
        chains = []
        for s, (a, b, a_rows) in zip(seqs, prep):
            for h in range(H_AB):
                chains.append((s, mlstm_head(s, h, a, b, a_rows)))
                chains.append((s, ret_head(s, h)))
        new = _lock_step([g for _, g in chains])
        for (s, _), new_state in zip(chains, new):
            for ref, h, val in new_state:
                ref[s, h] = val
        return carry_

    def step(zc_ref, zn_ref):
        cur["zc"], cur["zn"] = zc_ref, zn_ref
        lax.fori_loop(0, sb // unroll, lambda i, carry_: group(i, carry_), 0)

    if n_chunks > 1:
        @pl.when(chunk % 2 == 0)
        def _():
            step(za_ref, zb_ref)

        @pl.when(chunk % 2 == 1)
        def _():
            step(zb_ref, za_ref)
    else:
        step(za_ref, None)


def _ab_mixer(x, gnm, w, cos_t, sin_t, bias, lg, gna, gnb, c0, n0, m0, r0, *, chunk, sb, unroll):
    n_seq, seq_len, d = x.shape
    n_chunks = seq_len // chunk
    width = w.shape[1]
    hd = H_AB * HEAD_DIM
    bcast = c0.shape[0] == 1 and n_seq > 1
    sb0 = 1 if bcast else sb
    st = (lambda i: 0) if bcast else (lambda i: i)
    full2 = lambda a: pl.BlockSpec(a.shape, lambda i, c: (0, 0))
    mat_in = pl.BlockSpec((sb0, H_AB, HEAD_DIM, HEAD_DIM), lambda i, c: (st(i), 0, 0, 0))
    vec_in = pl.BlockSpec((sb0, H_AB, 1, LANES), lambda i, c: (st(i), 0, 0, 0))
    mat_out = pl.BlockSpec((sb, H_AB, HEAD_DIM, HEAD_DIM), lambda i, c: (i, 0, 0, 0))
    vec_out = pl.BlockSpec((sb, H_AB, 1, LANES), lambda i, c: (i, 0, 0, 0))
    mix_out = pl.BlockSpec((sb, chunk, hd), lambda i, c: (i, c, 0))
    tab = pl.BlockSpec((chunk, LANES), lambda i, c: (c, 0))
    in_specs = [pl.BlockSpec((sb, chunk, d), lambda i, c: (i, c, 0))]
    args = [x]
    if n_chunks > 1:
        in_specs.append(pl.BlockSpec((sb, chunk, d), lambda i, c: (i, jnp.minimum(c + 1, n_chunks - 1), 0)))
        args.append(x)
    w_spec = pl.BlockSpec(w.shape, lambda i, c: (0, 0), pipeline_mode=pl.Buffered(1))
    in_specs += [full2(gnm), w_spec, tab, tab, full2(bias),
                 pl.BlockSpec(lg.shape, lambda i, c: (0, 0, 0)), full2(gna), full2(gnb),
                 mat_in, vec_in, vec_in, mat_in]
    args += [gnm, w, cos_t, sin_t, bias, lg, gna, gnb, c0, n0, m0, r0]
    if chunk > SUBLANES:
        tri = jnp.asarray(np.tril(np.ones((chunk, chunk), np.float32)), BF16)
        in_specs.append(full2(tri))
        args.append(tri)
    scratch = [pltpu.VMEM((sb * chunk, width), F32)]
    if n_chunks > 1:
        scratch += [pltpu.VMEM((sb * chunk, width), F32), pltpu.VMEM((sb * chunk, d), BF16)]
    out_shape = (jax.ShapeDtypeStruct((n_seq, seq_len, hd), BF16),
                 jax.ShapeDtypeStruct((n_seq, seq_len, hd), BF16),
                 jax.ShapeDtypeStruct((n_seq, H_AB, HEAD_DIM, HEAD_DIM), F32),
                 jax.ShapeDtypeStruct((n_seq, H_AB, 1, LANES), F32),
                 jax.ShapeDtypeStruct((n_seq, H_AB, 1, LANES), F32),
                 jax.ShapeDtypeStruct((n_seq, H_AB, HEAD_DIM, HEAD_DIM), F32))
    return pl.pallas_call(
        functools.partial(_ab_kernel, sb=sb, L=chunk, unroll=unroll, n_chunks=n_chunks),
        grid=(n_seq // sb, n_chunks),
        in_specs=in_specs,
        out_specs=(mix_out, mix_out, mat_out, vec_out, vec_out, mat_out),
        out_shape=out_shape,
        scratch_shapes=scratch,
        compiler_params=_params(2),
        name="ab_mixer",
    )(*args)


def _hgrn_level_tables(L):
    t = np.arange(L)[:, None]
    u = np.arange(L)[None, :]
    lvl = np.full((L, L), -1, np.int32)
    size, j = 1, 0
    while size < L:
        upper = ((t // size) % 2) == 1
        lvl[((t // size) == (u // size) + 1) & upper] = j
        size *= 2
        j += 1
    return (u <= t).astype(np.float32), lvl


def _hgrn_small_level_factors(f, r):
    one = jnp.ones_like(f)
    prev = lambda x, d: pltpu.roll(x, d, 1)
    nxt = lambda x, d: pltpu.roll(x, SUBLANES - d, 1)
    f_n1 = nxt(f, 1)
    p1 = f * prev(f, 1)
    sel = lambda idx, *vals: functools.reduce(
        lambda acc, iv: jnp.where(idx == iv[0], iv[1], acc), list(enumerate(vals))[:-1], vals[-1])
    g0 = jnp.where((r & 1) == 1, f, one)
    g1 = sel(r & 3, f_n1, one, f, p1)
    g2 = sel(r, f_n1 * nxt(p1, 3), nxt(p1, 2), f_n1, one, f, p1, p1 * prev(f, 2), p1 * prev(p1, 2))
    return [g0, g1, g2]


def _hgrn_kernel(*refs, sb, L, layer, n_heads, n_chunks, group_heads):
    refs = list(refs)
    xc_ref = refs.pop(0)
    xnext_ref = refs.pop(0) if n_chunks > 1 else None
    gnm_ref, w_ref, lbl_ref, gn_ref, s0_ref = [refs.pop(0) for _ in range(5)]
    if L > SUBLANES:
        tri_ref, lvl_ref = refs.pop(0), refs.pop(0)
    mx_ref, s_ref, za_ref = refs.pop(0), refs.pop(0), refs.pop(0)
    if n_chunks > 1:
        zb_ref, xnb_ref = refs
    chunk = pl.program_id(1)
    rows_all = sb * L
    d_model = xc_ref.shape[-1]

    @pl.when(chunk == 0)
    def _():
        s_ref[...] = jnp.broadcast_to(s0_ref[...], s_ref.shape)
        xn0 = _rms_norm(xc_ref[...].reshape(rows_all, d_model), gnm_ref[...])
        za_ref[...] = _dot(xn0, w_ref[...])

    if n_chunks > 1:
        xnb_ref[...] = _bf(_rms_norm(xnext_ref[...].reshape(rows_all, d_model), gnm_ref[...]))

    logits = lbl_ref[...]
    ex = jnp.exp(logits - jnp.max(logits, axis=0, keepdims=True))
    p = ex / jnp.sum(ex, axis=0, keepdims=True)
    cum = p[0:1, :]
    for r in range(1, layer + 1):
        cum = cum + p[r:r + 1, :]
    lb_all = cum - p[0:1, :]

    row = lax.broadcasted_iota(jnp.int32, (L, LANES), 0)
    n_levels = L.bit_length() - 1
    unit_w = 4 * HEAD_DIM
    width = n_heads * HEAD_DIM

    def unit(s, h, zc_ref, zn_ref, slot):
        rows = pl.ds(pl.multiple_of(s * L, L), L)
        base = h * unit_w
        hs = slice(h * HEAD_DIM, (h + 1) * HEAD_DIM)
        zblock = lambda j: zc_ref[rows, j * width + h * HEAD_DIM:j * width + (h + 1) * HEAD_DIM]
        pieces = {2 * slot + 1: 0, 2 * slot + 2: 1} if zn_ref is not None else {}
        rounds = [0]

        def tick():
            rounds[0] += 1
            half = pieces.pop(rounds[0], None)
            if half is not None:
                c0 = base + half * (unit_w // 2)
                zn_ref[rows, c0:c0 + unit_w // 2] = jnp.dot(xnb_ref[rows, :], w_ref[:, c0:c0 + unit_w // 2],
                                                            preferred_element_type=F32)

        lb = lb_all[:, hs]
        qv = zblock(0)
        fv = zblock(1)
        iv = zblock(2)
        e = jnp.exp(-jnp.abs(fv))
        rcp = 1.0 / (1.0 + e)
        pos = fv >= 0.0
        sig = jnp.where(pos, rcp, e * rcp)
        sig_neg = jnp.where(pos, e * rcp, rcp)
        f_gate = lb + (1.0 - lb) * sig
        log_f = jnp.log(f_gate)
        k = (1.0 - lb) * sig_neg
        s_prev = s_ref[s, h]
        diag = jnp.sum(qv * k, axis=-1, keepdims=True)

        if L > SUBLANES:
            b = _split_dot(tri_ref[...], log_f)
            tick()
            yield
            nt = L // SUBLANES
            tile3 = lambda x: x.reshape(nt, SUBLANES, x.shape[-1])
            q3, k3, b3 = tile3(qv), tile3(k), tile3(b)
            r3 = lax.broadcasted_iota(jnp.int32, (1, SUBLANES, LANES), 1)
            small = _hgrn_small_level_factors(tile3(f_gate), r3)
            acc = [jnp.zeros((SUBLANES, L), F32)] * nt
            for j in range(n_levels):
                size = 1 << j
                if size < SUBLANES:
                    xj = (jnp.where((r3 & size) != 0, q3, k3) * small[j]).reshape(L, LANES)
                    up = list(range(nt))
                    a_j = _dot_nt(xj, xj)
                else:
                    s8 = size // SUBLANES
                    up = [i for i in range(nt) if i & s8]
                    mids = {i: jnp.broadcast_to(b3[i, SUBLANES - 1:, :], (SUBLANES, LANES))
                            for i in range(s8 - 1, nt, 2 * s8)}
                    b_mid = jnp.stack([mids[(i // (2 * s8)) * 2 * s8 + s8 - 1] for i in range(nt)])
                    fac = jnp.exp(-jnp.abs(b3 - b_mid))
                    x3 = jnp.stack([q3[i] if i & s8 else k3[i] for i in range(nt)]) * fac
                    xj = x3.reshape(L, LANES)
                    a_j = _dot_nt(jnp.concatenate([x3[i] for i in up], axis=0), xj)
                tick()
                yield
                for n, i in enumerate(up):
                    keep = lvl_ref[i * SUBLANES:(i + 1) * SUBLANES, :] == j
                    acc[i] = jnp.where(keep, a_j[n * SUBLANES:(n + 1) * SUBLANES, :], acc[i])
            o = diag * iv + _dot(jnp.concatenate(acc, axis=0), iv)
            tick()
            yield
        else:
            b = _cumsum_rows(log_f)
            pair = []
            for j in range(1, L):
                dec = jnp.exp(jnp.minimum(b - pltpu.roll(b, j, 0), 0.0))
                pair.append(jnp.sum(jnp.where(row >= j, qv * pltpu.roll(k, j, 0) * dec, 0.0),
                                    axis=-1, keepdims=True))
            tick()
            yield
            o = diag * iv
            for j, a in enumerate(pair, start=1):
                o = o + a * pltpu.roll(iv, j, 0)

        o = o + _dot(qv * jnp.exp(b), s_prev)
        tick()
        yield
        b_last = b[L - 1:L, :]
        e_col = jnp.transpose(jnp.broadcast_to(jnp.exp(b_last), (HEAD_DIM, LANES)))
        s_new = e_col * s_prev + _dot_tn(k * jnp.exp(b_last - b), iv)
        ms = jnp.mean(o * o, axis=-1, keepdims=True)
        tick()
        yield
        assert not pieces, "not enough lock-step rounds for the projection pieces"
        y = o * lax.rsqrt(ms + EPS) * gn_ref[:, hs]
        gv = zblock(3)
        mx_ref[s, :, hs] = (y * (gv * _sigmoid(gv))).astype(mx_ref.dtype)
        return s_new

    def step(zc_ref, zn_ref):
        def per_seq(s, carry_):
            for h0 in range(0, n_heads, group_heads):
                heads = list(range(h0, h0 + group_heads))
                new = _lock_step([unit(s, h, zc_ref, zn_ref, slot) for slot, h in enumerate(heads)])
                for h, s_new in zip(heads, new):
                    s_ref[s, h] = s_new
            return carry_
        lax.fori_loop(0, sb, per_seq, 0)

    if n_chunks > 1:
        @pl.when(chunk % 2 == 0)
        def _():
            step(za_ref, zb_ref)

        @pl.when(chunk % 2 == 1)
        def _():
            step(zb_ref, za_ref)
    else:
        step(za_ref, None)


def _hgrn_mixer(x, gnm, w, lb_logits, gain, s0, *, chunk, sb, layer, group_heads):
    n_seq, seq_len, d = x.shape
    n_heads = s0.shape[1]
    n_chunks = seq_len // chunk
    width = n_heads * HEAD_DIM
    bcast = s0.shape[0] == 1 and n_seq > 1
    sb0 = 1 if bcast else sb
    st = (lambda i: 0) if bcast else (lambda i: i)
    full2 = lambda a: pl.BlockSpec(a.shape, lambda i, c: (0, 0))
    in_specs = [pl.BlockSpec((sb, chunk, d), lambda i, c: (i, c, 0))]
    args = [x]
    if n_chunks > 1:
        in_specs.append(pl.BlockSpec((sb, chunk, d), lambda i, c: (i, jnp.minimum(c + 1, n_chunks - 1), 0)))
        args.append(x)
    w_spec = pl.BlockSpec(w.shape, lambda i, c: (0, 0), pipeline_mode=pl.Buffered(1))
    in_specs += [full2(gnm), w_spec, full2(lb_logits), full2(gain),
                 pl.BlockSpec((sb0, n_heads, HEAD_DIM, HEAD_DIM), lambda i, c: (st(i), 0, 0, 0))]
    args += [gnm, w, lb_logits, gain, s0]
    if chunk > SUBLANES:
        tri, lvl = _hgrn_level_tables(chunk)
        in_specs += [pl.BlockSpec(tri.shape, lambda i, c: (0, 0)), pl.BlockSpec(lvl.shape, lambda i, c: (0, 0))]
        args += [jnp.asarray(tri, BF16), jnp.asarray(lvl)]
    out_specs = (pl.BlockSpec((sb, chunk, width), lambda i, c: (i, c, 0)),
                 pl.BlockSpec((sb, n_heads, HEAD_DIM, HEAD_DIM), lambda i, c: (i, 0, 0, 0)))
    out_shape = (jax.ShapeDtypeStruct((n_seq, seq_len, width), BF16),
                 jax.ShapeDtypeStruct((n_seq, n_heads, HEAD_DIM, HEAD_DIM), F32))
    scratch = [pltpu.VMEM((sb * chunk, w.shape[1]), F32)]
    if n_chunks > 1:
        scratch += [pltpu.VMEM((sb * chunk, w.shape[1]), F32), pltpu.VMEM((sb * chunk, d), BF16)]
    return pl.pallas_call(
        functools.partial(_hgrn_kernel, sb=sb, L=chunk, layer=layer, n_heads=n_heads, n_chunks=n_chunks,
                          group_heads=group_heads),
        grid=(n_seq // sb, n_chunks),
        in_specs=in_specs,
        out_specs=out_specs,
        out_shape=out_shape,
        scratch_shapes=scratch,
        compiler_params=_params(2),
        name="hgrn_mixer",
    )(*args)


def _ffn_kernel(*refs, n_mixed, s_blk, L, d_ff, final_norm):
    x_ref = refs[0]
    mix_refs = refs[1:1 + n_mixed]
    wo_refs = refs[1 + n_mixed:1 + 2 * n_mixed]
    (gn_ref, win_ref, cw_ref, cb_ref, wout_ref, buf_ref) = refs[1 + 2 * n_mixed:7 + 2 * n_mixed]
    pos = 7 + 2 * n_mixed
    if final_norm:
        gfin_ref = refs[pos]
        pos += 1
    xo_ref, bufo_ref, carry_ref = refs[pos:pos + 3]
    tile = pl.program_id(1)
    tm = s_blk * L

    @pl.when(tile == 0)
    def _():
        carry_ref[...] = jnp.broadcast_to(buf_ref[...], carry_ref.shape)

    x1 = x_ref[...]
    for m_ref, w_ref in zip(mix_refs, wo_refs):
        x1 = x1 + jnp.dot(m_ref[...], w_ref[...], preferred_element_type=F32)
    xn = _rms_norm(x1, gn_ref[...])
    ug = _dot(xn, win_ref[...])
    u = ug[:, :d_ff]
    gate = ug[:, d_ff:]

    t = lax.broadcasted_iota(jnp.int32, (tm, 1), 0) & (L - 1)
    if s_blk == 1:
        p0 = carry_ref[0, 0:1, :]
        p1 = carry_ref[0, 1:2, :]
    else:
        cr = carry_ref[...]
        p0 = jnp.broadcast_to(cr[:, 0:1, :], (s_blk, L, d_ff)).reshape(tm, d_ff)
        p1 = jnp.broadcast_to(cr[:, 1:2, :], (s_blk, L, d_ff)).reshape(tm, d_ff)
    prev1 = jnp.where(t == 0, p1, pltpu.roll(u, 1, 0))
    prev2 = jnp.where(t == 0, p0, jnp.where(t == 1, p1, pltpu.roll(u, 2, 0)))
    cw = cw_ref[...]
    conv = cb_ref[...] + (prev2 * cw[0:1, :] + prev1 * cw[1:2, :] + u * cw[2:3, :])
    hid = conv * _sigmoid(conv) * gate
    x2 = x1 + _dot(hid, wout_ref[...])
    if final_norm:
        xo_ref[...] = _rms_norm(x2, gfin_ref[...])
    else:
        xo_ref[...] = x2

    if s_blk == 1:
        carry_ref[0] = u[tm - (CONV_W - 1):tm, :]
    else:
        carry_ref[...] = u.reshape(s_blk, L, d_ff)[:, L - (CONV_W - 1):L, :]
    bufo_ref[...] = carry_ref[...]


def _ffn(x2d, mixed, wo_parts, gn, w_in, cw, cb, w_out, buf, gfin, *, n_seq, seq_len, tm):
    rows, d = x2d.shape
    d_ff = w_out.shape[0]
    if tm <= seq_len:
        s_blk, L = 1, tm
        tiles = seq_len // tm
        grid = (n_seq, tiles)
        rowmap = lambda s, j: (s * tiles + j, 0)
    else:
        s_blk, L = tm // seq_len, seq_len
        grid = (n_seq // s_blk, 1)
        rowmap = lambda s, j: (s, 0)
    bcast = buf.shape[0] == 1 and n_seq > 1
    bufmap = (lambda s, j: (0, 0, 0)) if bcast else (lambda s, j: (s, 0, 0))
    buf_blk = 1 if bcast else s_blk
    const = lambda s, j: (0, 0)
    n_mixed = len(mixed)
    in_specs = [pl.BlockSpec((tm, d), rowmap)]
    in_specs += [pl.BlockSpec((tm, m.shape[1]), rowmap) for m in mixed]
    once = lambda a: pl.BlockSpec(a.shape, const, pipeline_mode=pl.Buffered(1))
    in_specs += [once(w) for w in wo_parts]
    in_specs += [pl.BlockSpec((1, d), const),
                 once(w_in),
                 pl.BlockSpec(cw.shape, const),
                 pl.BlockSpec((1, d_ff), const),
                 once(w_out),
                 pl.BlockSpec((buf_blk, CONV_W - 1, d_ff), bufmap)]
    args = [x2d, *mixed, *wo_parts, gn, w_in, cw, cb, w_out, buf]
    if gfin is not None:
        in_specs.append(pl.BlockSpec((1, d), const))
        args.append(gfin)
    out_specs = (pl.BlockSpec((tm, d), rowmap),
                 pl.BlockSpec((s_blk, CONV_W - 1, d_ff), lambda s, j: (s, 0, 0)))
    out_shape = (jax.ShapeDtypeStruct((rows, d), F32),
                 jax.ShapeDtypeStruct((n_seq, CONV_W - 1, d_ff), F32))
    return pl.pallas_call(
        functools.partial(_ffn_kernel, n_mixed=n_mixed, s_blk=s_blk, L=L, d_ff=d_ff,
                          final_norm=gfin is not None),
        grid=grid,
        in_specs=in_specs,
        out_specs=out_specs,
        out_shape=out_shape,
        scratch_shapes=[pltpu.VMEM((s_blk, CONV_W - 1, d_ff), F32)],
        compiler_params=_params(2),
        name="ffn",
    )(*args)


def _tile_rows(n_seq, seq_len, target):
    rows = n_seq * seq_len
    if seq_len >= target:
        return target
    return min(rows, target)


def _trunk(x, pos, st, w, *, chunk_ab, chunk_c, sb_ab, sb_c, unroll_ab, group_c):
    n_seq, seq_len, d = x.shape
    rows = n_seq * seq_len
    tm = _tile_rows(n_seq, seq_len, 512)
    x2d = x.reshape(rows, d)

    half = HEAD_DIM // 2
    inv = 1.0 / (ROPE_BASE ** jnp.linspace(0.0, 1.0, half, dtype=F32))
    ang = pos[:, None] * inv[None, :]
    cos_t = jnp.concatenate([jnp.cos(ang), jnp.cos(ang)], axis=-1)
    sin_t = jnp.concatenate([-jnp.sin(ang), jnp.sin(ang)], axis=-1)

    mxa, mxb, c_new, n_new, m_new, r_new = _ab_mixer(
        x, w["norm_mix"][0], w["w_in_ab"], cos_t, sin_t, w["gate_bias"], w["lg"], w["gn_a"], w["gn_b"],
        st["c"], st["n"], st["m"], st["r"], chunk=chunk_ab, sb=sb_ab, unroll=unroll_ab)
    hd = H_AB * HEAD_DIM
    x2d, buf0 = _ffn(x2d, [mxa.reshape(rows, hd), mxb.reshape(rows, hd)], w["w_out_ab"],
                     w["norm_ffn"][0], w["w_ffn_in"][0], w["conv_w"][0], w["conv_b"][0], w["w_ffn_out"][0],
                     st["conv"][0], None, n_seq=n_seq, seq_len=seq_len, tm=tm)

    mx, s_new = _hgrn_mixer(x2d.reshape(n_seq, seq_len, d), w["norm_mix"][1], w["w_in_c"], w["lb_logits"],
                            w["gn_c"], st["s"], chunk=chunk_c, sb=sb_c, layer=1, group_heads=group_c)
    y2d, buf1 = _ffn(x2d, [mx.reshape(rows, -1)], w["w_out_c"],
                     w["norm_ffn"][1], w["w_ffn_in"][1], w["conv_w"][1], w["conv_b"][1], w["w_ffn_out"][1],
                     st["conv"][1], w["norm_final"], n_seq=n_seq, seq_len=seq_len, tm=tm)
    new_st = {"c": c_new, "n": n_new, "m": m_new, "r": r_new, "s": s_new, "conv": (buf0, buf1)}
    return y2d.reshape(n_seq, seq_len, d), new_st


def _lane_rep(v):
    return jnp.broadcast_to(v[..., None, None], v.shape + (1, LANES))


def kernel(x_prompt, x_sample, state_mlstm_C, state_mlstm_n, state_mlstm_m, state_ret_S, state_hgrn_S,
           state_ffn_conv, meta_tokens, norm_mix, w_in_ab, b_igate, b_fgate, gn_mlstm, gn_ret, w_out_ab,
           lb_logits, w_in_c, gn_hgrn, w_out_c, norm_ffn, w_ffn_in, conv_w, conv_b, w_ffn_out, norm_final):
    bp, seq, d = x_prompt.shape
    bs, dec_seq, _ = x_sample.shape
    hd = H_AB * HEAD_DIM
    n_hc = state_hgrn_S.shape[2]
    d_ff = w_ffn_out.shape[1]
    assert w_in_ab.shape[0] == 1 and w_in_c.shape[0] == 1 and norm_mix.shape[0] == 2

    wab = w_in_ab[0].astype(BF16)
    g0 = 4 * hd
    w_ab = jnp.concatenate([wab[:, :g0], wab[:, g0 + 2 * H_AB:], wab[:, g0:g0 + 2 * H_AB],
                            jnp.zeros((d, LANES - 2 * H_AB), BF16)], axis=1)
    wo_ab = w_out_ab[0].astype(BF16)
    log_gamma = jnp.log1p(-jnp.exp2(-5.0 - jnp.arange(H_AB, dtype=F32)))
    gate_bias = jnp.concatenate([b_igate[0], b_fgate[0], jnp.zeros((LANES - 2 * H_AB,), F32)])[None, :]
    w = {
        "norm_mix": norm_mix[:, None, :],
        "norm_ffn": norm_ffn[:, None, :],
        "norm_final": norm_final[None, :],
        "w_in_ab": w_ab,
        "gate_bias": gate_bias,
        "lg": _lane_rep(log_gamma),
        "gn_a": gn_mlstm[0][None, :],
        "gn_b": gn_ret[0][None, :],
        "w_out_ab": [wo_ab[:hd], wo_ab[hd:]],
        "lb_logits": lb_logits,
        "w_in_c": w_in_c[0].astype(BF16),
        "gn_c": gn_hgrn[0][None, :],
        "w_out_c": [w_out_c[0].astype(BF16)],
        "w_ffn_in": w_ffn_in.astype(BF16),
        "conv_w": conv_w,
        "conv_b": conv_b[:, None, :],
        "w_ffn_out": w_ffn_out.astype(BF16),
    }

    def zero_state(n):
        return {"c": jnp.zeros((n, H_AB, HEAD_DIM, HEAD_DIM), F32),
                "n": jnp.zeros((n, H_AB, 1, LANES), F32),
                "m": jnp.zeros((n, H_AB, 1, LANES), F32),
                "r": jnp.zeros((n, H_AB, HEAD_DIM, HEAD_DIM), F32),
                "s": jnp.zeros((n, n_hc, HEAD_DIM, HEAD_DIM), F32),
                "conv": (jnp.zeros((n, CONV_W - 1, d_ff), F32), jnp.zeros((n, CONV_W - 1, d_ff), F32))}

    _, st_meta = _trunk(meta_tokens[None].astype(F32), jnp.arange(N_META, dtype=F32), zero_state(1), w,
                        chunk_ab=N_META, chunk_c=N_META, sb_ab=1, sb_c=1, unroll_ab=1, group_c=4)
    pos_p = N_META + jnp.arange(seq, dtype=F32)
    y_prompt, st_p = _trunk(x_prompt, pos_p, st_meta, w, chunk_ab=128, chunk_c=128,
                            sb_ab=min(bp, 2), sb_c=min(bp, 2), unroll_ab=min(bp, 2), group_c=4)
    st_s = {"c": state_mlstm_C[0], "n": state_mlstm_n[0][:, :, None, :], "m": _lane_rep(state_mlstm_m[0]),
            "r": state_ret_S[0], "s": state_hgrn_S[0], "conv": (state_ffn_conv[0], state_ffn_conv[1])}
    pos_s = PAST_LEN + jnp.arange(dec_seq, dtype=F32)
    y_sample, st_s = _trunk(x_sample, pos_s, st_s, w, chunk_ab=dec_seq, chunk_c=dec_seq,
                            sb_ab=min(bs, 16), sb_c=min(bs, 16), unroll_ab=min(bs, 2), group_c=n_hc)

    def outs(s):
        return (s["c"][None], s["n"][:, :, 0, :][None], s["m"][:, :, 0, 0][None], s["r"][None], s["s"][None],
                jnp.stack(s["conv"]))

    cp, n_p, mp, rp, sp, convp = outs(st_p)
    cs, n_s, ms, rs, ss, convs = outs(st_s)
    return (y_prompt, y_sample, cp, cs, n_p, n_s, mp, ms, rp, rs, sp, ss, convp, convs)
```

```python
import functools

import numpy as np

import jax
import jax.numpy as jnp
from jax import lax
from jax.experimental import pallas as pl
from jax.experimental.pallas import tpu as pltpu

EPS = 1e-6
N_META = 16
PAST_LEN = 16384
ROPE_BASE = 10000.0
HEAD_DIM = 128
H_AB = 4
CONV_W = 3
LANES = 128
SUBLANES = 8
VMEM_LIMIT = 56 * 1024 * 1024
F32 = jnp.float32
BF16 = jnp.bfloat16


def _bf(x):
    return x.astype(BF16)


def _dot(a, b):
    return jnp.dot(_bf(a), _bf(b), preferred_element_type=F32)


def _dot_nt(a, b):
    return lax.dot_general(_bf(a), _bf(b), (((1,), (1,)), ((), ())), preferred_element_type=F32)


def _dot_tn(a, b):
    return lax.dot_general(_bf(a), _bf(b), (((0,), (0,)), ((), ())), preferred_element_type=F32)


def _rms_norm(x, gain):
    y = x * lax.rsqrt(jnp.mean(x * x, axis=-1, keepdims=True) + EPS)
    return y * gain


def _sigmoid(x):
    return 1.0 / (1.0 + jnp.exp(-x))


def _layer_norm(x, gain, tick):
    mu = jnp.mean(x, axis=-1, keepdims=True)
    if x.shape[0] > SUBLANES:
        tick()
        yield
    cen = x - mu
    var = jnp.mean(cen * cen, axis=-1, keepdims=True)
    if x.shape[0] > SUBLANES:
        tick()
        yield
    return cen * lax.rsqrt(var + EPS) * gain


def _cumsum_rows(x):
    n = x.shape[0]
    row = lax.broadcasted_iota(jnp.int32, x.shape, 0)
    k = 1
    while k < n:
        x = x + jnp.where(row >= k, pltpu.roll(x, k, 0), 0.0)
        k *= 2
    return x


def _split_dot(mat_bf16, x):
    hi = _bf(x)
    lo = _bf(x - hi.astype(F32))
    n = x.shape[1]
    r = jnp.dot(mat_bf16, jnp.concatenate([hi, lo], axis=1), preferred_element_type=F32)
    return r[:, :n] + r[:, n:]


def _lock_step(gens):
    results = [None] * len(gens)
    live = list(enumerate(gens))
    while live:
        still = []
        for idx, g in live:
            try:
                next(g)
                still.append((idx, g))
            except StopIteration as stop:
                results[idx] = stop.value
        live = still
    return results


def _params(n_grid):
    return pltpu.CompilerParams(dimension_semantics=("arbitrary",) * n_grid,
                                vmem_limit_bytes=VMEM_LIMIT)


def _ab_kernel(*refs, sb, L, unroll, n_chunks):
    assert sb % unroll == 0
    refs = list(refs)
    xc_ref = refs.pop(0)
    xnext_ref = refs.pop(0) if n_chunks > 1 else None
    (gnm_ref, w_ref, cos_ref, sin_ref, bias_ref, lg_ref, gna_ref, gnb_ref,
     c0_ref, n0_ref, m0_ref, r0_ref) = [refs.pop(0) for _ in range(12)]
    if L > SUBLANES:
        tri_ref = refs.pop(0)
    mxa_ref, mxb_ref, c_ref, n_ref, m_ref, r_ref, za_ref = [refs.pop(0) for _ in range(7)]
    if n_chunks > 1:
        zb_ref, xnb_ref = refs
    chunk = pl.program_id(1)
    rows_all = sb * L
    d_model = xc_ref.shape[-1]

    @pl.when(chunk == 0)
    def _():
        c_ref[...] = jnp.broadcast_to(c0_ref[...], c_ref.shape)
        n_ref[...] = jnp.broadcast_to(n0_ref[...], n_ref.shape)
        m_ref[...] = jnp.broadcast_to(m0_ref[...], m_ref.shape)
        r_ref[...] = jnp.broadcast_to(r0_ref[...], r_ref.shape)
        xn0 = _rms_norm(xc_ref[...].reshape(rows_all, d_model), gnm_ref[...])
        za_ref[...] = _dot(xn0, w_ref[...])

    if n_chunks > 1:
        xnb_ref[...] = _bf(_rms_norm(xnext_ref[...].reshape(rows_all, d_model), gnm_ref[...]))
    cur = {}

    def seq_rows(s):
        return pl.ds(pl.multiple_of(s * L, L), L)

    def project_next(s, c0, c1):
        if cur["zn"] is not None:
            rows = seq_rows(s)
            cur["zn"][rows, c0:c1] = jnp.dot(xnb_ref[rows, :], w_ref[:, c0:c1], preferred_element_type=F32)

    def ticker(s, slab, first_round):
        half_w = 2 * HEAD_DIM
        pieces = {first_round: 0, first_round + 1: 1} if cur["zn"] is not None else {}
        rounds = [0]

        def tick():
            rounds[0] += 1
            half = pieces.pop(rounds[0], None)
            if half is not None:
                c0 = (2 * slab + half) * half_w
                project_next(s, c0, c0 + half_w)

        tick.pending = pieces
        return tick

    scale = HEAD_DIM ** -0.5
    row = lax.broadcasted_iota(jnp.int32, (L, L), 0)
    col = lax.broadcasted_iota(jnp.int32, (L, L), 1)
    causal = row >= col
    eye = row == col
    rel = jnp.where(causal, row - col, 0).astype(F32)
    tcol = lax.broadcasted_iota(jnp.int32, (L, 1), 0).astype(F32)
    cosv = cos_ref[...]
    sinv = sin_ref[...]
    bias = bias_ref[...]
    gate_col = 8 * H_AB * HEAD_DIM
    ones_bf = jnp.ones((L, LANES), BF16)
    ret_tabs = []
    for h in range(H_AB):
        lg = lg_ref[h][:, :1]
        ret_tabs.append((jnp.where(causal, jnp.exp(rel * lg), 0.0),
                         jnp.broadcast_to(jnp.exp((tcol + 1.0) * lg), (L, LANES)),
                         jnp.broadcast_to(jnp.exp((L - 1.0 - tcol) * lg), (L, LANES)),
                         jnp.exp(L * lg)))

    def gates(s):
        project_next(s, gate_col, gate_col + LANES)
        pre = cur["zc"][seq_rows(s), gate_col:gate_col + LANES] + bias
        lf = jnp.minimum(pre, 0.0) - jnp.log1p(jnp.exp(-jnp.abs(pre)))
        lf = pltpu.roll(lf, LANES - H_AB, 1)
        if L > SUBLANES:
            b = _split_dot(tri_ref[...], lf)
            yield
        else:
            b = _cumsum_rows(lf)
        a = pre - b
        a_rows = jnp.transpose(a) if L == LANES else None
        return a, b, a_rows

    def zcol(s, j, h):
        c0 = (j * H_AB + h) * HEAD_DIM
        return cur["zc"][seq_rows(s), c0:c0 + HEAD_DIM]

    def mlstm_head(s, h, a, b, a_rows):
        tick = ticker(s, 2 * h, 1 + 2 * (h % 2))
        hs = slice(h * HEAD_DIM, (h + 1) * HEAD_DIM)
        a_col = a[:, h:h + 1]
        b_col = b[:, h:h + 1]
        if a_rows is not None:
            a_row = a_rows[h:h + 1, :]
        else:
            a_row = jnp.sum(jnp.where(eye, a_col, 0.0), axis=0, keepdims=True)
        m0 = m_ref[s, h][:, :1]
        a_caus = jnp.where(causal, a_row, -jnp.inf)
        a_max = jnp.max(a_caus, axis=-1, keepdims=True)
        if L > SUBLANES:
            tick()
            yield
        mx = jnp.maximum(m0, a_max)
        m_t = b_col + mx
        d = jnp.exp(a_caus - mx)
        inter = jnp.broadcast_to(jnp.exp(m0 - mx), (L, LANES))
        q_f32 = zcol(s, 0, h)
        q = _bf(q_f32)
        k = zcol(s, 1, h) * scale
        v = _bf(zcol(s, 2, h))
        c_prev = c_ref[s, h]
        n_prev = n_ref[s, h]
        qk = _dot_nt(q, k)
        qc = _dot(q, c_prev)
        tick()
        yield
        sm_f32 = qk * d
        sm = _bf(sm_f32)
        m_new = m_t[L - 1:L, :]
        b_last = b_col[L - 1:L, :]
        w_col = jnp.exp(a_col + b_last - m_new)
        keep = jnp.exp(b_last + m0 - m_new)
        kw = k * w_col
        if L > SUBLANES:
            nd = jnp.dot(sm, jnp.concatenate([v, ones_bf], axis=1), preferred_element_type=F32)
            qn = _dot_nt(q, jnp.broadcast_to(n_prev, (HEAD_DIM, LANES)))
            kv = _dot_tn(kw, v)
            tick()
            yield
            num = nd[:, :LANES] + inter * qc
            den = nd[:, LANES:] + inter * qn
        else:
            sv = _dot(sm, v)
            kv = _dot_tn(kw, v)
            tick()
            yield
            num = sv + inter * qc
            den = (jnp.sum(sm_f32, axis=-1, keepdims=True)
                   + inter * jnp.sum(q_f32 * n_prev, axis=-1, keepdims=True))
        hv = num / jnp.maximum(jnp.abs(den), jnp.exp(-m_t))
        y = yield from _layer_norm(hv, gna_ref[:, hs], tick)
        assert not tick.pending, "not enough lock-step rounds for the projection pieces"
        mxa_ref[s, :, hs] = (_sigmoid(zcol(s, 3, h)) * y).astype(mxa_ref.dtype)
        return [(c_ref, h, keep * c_prev + kv),
                (n_ref, h, keep * n_prev + jnp.sum(kw, axis=0, keepdims=True)),
                (m_ref, h, jnp.broadcast_to(m_new, (1, LANES)))]

    def ret_head(s, h):
        tick = ticker(s, 2 * h + 1, 3 - 2 * (h % 2))
        hs = slice(h * HEAD_DIM, (h + 1) * HEAD_DIM)
        decay, inner, tail, g_pow = ret_tabs[h]
        q2 = zcol(s, 4, h)
        k2 = zcol(s, 5, h)
        v2 = _bf(zcol(s, 6, h))
        qr = _bf(q2 * cosv + pltpu.roll(q2, HEAD_DIM // 2, 1) * sinv)
        kr = (k2 * cosv + pltpu.roll(k2, HEAD_DIM // 2, 1) * sinv) * scale
        r_prev = r_ref[s, h]
        qk = _dot_nt(qr, kr)
        qs = _dot(qr, r_prev)
        kv = _dot_tn(kr * tail, v2)
        tick()
        yield
        o2 = _dot(qk * decay, v2)
        tick()
        yield
        o2 = o2 + qs * inner
        y2 = yield from _layer_norm(o2, gnb_ref[:, hs], tick)
        assert not tick.pending, "not enough lock-step rounds for the projection pieces"
        gv = zcol(s, 7, h)
        mxb_ref[s, :, hs] = (gv * _sigmoid(gv) * y2).astype(mxb_ref.dtype)
        return [(r_ref, h, g_pow * r_prev + kv)]

    def group(i, carry_):
        seqs = [i * unroll + j for j in range(unroll)]
        prep = _lock_step([gates(s) for s in seqs])
        chains = []
        for s, (a, b, a_rows) in zip(seqs, prep):
            for h in range(H_AB):
                chains.append((s, mlstm_head(s, h, a, b, a_rows)))
                chains.append((s, ret_head(s, h)))
        new = _lock_step([g for _, g in chains])
        for (s, _), new_state in zip(chains, new):
            for ref, h, val in new_state:
                ref[s, h] = val
        return carry_

    def step(zc_ref, zn_ref):
        cur["zc"], cur["zn"] = zc_ref, zn_ref
        lax.fori_loop(0, sb // unroll, lambda i, carry_: group(i, carry_), 0)

    if n_chunks > 1:
        @pl.when(chunk % 2 == 0)
        def _():
            step(za_ref, zb_ref)

        @pl.when(chunk % 2 == 1)
        def _():
            step(zb_ref, za_ref)
    else:
        step(za_ref, None)


def _ab_mixer(x, gnm, w, cos_t, sin_t, bias, lg, gna, gnb, c0, n0, m0, r0, *, chunk, sb, unroll):
    n_seq, seq_len, d = x.shape
    n_chunks = seq_len // chunk
    width = w.shape[1]
    hd = H_AB * HEAD_DIM
    bcast = c0.shape[0] == 1 and n_seq > 1
    sb0 = 1 if bcast else sb
    st = (lambda i: 0) if bcast else (lambda i: i)
    full2 = lambda a: pl.BlockSpec(a.shape, lambda i, c: (0, 0))
    mat_in = pl.BlockSpec((sb0, H_AB, HEAD_DIM, HEAD_DIM), lambda i, c: (st(i), 0, 0, 0))
    vec_in = pl.BlockSpec((sb0, H_AB, 1, LANES), lambda i, c: (st(i), 0, 0, 0))
    mat_out = pl.BlockSpec((sb, H_AB, HEAD_DIM, HEAD_DIM), lambda i, c: (i, 0, 0, 0))
    vec_out = pl.BlockSpec((sb, H_AB, 1, LANES), lambda i, c: (i, 0, 0, 0))
    mix_out = pl.BlockSpec((sb, chunk, hd), lambda i, c: (i, c, 0))
    tab = pl.BlockSpec((chunk, LANES), lambda i, c: (c, 0))
    in_specs = [pl.BlockSpec((sb, chunk, d), lambda i, c: (i, c, 0))]
    args = [x]
    if n_chunks > 1:
        in_specs.append(pl.BlockSpec((sb, chunk, d), lambda i, c: (i, jnp.minimum(c + 1, n_chunks - 1), 0)))
        args.append(x)
    w_spec = pl.BlockSpec(w.shape, lambda i, c: (0, 0), pipeline_mode=pl.Buffered(1))
    in_specs += [full2(gnm), w_spec, tab, tab, full2(bias),
                 pl.BlockSpec(lg.shape, lambda i, c: (0, 0, 0)), full2(gna), full2(gnb),
                 mat_in, vec_in, vec_in, mat_in]
    args += [gnm, w, cos_t, sin_t, bias, lg, gna, gnb, c0, n0, m0, r0]
    if chunk > SUBLANES:
        tri = jnp.asarray(np.tril(np.ones((chunk, chunk), np.float32)), BF16)
        in_specs.append(full2(tri))
        args.append(tri)
    scratch = [pltpu.VMEM((sb * chunk, width), F32)]
    if n_chunks > 1:
        scratch += [pltpu.VMEM((sb * chunk, width), F32), pltpu.VMEM((sb * chunk, d), BF16)]
    out_shape = (jax.ShapeDtypeStruct((n_seq, seq_len, hd), BF16),
                 jax.ShapeDtypeStruct((n_seq, seq_len, hd), BF16),
                 jax.ShapeDtypeStruct((n_seq, H_AB, HEAD_DIM, HEAD_DIM), F32),
                 jax.ShapeDtypeStruct((n_seq, H_AB, 1, LANES), F32),
                 jax.ShapeDtypeStruct((n_seq, H_AB, 1, LANES), F32),
                 jax.ShapeDtypeStruct((n_seq, H_AB, HEAD_DIM, HEAD_DIM), F32))
    return pl.pallas_call(
        functools.partial(_ab_kernel, sb=sb, L=chunk, unroll=unroll, n_chunks=n_chunks),
        grid=(n_seq // sb, n_chunks),
        in_specs=in_specs,
        out_specs=(mix_out, mix_out, mat_out, vec_out, vec_out, mat_out),
        out_shape=out_shape,
        scratch_shapes=scratch,
        compiler_params=_params(2),
        name="ab_mixer",
    )(*args)


def _hgrn_level_tables(L):
    t = np.arange(L)[:, None]
    u = np.arange(L)[None, :]
    lvl = np.full((L, L), -1, np.int32)
    size, j = 1, 0
    while size < L:
        upper = ((t // size) % 2) == 1
        lvl[((t // size) == (u // size) + 1) & upper] = j
        size *= 2
        j += 1
    return (u <= t).astype(np.float32), lvl


def _hgrn_small_level_factors(f, r):
    one = jnp.ones_like(f)
    prev = lambda x, d: pltpu.roll(x, d, 1)
    nxt = lambda x, d: pltpu.roll(x, SUBLANES - d, 1)
    f_n1 = nxt(f, 1)
    p1 = f * prev(f, 1)
    sel = lambda idx, *vals: functools.reduce(
        lambda acc, iv: jnp.where(idx == iv[0], iv[1], acc), list(enumerate(vals))[:-1], vals[-1])
    g0 = jnp.where((r & 1) == 1, f, one)
    g1 = sel(r & 3, f_n1, one, f, p1)
    g2 = sel(r, f_n1 * nxt(p1, 3), nxt(p1, 2), f_n1, one, f, p1, p1 * prev(f, 2), p1 * prev(p1, 2))
    return [g0, g1, g2]


def _hgrn_kernel(*refs, sb, L, layer, n_heads, n_chunks, group_heads):
    refs = list(refs)
    xc_ref = refs.pop(0)
    xnext_ref = refs.pop(0) if n_chunks > 1 else None
    gnm_ref, w_ref, lbl_ref, gn_ref, s0_ref = [refs.pop(0) for _ in range(5)]
    if L > SUBLANES:
        tri_ref, lvl_ref = refs.pop(0), refs.pop(0)
    mx_ref, s_ref, za_ref = refs.pop(0), refs.pop(0), refs.pop(0)
    if n_chunks > 1:
        zb_ref, xnb_ref = refs
    chunk = pl.program_id(1)
    rows_all = sb * L
    d_model = xc_ref.shape[-1]

    @pl.when(chunk == 0)
    def _():
        s_ref[...] = jnp.broadcast_to(s0_ref[...], s_ref.shape)
        xn0 = _rms_norm(xc_ref[...].reshape(rows_all, d_model), gnm_ref[...])
        za_ref[...] = _dot(xn0, w_ref[...])

    if n_chunks > 1:
        xnb_ref[...] = _bf(_rms_norm(xnext_ref[...].reshape(rows_all, d_model), gnm_ref[...]))

    logits = lbl_ref[...]
    ex = jnp.exp(logits - jnp.max(logits, axis=0, keepdims=True))
    p = ex / jnp.sum(ex, axis=0, keepdims=True)
    cum = p[0:1, :]
    for r in range(1, layer + 1):
        cum = cum + p[r:r + 1, :]
    lb_all = cum - p[0:1, :]

    row = lax.broadcasted_iota(jnp.int32, (L, LANES), 0)
    n_levels = L.bit_length() - 1
    width = n_heads * HEAD_DIM
    slab_w = 4 * width // (sb * n_heads)
    assert n_chunks == 1 or slab_w % (2 * LANES) == 0

    def unit(s, h, zc_ref, zn_ref, slot):
        rows = pl.ds(s * L, L) if isinstance(s, int) else pl.ds(pl.multiple_of(s * L, L), L)
        hs = slice(h * HEAD_DIM, (h + 1) * HEAD_DIM)
        zblock = lambda j: zc_ref[rows, j * width + h * HEAD_DIM:j * width + (h + 1) * HEAD_DIM]
        pieces = {2 * slot + 1: s * n_heads + h} if zn_ref is not None else {}
        rounds = [0]

        def tick():
            rounds[0] += 1
            slab = pieces.pop(rounds[0], None)
            if slab is not None:
                c0 = slab * slab_w
                zn_ref[:, c0:c0 + slab_w] = jnp.dot(xnb_ref[...], w_ref[:, c0:c0 + slab_w],
                                                    preferred_element_type=F32)

        lb = lb_all[:, hs]
        qv = zblock(0)
        fv = zblock(1)
        iv = zblock(2)
        e = jnp.exp(-jnp.abs(fv))
        rcp = 1.0 / (1.0 + e)
        pos = fv >= 0.0
        sig = jnp.where(pos, rcp, e * rcp)
        sig_neg = jnp.where(pos, e * rcp, rcp)
        f_gate = lb + (1.0 - lb) * sig
        log_f = jnp.log(f_gate)
        k = (1.0 - lb) * sig_neg
        s_prev = s_ref[s, h]
        diag = jnp.sum(qv * k, axis=-1, keepdims=True)

        if L > SUBLANES:
            b = _split_dot(tri_ref[...], log_f)
            tick()
            yield
            nt = L // SUBLANES
            tile3 = lambda x: x.reshape(nt, SUBLANES, x.shape[-1])
            q3, k3, b3 = tile3(qv), tile3(k), tile3(b)
            r3 = lax.broadcasted_iota(jnp.int32, (1, SUBLANES, LANES), 1)
            small = _hgrn_small_level_factors(tile3(f_gate), r3)
            acc = [jnp.zeros((SUBLANES, L), F32)] * nt
            for j in range(n_levels):
                size = 1 << j
                if size < SUBLANES:
                    xj = (jnp.where((r3 & size) != 0, q3, k3) * small[j]).reshape(L, LANES)
                    up = list(range(nt))
                    a_j = _dot_nt(xj, xj)
                else:
                    s8 = size // SUBLANES
                    up = [i for i in range(nt) if i & s8]
                    mids = {i: jnp.broadcast_to(b3[i, SUBLANES - 1:, :], (SUBLANES, LANES))
                            for i in range(s8 - 1, nt, 2 * s8)}
                    b_mid = jnp.stack([mids[(i // (2 * s8)) * 2 * s8 + s8 - 1] for i in range(nt)])
                    fac = jnp.exp(-jnp.abs(b3 - b_mid))
                    x3 = jnp.stack([q3[i] if i & s8 else k3[i] for i in range(nt)]) * fac
                    xj = x3.reshape(L, LANES)
                    a_j = _dot_nt(jnp.concatenate([x3[i] for i in up], axis=0), xj)
                tick()
                yield
                for n, i in enumerate(up):
                    keep = lvl_ref[i * SUBLANES:(i + 1) * SUBLANES, :] == j
                    acc[i] = jnp.where(keep, a_j[n * SUBLANES:(n + 1) * SUBLANES, :], acc[i])
            o = diag * iv + _dot(jnp.concatenate(acc, axis=0), iv)
            tick()
            yield
        else:
            b = _cumsum_rows(log_f)
            pair = []
            for j in range(1, L):
                dec = jnp.exp(jnp.minimum(b - pltpu.roll(b, j, 0), 0.0))
                pair.append(jnp.sum(jnp.where(row >= j, qv * pltpu.roll(k, j, 0) * dec, 0.0),
                                    axis=-1, keepdims=True))
            tick()
            yield
            o = diag * iv
            for j, a in enumerate(pair, start=1):
                o = o + a * pltpu.roll(iv, j, 0)

        o = o + _dot(qv * jnp.exp(b), s_prev)
        tick()
        yield
        b_last = b[L - 1:L, :]
        e_last = jnp.exp(b_last)
        e_hi = _bf(e_last).astype(F32)
        sub = lax.broadcasted_iota(jnp.int32, (SUBLANES, LANES), 0)
        e_rows = jnp.where(sub == 0, e_hi, jnp.where(sub == 1, e_last - e_hi, 0.0))
        e_col = _dot_tn(e_rows, jnp.ones((SUBLANES, LANES), BF16))
        s_new = e_col * s_prev + _dot_tn(k * jnp.exp(b_last - b), iv)
        ms = jnp.mean(o * o, axis=-1, keepdims=True)
        tick()
        yield
        assert not pieces, "not enough lock-step rounds for the projection pieces"
        y = o * lax.rsqrt(ms + EPS) * gn_ref[:, hs]
        gv = zblock(3)
        mx_ref[s, :, hs] = (y * (gv * _sigmoid(gv))).astype(mx_ref.dtype)
        return s_new

    def step(zc_ref, zn_ref):
        def per_seq(s, carry_):
            for h0 in range(0, n_heads, group_heads):
                heads = list(range(h0, h0 + group_heads))
                new = _lock_step([unit(s, h, zc_ref, zn_ref, slot) for slot, h in enumerate(heads)])
                for h, s_new in zip(heads, new):
                    s_ref[s, h] = s_new
            return carry_
        if zn_ref is None:
            lax.fori_loop(0, sb, per_seq, 0)
        else:
            for s in range(sb):
                per_seq(s, 0)

    if n_chunks > 1:
        @pl.when(chunk % 2 == 0)
        def _():
            step(za_ref, zb_ref)

        @pl.when(chunk % 2 == 1)
        def _():
            step(zb_ref, za_ref)
    else:
        step(za_ref, None)


def _hgrn_mixer(x, gnm, w, lb_logits, gain, s0, *, chunk, sb, layer, group_heads):
    n_seq, seq_len, d = x.shape
    n_heads = s0.shape[1]
    n_chunks = seq_len // chunk
    width = n_heads * HEAD_DIM
    bcast = s0.shape[0] == 1 and n_seq > 1
    sb0 = 1 if bcast else sb
    st = (lambda i: 0) if bcast else (lambda i: i)
    full2 = lambda a: pl.BlockSpec(a.shape, lambda i, c: (0, 0))
    in_specs = [pl.BlockSpec((sb, chunk, d), lambda i, c: (i, c, 0))]
    args = [x]
    if n_chunks > 1:
        in_specs.append(pl.BlockSpec((sb, chunk, d), lambda i, c: (i, jnp.minimum(c + 1, n_chunks - 1), 0)))
        args.append(x)
    w_spec = pl.BlockSpec(w.shape, lambda i, c: (0, 0), pipeline_mode=pl.Buffered(1))
    in_specs += [full2(gnm), w_spec, full2(lb_logits), full2(gain),
                 pl.BlockSpec((sb0, n_heads, HEAD_DIM, HEAD_DIM), lambda i, c: (st(i), 0, 0, 0))]
    args += [gnm, w, lb_logits, gain, s0]
    if chunk > SUBLANES:
        tri, lvl = _hgrn_level_tables(chunk)
        in_specs += [pl.BlockSpec(tri.shape, lambda i, c: (0, 0)), pl.BlockSpec(lvl.shape, lambda i, c: (0, 0))]
        args += [jnp.asarray(tri, BF16), jnp.asarray(lvl)]
    out_specs = (pl.BlockSpec((sb, chunk, width), lambda i, c: (i, c, 0)),
                 pl.BlockSpec((sb, n_heads, HEAD_DIM, HEAD_DIM), lambda i, c: (i, 0, 0, 0)))
    out_shape = (jax.ShapeDtypeStruct((n_seq, seq_len, width), BF16),
                 jax.ShapeDtypeStruct((n_seq, n_heads, HEAD_DIM, HEAD_DIM), F32))
    scratch = [pltpu.VMEM((sb * chunk, w.shape[1]), F32)]
    if n_chunks > 1:
        scratch += [pltpu.VMEM((sb * chunk, w.shape[1]), F32), pltpu.VMEM((sb * chunk, d), BF16)]
    return pl.pallas_call(
        functools.partial(_hgrn_kernel, sb=sb, L=chunk, layer=layer, n_heads=n_heads, n_chunks=n_chunks,
                          group_heads=group_heads),
        grid=(n_seq // sb, n_chunks),
        in_specs=in_specs,
        out_specs=out_specs,
        out_shape=out_shape,
        scratch_shapes=scratch,
        compiler_params=_params(2),
        name="hgrn_mixer",
    )(*args)


def _ffn_kernel(*refs, n_mixed, s_blk, L, d_ff, final_norm):
    x_ref = refs[0]
    mix_refs = refs[1:1 + n_mixed]
    wo_refs = refs[1 + n_mixed:1 + 2 * n_mixed]
    (gn_ref, win_ref, cw_ref, cb_ref, wout_ref, buf_ref) = refs[1 + 2 * n_mixed:7 + 2 * n_mixed]
    pos = 7 + 2 * n_mixed
    if final_norm:
        gfin_ref = refs[pos]
        pos += 1
    xo_ref, bufo_ref, carry_ref = refs[pos:pos + 3]
    tile = pl.program_id(1)
    tm = s_blk * L

    @pl.when(tile == 0)
    def _():
        carry_ref[...] = jnp.broadcast_to(buf_ref[...], carry_ref.shape)

    x1 = x_ref[...]
    for m_ref, w_ref in zip(mix_refs, wo_refs):
        x1 = x1 + jnp.dot(m_ref[...], w_ref[...], preferred_element_type=F32)
    xn = _rms_norm(x1, gn_ref[...])
    ug = _dot(xn, win_ref[...])
    u = ug[:, :d_ff]
    gate = ug[:, d_ff:]

    t = lax.broadcasted_iota(jnp.int32, (tm, 1), 0) & (L - 1)
    if s_blk == 1:
        p0 = carry_ref[0, 0:1, :]
        p1 = carry_ref[0, 1:2, :]
    else:
        cr = carry_ref[...]
        p0 = jnp.broadcast_to(cr[:, 0:1, :], (s_blk, L, d_ff)).reshape(tm, d_ff)
        p1 = jnp.broadcast_to(cr[:, 1:2, :], (s_blk, L, d_ff)).reshape(tm, d_ff)
    prev1 = jnp.where(t == 0, p1, pltpu.roll(u, 1, 0))
    prev2 = jnp.where(t == 0, p0, jnp.where(t == 1, p1, pltpu.roll(u, 2, 0)))
    cw = cw_ref[...]
    conv = cb_ref[...] + (prev2 * cw[0:1, :] + prev1 * cw[1:2, :] + u * cw[2:3, :])
    hid = conv * _sigmoid(conv) * gate
    x2 = x1 + _dot(hid, wout_ref[...])
    if final_norm:
        xo_ref[...] = _rms_norm(x2, gfin_ref[...])
    else:
        xo_ref[...] = x2

    if s_blk == 1:
        carry_ref[0] = u[tm - (CONV_W - 1):tm, :]
    else:
        carry_ref[...] = u.reshape(s_blk, L, d_ff)[:, L - (CONV_W - 1):L, :]
    bufo_ref[...] = carry_ref[...]


def _ffn(x2d, mixed, wo_parts, gn, w_in, cw, cb, w_out, buf, gfin, *, n_seq, seq_len, tm):
    rows, d = x2d.shape
    d_ff = w_out.shape[0]
    if tm <= seq_len:
        s_blk, L = 1, tm
        tiles = seq_len // tm
        grid = (n_seq, tiles)
        rowmap = lambda s, j: (s * tiles + j, 0)
    else:
        s_blk, L = tm // seq_len, seq_len
        grid = (n_seq // s_blk, 1)
        rowmap = lambda s, j: (s, 0)
    bcast = buf.shape[0] == 1 and n_seq > 1
    bufmap = (lambda s, j: (0, 0, 0)) if bcast else (lambda s, j: (s, 0, 0))
    buf_blk = 1 if bcast else s_blk
    const = lambda s, j: (0, 0)
    n_mixed = len(mixed)
    in_specs = [pl.BlockSpec((tm, d), rowmap)]
    in_specs += [pl.BlockSpec((tm, m.shape[1]), rowmap) for m in mixed]
    once = lambda a: pl.BlockSpec(a.shape, const, pipeline_mode=pl.Buffered(1))
    in_specs += [once(w) for w in wo_parts]
    in_specs += [pl.BlockSpec((1, d), const),
                 once(w_in),
                 pl.BlockSpec(cw.shape, const),
                 pl.BlockSpec((1, d_ff), const),
                 once(w_out),
                 pl.BlockSpec((buf_blk, CONV_W - 1, d_ff), bufmap)]
    args = [x2d, *mixed, *wo_parts, gn, w_in, cw, cb, w_out, buf]
    if gfin is not None:
        in_specs.append(pl.BlockSpec((1, d), const))
        args.append(gfin)
    out_specs = (pl.BlockSpec((tm, d), rowmap),
                 pl.BlockSpec((s_blk, CONV_W - 1, d_ff), lambda s, j: (s, 0, 0)))
    out_shape = (jax.ShapeDtypeStruct((rows, d), F32),
                 jax.ShapeDtypeStruct((n_seq, CONV_W - 1, d_ff), F32))
    return pl.pallas_call(
        functools.partial(_ffn_kernel, n_mixed=n_mixed, s_blk=s_blk, L=L, d_ff=d_ff,
                          final_norm=gfin is not None),
        grid=grid,
        in_specs=in_specs,
        out_specs=out_specs,
        out_shape=out_shape,
        scratch_shapes=[pltpu.VMEM((s_blk, CONV_W - 1, d_ff), F32)],
        compiler_params=_params(2),
        name="ffn",
    )(*args)


def _tile_rows(n_seq, seq_len, target):
    rows = n_seq * seq_len
    if seq_len >= target:
        return target
    return min(rows, target)


def _trunk(x, pos, st, w, *, chunk_ab, chunk_c, sb_ab, sb_c, unroll_ab, group_c):
    n_seq, seq_len, d = x.shape
    rows = n_seq * seq_len
    tm = _tile_rows(n_seq, seq_len, 512)
    x2d = x.reshape(rows, d)

    half = HEAD_DIM // 2
    inv = 1.0 / (ROPE_BASE ** jnp.linspace(0.0, 1.0, half, dtype=F32))
    ang = pos[:, None] * inv[None, :]
    cos_t = jnp.concatenate([jnp.cos(ang), jnp.cos(ang)], axis=-1)
    sin_t = jnp.concatenate([-jnp.sin(ang), jnp.sin(ang)], axis=-1)

    mxa, mxb, c_new, n_new, m_new, r_new = _ab_mixer(
        x, w["norm_mix"][0], w["w_in_ab"], cos_t, sin_t, w["gate_bias"], w["lg"], w["gn_a"], w["gn_b"],
        st["c"], st["n"], st["m"], st["r"], chunk=chunk_ab, sb=sb_ab, unroll=unroll_ab)
    hd = H_AB * HEAD_DIM
    x2d, buf0 = _ffn(x2d, [mxa.reshape(rows, hd), mxb.reshape(rows, hd)], w["w_out_ab"],
                     w["norm_ffn"][0], w["w_ffn_in"][0], w["conv_w"][0], w["conv_b"][0], w["w_ffn_out"][0],
                     st["conv"][0], None, n_seq=n_seq, seq_len=seq_len, tm=tm)

    mx, s_new = _hgrn_mixer(x2d.reshape(n_seq, seq_len, d), w["norm_mix"][1], w["w_in_c"], w["lb_logits"],
                            w["gn_c"], st["s"], chunk=chunk_c, sb=sb_c, layer=1, group_heads=group_c)
    y2d, buf1 = _ffn(x2d, [mx.reshape(rows, -1)], w["w_out_c"],
                     w["norm_ffn"][1], w["w_ffn_in"][1], w["conv_w"][1], w["conv_b"][1], w["w_ffn_out"][1],
                     st["conv"][1], w["norm_final"], n_seq=n_seq, seq_len=seq_len, tm=tm)
    new_st = {"c": c_new, "n": n_new, "m": m_new, "r": r_new, "s": s_new, "conv": (buf0, buf1)}
    return y2d.reshape(n_seq, seq_len, d), new_st


def _lane_rep(v):
    return jnp.broadcast_to(v[..., None, None], v.shape + (1, LANES))


def kernel(x_prompt, x_sample, state_mlstm_C, state_mlstm_n, state_mlstm_m, state_ret_S, state_hgrn_S,
           state_ffn_conv, meta_tokens, norm_mix, w_in_ab, b_igate, b_fgate, gn_mlstm, gn_ret, w_out_ab,
           lb_logits, w_in_c, gn_hgrn, w_out_c, norm_ffn, w_ffn_in, conv_w, conv_b, w_ffn_out, norm_final):
    bp, seq, d = x_prompt.shape
    bs, dec_seq, _ = x_sample.shape
    hd = H_AB * HEAD_DIM
    n_hc = state_hgrn_S.shape[2]
    d_ff = w_ffn_out.shape[1]
    assert w_in_ab.shape[0] == 1 and w_in_c.shape[0] == 1 and norm_mix.shape[0] == 2

    wab = w_in_ab[0].astype(BF16)
    g0 = 4 * hd
    w_ab = jnp.concatenate([wab[:, :g0], wab[:, g0 + 2 * H_AB:], wab[:, g0:g0 + 2 * H_AB],
                            jnp.zeros((d, LANES - 2 * H_AB), BF16)], axis=1)
    wo_ab = w_out_ab[0].astype(BF16)
    log_gamma = jnp.log1p(-jnp.exp2(-5.0 - jnp.arange(H_AB, dtype=F32)))
    gate_bias = jnp.concatenate([b_igate[0], b_fgate[0], jnp.zeros((LANES - 2 * H_AB,), F32)])[None, :]
    w = {
        "norm_mix": norm_mix[:, None, :],
        "norm_ffn": norm_ffn[:, None, :],
        "norm_final": norm_final[None, :],
        "w_in_ab": w_ab,
        "gate_bias": gate_bias,
        "lg": _lane_rep(log_gamma),
        "gn_a": gn_mlstm[0][None, :],
        "gn_b": gn_ret[0][None, :],
        "w_out_ab": [wo_ab[:hd], wo_ab[hd:]],
        "lb_logits": lb_logits,
        "w_in_c": w_in_c[0].astype(BF16),
        "gn_c": gn_hgrn[0][None, :],
        "w_out_c": [w_out_c[0].astype(BF16)],
        "w_ffn_in": w_ffn_in.astype(BF16),
        "conv_w": conv_w,
        "conv_b": conv_b[:, None, :],
        "w_ffn_out": w_ffn_out.astype(BF16),
    }

    def zero_state(n):
        return {"c": jnp.zeros((n, H_AB, HEAD_DIM, HEAD_DIM), F32),
                "n": jnp.zeros((n, H_AB, 1, LANES), F32),
                "m": jnp.zeros((n, H_AB, 1, LANES), F32),
                "r": jnp.zeros((n, H_AB, HEAD_DIM, HEAD_DIM), F32),
                "s": jnp.zeros((n, n_hc, HEAD_DIM, HEAD_DIM), F32),
                "conv": (jnp.zeros((n, CONV_W - 1, d_ff), F32), jnp.zeros((n, CONV_W - 1, d_ff), F32))}

    _, st_meta = _trunk(meta_tokens[None].astype(F32), jnp.arange(N_META, dtype=F32), zero_state(1), w,
                        chunk_ab=N_META, chunk_c=N_META, sb_ab=1, sb_c=1, unroll_ab=1, group_c=4)
    pos_p = N_META + jnp.arange(seq, dtype=F32)
    y_prompt, st_p = _trunk(x_prompt, pos_p, st_meta, w, chunk_ab=128, chunk_c=128,
                            sb_ab=min(bp, 2), sb_c=min(bp, 2), unroll_ab=min(bp, 2), group_c=4)
    st_s = {"c": state_mlstm_C[0], "n": state_mlstm_n[0][:, :, None, :], "m": _lane_rep(state_mlstm_m[0]),
            "r": state_ret_S[0], "s": state_hgrn_S[0], "conv": (state_ffn_conv[0], state_ffn_conv[1])}
    pos_s = PAST_LEN + jnp.arange(dec_seq, dtype=F32)
    y_sample, st_s = _trunk(x_sample, pos_s, st_s, w, chunk_ab=dec_seq, chunk_c=dec_seq,
                            sb_ab=min(bs, 16), sb_c=min(bs, 16), unroll_ab=min(bs, 2), group_c=n_hc)

    def outs(s):
        return (s["c"][None], s["n"][:, :, 0, :][None], s["m"][:, :, 0, 0][None], s["r"][None], s["s"][None],
                jnp.stack(s["conv"]))

    cp, n_p, mp, rp, sp, convp = outs(st_p)
    cs, n_s, ms, rs, ss, convs = outs(st_s)
    return (y_prompt, y_sample, cp, cs, n_p, n_s, mp, ms, rp, rs, sp, ss, convp, convs)
```

```python
import functools

import numpy as np

import jax
import jax.numpy as jnp
from jax import lax
from jax.experimental import pallas as pl
from jax.experimental.pallas import tpu as pltpu

EPS = 1e-6
N_META = 16
PAST_LEN = 16384
ROPE_BASE = 10000.0
HEAD_DIM = 128
H_AB = 4
CONV_W = 3
LANES = 128
SUBLANES = 8
VMEM_LIMIT = 56 * 1024 * 1024
F32 = jnp.float32
BF16 = jnp.bfloat16


def _bf(x):
    return x.astype(BF16)


def _dot(a, b):
    return jnp.dot(_bf(a), _bf(b), preferred_element_type=F32)


def _dot_nt(a, b):
    return lax.dot_general(_bf(a), _bf(b), (((1,), (1,)), ((), ())), preferred_element_type=F32)


def _dot_tn(a, b):
    return lax.dot_general(_bf(a), _bf(b), (((0,), (0,)), ((), ())), preferred_element_type=F32)


def _rms_norm(x, gain):
    y = x * lax.rsqrt(jnp.mean(x * x, axis=-1, keepdims=True) + EPS)
    return y * gain


def _sigmoid(x):
    return 1.0 / (1.0 + jnp.exp(-x))


def _layer_norm(x, gain, tick):
    mu = jnp.mean(x, axis=-1, keepdims=True)
    if x.shape[0] > SUBLANES:
        tick()
        yield
    cen = x - mu
    var = jnp.mean(cen * cen, axis=-1, keepdims=True)
    if x.shape[0] > SUBLANES:
        tick()
        yield
    return cen * lax.rsqrt(var + EPS) * gain


def _cumsum_rows(x):
    n = x.shape[0]
    row = lax.broadcasted_iota(jnp.int32, x.shape, 0)
    k = 1
    while k < n:
        x = x + jnp.where(row >= k, pltpu.roll(x, k, 0), 0.0)
        k *= 2
    return x


def _split_dot(mat_bf16, x):
    hi = _bf(x)
    lo = _bf(x - hi.astype(F32))
    n = x.shape[1]
    r = jnp.dot(mat_bf16, jnp.concatenate([hi, lo], axis=1), preferred_element_type=F32)
    return r[:, :n] + r[:, n:]


def _lock_step(gens):
    results = [None] * len(gens)
    live = list(enumerate(gens))
    while live:
        still = []
        for idx, g in live:
            try:
                next(g)
                still.append((idx, g))
            except StopIteration as stop:
                results[idx] = stop.value
        live = still
    return results


def _params(n_grid):
    return pltpu.CompilerParams(dimension_semantics=("arbitrary",) * n_grid,
                                vmem_limit_bytes=VMEM_LIMIT)


def _ab_kernel(*refs, sb, L, unroll, n_chunks):
    assert sb % unroll == 0
    refs = list(refs)
    xc_ref = refs.pop(0)
    xnext_ref = refs.pop(0) if n_chunks > 1 else None
    (gnm_ref, w_ref, cos_ref, sin_ref, bias_ref, lg_ref, gna_ref, gnb_ref,
     c0_ref, n0_ref, m0_ref, r0_ref) = [refs.pop(0) for _ in range(12)]
    if L > SUBLANES:
        tri_ref = refs.pop(0)
    mxa_ref, mxb_ref, c_ref, n_ref, m_ref, r_ref, za_ref = [refs.pop(0) for _ in range(7)]
    if n_chunks > 1:
        zb_ref, xnb_ref = refs
    chunk = pl.program_id(1)
    rows_all = sb * L
    d_model = xc_ref.shape[-1]

    @pl.when(chunk == 0)
    def _():
        c_ref[...] = jnp.broadcast_to(c0_ref[...], c_ref.shape)
        n_ref[...] = jnp.broadcast_to(n0_ref[...], n_ref.shape)
        m_ref[...] = jnp.broadcast_to(m0_ref[...], m_ref.shape)
        r_ref[...] = jnp.broadcast_to(r0_ref[...], r_ref.shape)
        xn0 = _rms_norm(xc_ref[...].reshape(rows_all, d_model), gnm_ref[...])
        za_ref[...] = _dot(xn0, w_ref[...])

    if n_chunks > 1:
        xnb_ref[...] = _bf(_rms_norm(xnext_ref[...].reshape(rows_all, d_model), gnm_ref[...]))
    cur = {}

    def seq_rows(s):
        return pl.ds(pl.multiple_of(s * L, L), L)

    def project_next(s, c0, c1):
        if cur["zn"] is not None:
            rows = seq_rows(s)
            cur["zn"][rows, c0:c1] = jnp.dot(xnb_ref[rows, :], w_ref[:, c0:c1], preferred_element_type=F32)

    def ticker(s, slab, first_round):
        half_w = 2 * HEAD_DIM
        pieces = {first_round: 0, first_round + 1: 1} if cur["zn"] is not None else {}
        rounds = [0]

        def tick():
            rounds[0] += 1
            half = pieces.pop(rounds[0], None)
            if half is not None:
                c0 = (2 * slab + half) * half_w
                project_next(s, c0, c0 + half_w)

        tick.pending = pieces
        return tick

    scale = HEAD_DIM ** -0.5
    row = lax.broadcasted_iota(jnp.int32, (L, L), 0)
    col = lax.broadcasted_iota(jnp.int32, (L, L), 1)
    causal = row >= col
    eye = row == col
    rel = jnp.where(causal, row - col, 0).astype(F32)
    tcol = lax.broadcasted_iota(jnp.int32, (L, 1), 0).astype(F32)
    cosv = cos_ref[...]
    sinv = sin_ref[...]
    bias = bias_ref[...]
    gate_col = 8 * H_AB * HEAD_DIM
    ones_bf = jnp.ones((L, LANES), BF16)
    ret_tabs = []
    for h in range(H_AB):
        lg = lg_ref[h][:, :1]
        ret_tabs.append((jnp.where(causal, jnp.exp(rel * lg), 0.0),
                         jnp.broadcast_to(jnp.exp((tcol + 1.0) * lg), (L, LANES)),
                         jnp.broadcast_to(jnp.exp((L - 1.0 - tcol) * lg), (L, LANES)),
                         jnp.exp(L * lg)))

    def gates(s):
        project_next(s, gate_col, gate_col + LANES)
        pre = cur["zc"][seq_rows(s), gate_col:gate_col + LANES] + bias
        lf = jnp.minimum(pre, 0.0) - jnp.log1p(jnp.exp(-jnp.abs(pre)))
        lf = pltpu.roll(lf, LANES - H_AB, 1)
        if L > SUBLANES:
            b = _split_dot(tri_ref[...], lf)
            yield
        else:
            b = _cumsum_rows(lf)
        a = pre - b
        a_rows = jnp.transpose(a) if L == LANES else None
        return a, b, a_rows

    def zcol(s, j, h):
        c0 = (j * H_AB + h) * HEAD_DIM
        return cur["zc"][seq_rows(s), c0:c0 + HEAD_DIM]

    def mlstm_head(s, h, a, b, a_rows):
        tick = ticker(s, 2 * h, 1 + 2 * (h % 2))
        hs = slice(h * HEAD_DIM, (h + 1) * HEAD_DIM)
        a_col = a[:, h:h + 1]
        b_col = b[:, h:h + 1]
        if a_rows is not None:
            a_row = a_rows[h:h + 1, :]
        else:
            a_row = jnp.sum(jnp.where(eye, a_col, 0.0), axis=0, keepdims=True)
        m0 = m_ref[s, h][:, :1]
        a_caus = jnp.where(causal, a_row, -jnp.inf)
        a_max = jnp.max(a_caus, axis=-1, keepdims=True)
        if L > SUBLANES:
            tick()
            yield
        mx = jnp.maximum(m0, a_max)
        m_t = b_col + mx
        d = jnp.exp(a_caus - mx)
        inter = jnp.broadcast_to(jnp.exp(m0 - mx), (L, LANES))
        q_f32 = zcol(s, 0, h)
        q = _bf(q_f32)
        k = zcol(s, 1, h) * scale
        v = _bf(zcol(s, 2, h))
        c_prev = c_ref[s, h]
        n_prev = n_ref[s, h]
        qk = _dot_nt(q, k)
        qc = _dot(q, c_prev)
        tick()
        yield
        sm_f32 = qk * d
        sm = _bf(sm_f32)
        m_new = m_t[L - 1:L, :]
        b_last = b_col[L - 1:L, :]
        w_col = jnp.exp(a_col + b_last - m_new)
        keep = jnp.exp(b_last + m0 - m_new)
        kw = k * w_col
        if L > SUBLANES:
            nd = jnp.dot(sm, jnp.concatenate([v, ones_bf], axis=1), preferred_element_type=F32)
            qn = _dot_nt(q, jnp.broadcast_to(n_prev, (HEAD_DIM, LANES)))
            kv = _dot_tn(kw, v)
            tick()
            yield
            num = nd[:, :LANES] + inter * qc
            den = nd[:, LANES:] + inter * qn
        else:
            sv = _dot(sm, v)
            kv = _dot_tn(kw, v)
            tick()
            yield
            num = sv + inter * qc
            den = (jnp.sum(sm_f32, axis=-1, keepdims=True)
                   + inter * jnp.sum(q_f32 * n_prev, axis=-1, keepdims=True))
        hv = num / jnp.maximum(jnp.abs(den), jnp.exp(-m_t))
        y = yield from _layer_norm(hv, gna_ref[:, hs], tick)
        assert not tick.pending, "not enough lock-step rounds for the projection pieces"
        mxa_ref[s, :, hs] = (_sigmoid(zcol(s, 3, h)) * y).astype(mxa_ref.dtype)
        return [(c_ref, h, keep * c_prev + kv),
                (n_ref, h, keep * n_prev + jnp.sum(kw, axis=0, keepdims=True)),
                (m_ref, h, jnp.broadcast_to(m_new, (1, LANES)))]

    def ret_head(s, h):
        tick = ticker(s, 2 * h + 1, 3 - 2 * (h % 2))
        hs = slice(h * HEAD_DIM, (h + 1) * HEAD_DIM)
        decay, inner, tail, g_pow = ret_tabs[h]
        q2 = zcol(s, 4, h)
        k2 = zcol(s, 5, h)
        v2 = _bf(zcol(s, 6, h))
        qr = _bf(q2 * cosv + pltpu.roll(q2, HEAD_DIM // 2, 1) * sinv)
        kr = (k2 * cosv + pltpu.roll(k2, HEAD_DIM // 2, 1) * sinv) * scale
        r_prev = r_ref[s, h]
        qk = _dot_nt(qr, kr)
        qs = _dot(qr, r_prev)
        kv = _dot_tn(kr * tail, v2)
        tick()
        yield
        o2 = _dot(qk * decay, v2)
        tick()
        yield
        o2 = o2 + qs * inner
        y2 = yield from _layer_norm(o2, gnb_ref[:, hs], tick)
        assert not tick.pending, "not enough lock-step rounds for the projection pieces"
        gv = zcol(s, 7, h)
        mxb_ref[s, :, hs] = (gv * _sigmoid(gv) * y2).astype(mxb_ref.dtype)
        return [(r_ref, h, g_pow * r_prev + kv)]

    def group(i, carry_):
        seqs = [i * unroll + j for j in range(unroll)]
        prep = _lock_step([gates(s) for s in seqs])
        chains = []
        for s, (a, b, a_rows) in zip(seqs, prep):
            for h in range(H_AB):
                chains.append((s, mlstm_head(s, h, a, b, a_rows)))
                chains.append((s, ret_head(s, h)))
        new = _lock_step([g for _, g in chains])
        for (s, _), new_state in zip(chains, new):
            for ref, h, val in new_state:
                ref[s, h] = val
        return carry_

    def step(zc_ref, zn_ref):
        cur["zc"], cur["zn"] = zc_ref, zn_ref
        lax.fori_loop(0, sb // unroll, lambda i, carry_: group(i, carry_), 0)

    if n_chunks > 1:
        @pl.when(chunk % 2 == 0)
        def _():
            step(za_ref, zb_ref)

        @pl.when(chunk % 2 == 1)
        def _():
            step(zb_ref, za_ref)
    else:
        step(za_ref, None)


def _ab_mixer(x, gnm, w, cos_t, sin_t, bias, lg, gna, gnb, c0, n0, m0, r0, *, chunk, sb, unroll):
    n_seq, seq_len, d = x.shape
    n_chunks = seq_len // chunk
    width = w.shape[1]
    hd = H_AB * HEAD_DIM
    bcast = c0.shape[0] == 1 and n_seq > 1
    sb0 = 1 if bcast else sb
    st = (lambda i: 0) if bcast else (lambda i: i)
    full2 = lambda a: pl.BlockSpec(a.shape, lambda i, c: (0, 0))
    mat_in = pl.BlockSpec((sb0, H_AB, HEAD_DIM, HEAD_DIM), lambda i, c: (st(i), 0, 0, 0))
    vec_in = pl.BlockSpec((sb0, H_AB, 1, LANES), lambda i, c: (st(i), 0, 0, 0))
    mat_out = pl.BlockSpec((sb, H_AB, HEAD_DIM, HEAD_DIM), lambda i, c: (i, 0, 0, 0))
    vec_out = pl.BlockSpec((sb, H_AB, 1, LANES), lambda i, c: (i, 0, 0, 0))
    mix_out = pl.BlockSpec((sb, chunk, hd), lambda i, c: (i, c, 0))
    tab = pl.BlockSpec((chunk, LANES), lambda i, c: (c, 0))
    in_specs = [pl.BlockSpec((sb, chunk, d), lambda i, c: (i, c, 0))]
    args = [x]
    if n_chunks > 1:
        in_specs.append(pl.BlockSpec((sb, chunk, d), lambda i, c: (i, jnp.minimum(c + 1, n_chunks - 1), 0)))
        args.append(x)
    w_spec = pl.BlockSpec(w.shape, lambda i, c: (0, 0), pipeline_mode=pl.Buffered(1))
    in_specs += [full2(gnm), w_spec, tab, tab, full2(bias),
                 pl.BlockSpec(lg.shape, lambda i, c: (0, 0, 0)), full2(gna), full2(gnb),
                 mat_in, vec_in, vec_in, mat_in]
    args += [gnm, w, cos_t, sin_t, bias, lg, gna, gnb, c0, n0, m0, r0]
    if chunk > SUBLANES:
        tri = jnp.asarray(np.tril(np.ones((chunk, chunk), np.float32)), BF16)
        in_specs.append(full2(tri))
        args.append(tri)
    scratch = [pltpu.VMEM((sb * chunk, width), F32)]
    if n_chunks > 1:
        scratch += [pltpu.VMEM((sb * chunk, width), F32), pltpu.VMEM((sb * chunk, d), BF16)]
    out_shape = (jax.ShapeDtypeStruct((n_seq, seq_len, hd), BF16),
                 jax.ShapeDtypeStruct((n_seq, seq_len, hd), BF16),
                 jax.ShapeDtypeStruct((n_seq, H_AB, HEAD_DIM, HEAD_DIM), F32),
                 jax.ShapeDtypeStruct((n_seq, H_AB, 1, LANES), F32),
                 jax.ShapeDtypeStruct((n_seq, H_AB, 1, LANES), F32),
                 jax.ShapeDtypeStruct((n_seq, H_AB, HEAD_DIM, HEAD_DIM), F32))
    return pl.pallas_call(
        functools.partial(_ab_kernel, sb=sb, L=chunk, unroll=unroll, n_chunks=n_chunks),
        grid=(n_seq // sb, n_chunks),
        in_specs=in_specs,
        out_specs=(mix_out, mix_out, mat_out, vec_out, vec_out, mat_out),
        out_shape=out_shape,
        scratch_shapes=scratch,
        compiler_params=_params(2),
        name="ab_mixer",
    )(*args)


def _hgrn_level_tables(L):
    t = np.arange(L)[:, None]
    u = np.arange(L)[None, :]
    lvl = np.full((L, L), -1, np.int32)
    size, j = 1, 0
    while size < L:
        upper = ((t // size) % 2) == 1
        lvl[((t // size) == (u // size) + 1) & upper] = j
        size *= 2
        j += 1
    return (u <= t).astype(np.float32), lvl


def _hgrn_small_level_factors(f, r):
    one = jnp.ones_like(f)
    prev = lambda x, d: pltpu.roll(x, d, 1)
    nxt = lambda x, d: pltpu.roll(x, SUBLANES - d, 1)
    f_n1 = nxt(f, 1)
    p1 = f * prev(f, 1)
    sel = lambda idx, *vals: functools.reduce(
        lambda acc, iv: jnp.where(idx == iv[0], iv[1], acc), list(enumerate(vals))[:-1], vals[-1])
    g0 = jnp.where((r & 1) == 1, f, one)
    g1 = sel(r & 3, f_n1, one, f, p1)
    g2 = sel(r, f_n1 * nxt(p1, 3), nxt(p1, 2), f_n1, one, f, p1, p1 * prev(f, 2), p1 * prev(p1, 2))
    return [g0, g1, g2]


def _hgrn_kernel(*refs, sb, L, layer, n_heads, n_chunks, group_heads):
    refs = list(refs)
    xc_ref = refs.pop(0)
    xnext_ref = refs.pop(0) if n_chunks > 1 else None
    gnm_ref, w_ref, lbl_ref, gn_ref, s0_ref = [refs.pop(0) for _ in range(5)]
    if L > SUBLANES:
        tri_ref, lvl_ref = refs.pop(0), refs.pop(0)
    mx_ref, s_ref, za_ref = refs.pop(0), refs.pop(0), refs.pop(0)
    if n_chunks > 1:
        zb_ref, xnb_ref = refs
    chunk = pl.program_id(1)
    rows_all = sb * L
    d_model = xc_ref.shape[-1]

    @pl.when(chunk == 0)
    def _():
        s_ref[...] = jnp.broadcast_to(s0_ref[...], s_ref.shape)
        xn0 = _rms_norm(xc_ref[...].reshape(rows_all, d_model), gnm_ref[...])
        za_ref[...] = _dot(xn0, w_ref[...])

    if n_chunks > 1:
        xnb_ref[...] = _bf(_rms_norm(xnext_ref[...].reshape(rows_all, d_model), gnm_ref[...]))

    logits = lbl_ref[...]
    ex = jnp.exp(logits - jnp.max(logits, axis=0, keepdims=True))
    p = ex / jnp.sum(ex, axis=0, keepdims=True)
    cum = p[0:1, :]
    for r in range(1, layer + 1):
        cum = cum + p[r:r + 1, :]
    lb_all = cum - p[0:1, :]

    row = lax.broadcasted_iota(jnp.int32, (L, LANES), 0)
    n_levels = L.bit_length() - 1
    width = n_heads * HEAD_DIM
    slab_w = 4 * width // (sb * n_heads)
    assert n_chunks == 1 or slab_w % (2 * LANES) == 0

    def unit(s, h, zc_ref, zn_ref, slot):
        rows = pl.ds(s * L, L) if isinstance(s, int) else pl.ds(pl.multiple_of(s * L, L), L)
        hs = slice(h * HEAD_DIM, (h + 1) * HEAD_DIM)
        zblock = lambda j: zc_ref[rows, j * width + h * HEAD_DIM:j * width + (h + 1) * HEAD_DIM]
        pieces = {2 * slot + 1: s * n_heads + h} if zn_ref is not None else {}
        rounds = [0]

        def tick():
            rounds[0] += 1
            slab = pieces.pop(rounds[0], None)
            if slab is not None:
                c0 = slab * slab_w
                zn_ref[:, c0:c0 + slab_w] = jnp.dot(xnb_ref[...], w_ref[:, c0:c0 + slab_w],
                                                    preferred_element_type=F32)

        lb = lb_all[:, hs]
        qv = zblock(0)
        fv = zblock(1)
        iv = zblock(2)
        e = jnp.exp(-jnp.abs(fv))
        rcp = 1.0 / (1.0 + e)
        pos = fv >= 0.0
        sig = jnp.where(pos, rcp, e * rcp)
        sig_neg = jnp.where(pos, e * rcp, rcp)
        f_gate = lb + (1.0 - lb) * sig
        log_f = jnp.log2(f_gate)
        k = (1.0 - lb) * sig_neg
        s_prev = s_ref[s, h]
        diag = jnp.sum(qv * k, axis=-1, keepdims=True)

        if L > SUBLANES:
            b = _split_dot(tri_ref[...], log_f)
            tick()
            yield
            nt = L // SUBLANES
            tile3 = lambda x: x.reshape(nt, SUBLANES, x.shape[-1])
            q3, k3, b3 = tile3(qv), tile3(k), tile3(b)
            r3 = lax.broadcasted_iota(jnp.int32, (1, SUBLANES, LANES), 1)
            small = _hgrn_small_level_factors(tile3(f_gate), r3)
            acc = [jnp.zeros((SUBLANES, L), F32)] * nt
            for j in range(n_levels):
                size = 1 << j
                if size < SUBLANES:
                    xj = (jnp.where((r3 & size) != 0, q3, k3) * small[j]).reshape(L, LANES)
                    up = list(range(nt))
                    a_j = _dot_nt(xj, xj)
                else:
                    s8 = size // SUBLANES
                    up = [i for i in range(nt) if i & s8]
                    mids = {i: jnp.broadcast_to(b3[i, SUBLANES - 1:, :], (SUBLANES, LANES))
                            for i in range(s8 - 1, nt, 2 * s8)}
                    b_mid = jnp.stack([mids[(i // (2 * s8)) * 2 * s8 + s8 - 1] for i in range(nt)])
                    fac = jnp.exp2(-jnp.abs(b3 - b_mid))
                    x3 = jnp.stack([q3[i] if i & s8 else k3[i] for i in range(nt)]) * fac
                    xj = x3.reshape(L, LANES)
                    a_j = _dot_nt(jnp.concatenate([x3[i] for i in up], axis=0), xj)
                tick()
                yield
                for n, i in enumerate(up):
                    keep = lvl_ref[i * SUBLANES:(i + 1) * SUBLANES, :] == j
                    acc[i] = jnp.where(keep, a_j[n * SUBLANES:(n + 1) * SUBLANES, :], acc[i])
            o = diag * iv + _dot(jnp.concatenate(acc, axis=0), iv)
            tick()
            yield
        else:
            b = _cumsum_rows(log_f)
            pair = []
            for j in range(1, L):
                dec = jnp.exp2(jnp.minimum(b - pltpu.roll(b, j, 0), 0.0))
                pair.append(jnp.sum(jnp.where(row >= j, qv * pltpu.roll(k, j, 0) * dec, 0.0),
                                    axis=-1, keepdims=True))
            tick()
            yield
            o = diag * iv
            for j, a in enumerate(pair, start=1):
                o = o + a * pltpu.roll(iv, j, 0)

        o = o + _dot(qv * jnp.exp2(b), s_prev)
        tick()
        yield
        b_last = b[L - 1:L, :]
        e_last = jnp.exp2(b_last)
        e_hi = _bf(e_last).astype(F32)
        sub = lax.broadcasted_iota(jnp.int32, (SUBLANES, LANES), 0)
        e_rows = jnp.where(sub == 0, e_hi, jnp.where(sub == 1, e_last - e_hi, 0.0))
        e_col = _dot_tn(e_rows, jnp.ones((SUBLANES, LANES), BF16))
        s_new = e_col * s_prev + _dot_tn(k * jnp.exp2(b_last - b), iv)
        ms = jnp.mean(o * o, axis=-1, keepdims=True)
        tick()
        yield
        assert not pieces, "not enough lock-step rounds for the projection pieces"
        y = o * lax.rsqrt(ms + EPS) * gn_ref[:, hs]
        gv = zblock(3)
        mx_ref[s, :, hs] = (y * (gv * _sigmoid(gv))).astype(mx_ref.dtype)
        return s_new

    def step(zc_ref, zn_ref):
        def per_seq(s, carry_):
            for h0 in range(0, n_heads, group_heads):
                heads = list(range(h0, h0 + group_heads))
                new = _lock_step([unit(s, h, zc_ref, zn_ref, slot) for slot, h in enumerate(heads)])
                for h, s_new in zip(heads, new):
                    s_ref[s, h] = s_new
            return carry_
        if zn_ref is None:
            lax.fori_loop(0, sb, per_seq, 0)
        else:
            for s in range(sb):
                per_seq(s, 0)

    if n_chunks > 1:
        @pl.when(chunk % 2 == 0)
        def _():
            step(za_ref, zb_ref)

        @pl.when(chunk % 2 == 1)
        def _():
            step(zb_ref, za_ref)
    else:
        step(za_ref, None)


def _hgrn_mixer(x, gnm, w, lb_logits, gain, s0, *, chunk, sb, layer, group_heads):
    n_seq, seq_len, d = x.shape
    n_heads = s0.shape[1]
    n_chunks = seq_len // chunk
    width = n_heads * HEAD_DIM
    bcast = s0.shape[0] == 1 and n_seq > 1
    sb0 = 1 if bcast else sb
    st = (lambda i: 0) if bcast else (lambda i: i)
    full2 = lambda a: pl.BlockSpec(a.shape, lambda i, c: (0, 0))
    in_specs = [pl.BlockSpec((sb, chunk, d), lambda i, c: (i, c, 0))]
    args = [x]
    if n_chunks > 1:
        in_specs.append(pl.BlockSpec((sb, chunk, d), lambda i, c: (i, jnp.minimum(c + 1, n_chunks - 1), 0)))
        args.append(x)
    w_spec = pl.BlockSpec(w.shape, lambda i, c: (0, 0), pipeline_mode=pl.Buffered(1))
    in_specs += [full2(gnm), w_spec, full2(lb_logits), full2(gain),
                 pl.BlockSpec((sb0, n_heads, HEAD_DIM, HEAD_DIM), lambda i, c: (st(i), 0, 0, 0))]
    args += [gnm, w, lb_logits, gain, s0]
    if chunk > SUBLANES:
        tri, lvl = _hgrn_level_tables(chunk)
        in_specs += [pl.BlockSpec(tri.shape, lambda i, c: (0, 0)), pl.BlockSpec(lvl.shape, lambda i, c: (0, 0))]
        args += [jnp.asarray(tri, BF16), jnp.asarray(lvl)]
    out_specs = (pl.BlockSpec((sb, chunk, width), lambda i, c: (i, c, 0)),
                 pl.BlockSpec((sb, n_heads, HEAD_DIM, HEAD_DIM), lambda i, c: (i, 0, 0, 0)))
    out_shape = (jax.ShapeDtypeStruct((n_seq, seq_len, width), BF16),
                 jax.ShapeDtypeStruct((n_seq, n_heads, HEAD_DIM, HEAD_DIM), F32))
    scratch = [pltpu.VMEM((sb * chunk, w.shape[1]), F32)]
    if n_chunks > 1:
        scratch += [pltpu.VMEM((sb * chunk, w.shape[1]), F32), pltpu.VMEM((sb * chunk, d), BF16)]
    return pl.pallas_call(
        functools.partial(_hgrn_kernel, sb=sb, L=chunk, layer=layer, n_heads=n_heads, n_chunks=n_chunks,
                          group_heads=group_heads),
        grid=(n_seq // sb, n_chunks),
        in_specs=in_specs,
        out_specs=out_specs,
        out_shape=out_shape,
        scratch_shapes=scratch,
        compiler_params=_params(2),
        name="hgrn_mixer",
    )(*args)


def _ffn_kernel(*refs, n_mixed, s_blk, L, d_ff, final_norm):
    x_ref = refs[0]
    mix_refs = refs[1:1 + n_mixed]
    wo_refs = refs[1 + n_mixed:1 + 2 * n_mixed]
    (gn_ref, win_ref, cw_ref, cb_ref, wout_ref, buf_ref) = refs[1 + 2 * n_mixed:7 + 2 * n_mixed]
    pos = 7 + 2 * n_mixed
    if final_norm:
        gfin_ref = refs[pos]
        pos += 1
    xo_ref, bufo_ref, carry_ref = refs[pos:pos + 3]
    tile = pl.program_id(1)
    tm = s_blk * L

    @pl.when(tile == 0)
    def _():
        carry_ref[...] = jnp.broadcast_to(buf_ref[...], carry_ref.shape)

    x1 = x_ref[...]
    for m_ref, w_ref in zip(mix_refs, wo_refs):
        x1 = x1 + jnp.dot(m_ref[...], w_ref[...], preferred_element_type=F32)
    xn = _rms_norm(x1, gn_ref[...])
    ug = _dot(xn, win_ref[...])
    u = ug[:, :d_ff]
    gate = ug[:, d_ff:]

    t = lax.broadcasted_iota(jnp.int32, (tm, 1), 0) & (L - 1)
    if s_blk == 1:
        p0 = carry_ref[0, 0:1, :]
        p1 = carry_ref[0, 1:2, :]
    else:
        cr = carry_ref[...]
        p0 = jnp.broadcast_to(cr[:, 0:1, :], (s_blk, L, d_ff)).reshape(tm, d_ff)
        p1 = jnp.broadcast_to(cr[:, 1:2, :], (s_blk, L, d_ff)).reshape(tm, d_ff)
    prev1 = jnp.where(t == 0, p1, pltpu.roll(u, 1, 0))
    prev2 = jnp.where(t == 0, p0, jnp.where(t == 1, p1, pltpu.roll(u, 2, 0)))
    cw = cw_ref[...]
    conv = cb_ref[...] + (prev2 * cw[0:1, :] + prev1 * cw[1:2, :] + u * cw[2:3, :])
    hid = conv * _sigmoid(conv) * gate
    x2 = x1 + _dot(hid, wout_ref[...])
    if final_norm:
        xo_ref[...] = _rms_norm(x2, gfin_ref[...])
    else:
        xo_ref[...] = x2

    if s_blk == 1:
        carry_ref[0] = u[tm - (CONV_W - 1):tm, :]
    else:
        carry_ref[...] = u.reshape(s_blk, L, d_ff)[:, L - (CONV_W - 1):L, :]
    bufo_ref[...] = carry_ref[...]


def _ffn(x2d, mixed, wo_parts, gn, w_in, cw, cb, w_out, buf, gfin, *, n_seq, seq_len, tm):
    rows, d = x2d.shape
    d_ff = w_out.shape[0]
    if tm <= seq_len:
        s_blk, L = 1, tm
        tiles = seq_len // tm
        grid = (n_seq, tiles)
        rowmap = lambda s, j: (s * tiles + j, 0)
    else:
        s_blk, L = tm // seq_len, seq_len
        grid = (n_seq // s_blk, 1)
        rowmap = lambda s, j: (s, 0)
    if isinstance(buf, tuple):
        buf, layer = buf
        buf_spec = pl.BlockSpec((None, s_blk, CONV_W - 1, d_ff), lambda s, j: (layer, s, 0, 0))
    else:
        bcast = buf.shape[0] == 1 and n_seq > 1
        bufmap = (lambda s, j: (0, 0, 0)) if bcast else (lambda s, j: (s, 0, 0))
        buf_spec = pl.BlockSpec((1 if bcast else s_blk, CONV_W - 1, d_ff), bufmap)
    const = lambda s, j: (0, 0)
    n_mixed = len(mixed)
    in_specs = [pl.BlockSpec((tm, d), rowmap)]
    in_specs += [pl.BlockSpec((tm, m.shape[1]), rowmap) for m in mixed]
    once = lambda a: pl.BlockSpec(a.shape, const, pipeline_mode=pl.Buffered(1))
    in_specs += [once(w) for w in wo_parts]
    in_specs += [pl.BlockSpec((1, d), const),
                 once(w_in),
                 pl.BlockSpec(cw.shape, const),
                 pl.BlockSpec((1, d_ff), const),
                 once(w_out),
                 buf_spec]
    args = [x2d, *mixed, *wo_parts, gn, w_in, cw, cb, w_out, buf]
    if gfin is not None:
        in_specs.append(pl.BlockSpec((1, d), const))
        args.append(gfin)
    out_specs = (pl.BlockSpec((tm, d), rowmap),
                 pl.BlockSpec((s_blk, CONV_W - 1, d_ff), lambda s, j: (s, 0, 0)))
    out_shape = (jax.ShapeDtypeStruct((rows, d), F32),
                 jax.ShapeDtypeStruct((n_seq, CONV_W - 1, d_ff), F32))
    return pl.pallas_call(
        functools.partial(_ffn_kernel, n_mixed=n_mixed, s_blk=s_blk, L=L, d_ff=d_ff,
                          final_norm=gfin is not None),
        grid=grid,
        in_specs=in_specs,
        out_specs=out_specs,
        out_shape=out_shape,
        scratch_shapes=[pltpu.VMEM((s_blk, CONV_W - 1, d_ff), F32)],
        compiler_params=_params(2),
        name="ffn",
    )(*args)


def _tile_rows(n_seq, seq_len, target):
    rows = n_seq * seq_len
    if seq_len >= target:
        return target
    return min(rows, target)


def _trunk(x, pos, st, w, *, chunk_ab, chunk_c, sb_ab, sb_c, unroll_ab, group_c):
    n_seq, seq_len, d = x.shape
    rows = n_seq * seq_len
    tm = _tile_rows(n_seq, seq_len, 512)
    x2d = x.reshape(rows, d)

    half = HEAD_DIM // 2
    inv = 1.0 / (ROPE_BASE ** jnp.linspace(0.0, 1.0, half, dtype=F32))
    ang = pos[:, None] * inv[None, :]
    cos_t = jnp.concatenate([jnp.cos(ang), jnp.cos(ang)], axis=-1)
    sin_t = jnp.concatenate([-jnp.sin(ang), jnp.sin(ang)], axis=-1)

    mxa, mxb, c_new, n_new, m_new, r_new = _ab_mixer(
        x, w["norm_mix"][0], w["w_in_ab"], cos_t, sin_t, w["gate_bias"], w["lg"], w["gn_a"], w["gn_b"],
        st["c"], st["n"], st["m"], st["r"], chunk=chunk_ab, sb=sb_ab, unroll=unroll_ab)
    hd = H_AB * HEAD_DIM
    x2d, buf0 = _ffn(x2d, [mxa.reshape(rows, hd), mxb.reshape(rows, hd)], w["w_out_ab"],
                     w["norm_ffn"][0], w["w_ffn_in"][0], w["conv_w"][0], w["conv_b"][0], w["w_ffn_out"][0],
                     st["conv"][0], None, n_seq=n_seq, seq_len=seq_len, tm=tm)

    mx, s_new = _hgrn_mixer(x2d.reshape(n_seq, seq_len, d), w["norm_mix"][1], w["w_in_c"], w["lb_logits"],
                            w["gn_c"], st["s"], chunk=chunk_c, sb=sb_c, layer=1, group_heads=group_c)
    y2d, buf1 = _ffn(x2d, [mx.reshape(rows, -1)], w["w_out_c"],
                     w["norm_ffn"][1], w["w_ffn_in"][1], w["conv_w"][1], w["conv_b"][1], w["w_ffn_out"][1],
                     st["conv"][1], w["norm_final"], n_seq=n_seq, seq_len=seq_len, tm=tm)
    new_st = {"c": c_new, "n": n_new, "m": m_new, "r": r_new, "s": s_new, "conv": (buf0, buf1)}
    return y2d.reshape(n_seq, seq_len, d), new_st


def _lane_rep(v):
    return jnp.broadcast_to(v[..., None, None], v.shape + (1, LANES))


def kernel(x_prompt, x_sample, state_mlstm_C, state_mlstm_n, state_mlstm_m, state_ret_S, state_hgrn_S,
           state_ffn_conv, meta_tokens, norm_mix, w_in_ab, b_igate, b_fgate, gn_mlstm, gn_ret, w_out_ab,
           lb_logits, w_in_c, gn_hgrn, w_out_c, norm_ffn, w_ffn_in, conv_w, conv_b, w_ffn_out, norm_final):
    bp, seq, d = x_prompt.shape
    bs, dec_seq, _ = x_sample.shape
    hd = H_AB * HEAD_DIM
    n_hc = state_hgrn_S.shape[2]
    d_ff = w_ffn_out.shape[1]
    assert w_in_ab.shape[0] == 1 and w_in_c.shape[0] == 1 and norm_mix.shape[0] == 2

    wab = w_in_ab[0].astype(BF16)
    g0 = 4 * hd
    w_ab = jnp.concatenate([wab[:, :g0], wab[:, g0 + 2 * H_AB:], wab[:, g0:g0 + 2 * H_AB],
                            jnp.zeros((d, LANES - 2 * H_AB), BF16)], axis=1)
    wo_ab = w_out_ab[0].astype(BF16)
    log_gamma = jnp.log1p(-jnp.exp2(-5.0 - jnp.arange(H_AB, dtype=F32)))
    gate_bias = jnp.concatenate([b_igate[0], b_fgate[0], jnp.zeros((LANES - 2 * H_AB,), F32)])[None, :]
    w = {
        "norm_mix": norm_mix[:, None, :],
        "norm_ffn": norm_ffn[:, None, :],
        "norm_final": norm_final[None, :],
        "w_in_ab": w_ab,
        "gate_bias": gate_bias,
        "lg": _lane_rep(log_gamma),
        "gn_a": gn_mlstm[0][None, :],
        "gn_b": gn_ret[0][None, :],
        "w_out_ab": [wo_ab[:hd], wo_ab[hd:]],
        "lb_logits": lb_logits,
        "w_in_c": w_in_c[0].astype(BF16),
        "gn_c": gn_hgrn[0][None, :],
        "w_out_c": [w_out_c[0].astype(BF16)],
        "w_ffn_in": w_ffn_in.astype(BF16),
        "conv_w": conv_w,
        "conv_b": conv_b[:, None, :],
        "w_ffn_out": w_ffn_out.astype(BF16),
    }

    def zero_state(n):
        return {"c": jnp.zeros((n, H_AB, HEAD_DIM, HEAD_DIM), F32),
                "n": jnp.zeros((n, H_AB, 1, LANES), F32),
                "m": jnp.zeros((n, H_AB, 1, LANES), F32),
                "r": jnp.zeros((n, H_AB, HEAD_DIM, HEAD_DIM), F32),
                "s": jnp.zeros((n, n_hc, HEAD_DIM, HEAD_DIM), F32),
                "conv": (jnp.zeros((n, CONV_W - 1, d_ff), F32), jnp.zeros((n, CONV_W - 1, d_ff), F32))}

    _, st_meta = _trunk(meta_tokens[None].astype(F32), jnp.arange(N_META, dtype=F32), zero_state(1), w,
                        chunk_ab=N_META, chunk_c=N_META, sb_ab=1, sb_c=1, unroll_ab=1, group_c=4)
    pos_p = N_META + jnp.arange(seq, dtype=F32)
    y_prompt, st_p = _trunk(x_prompt, pos_p, st_meta, w, chunk_ab=128, chunk_c=128,
                            sb_ab=min(bp, 2), sb_c=min(bp, 2), unroll_ab=min(bp, 2), group_c=4)
    st_s = {"c": state_mlstm_C[0], "n": state_mlstm_n[0][:, :, None, :], "m": _lane_rep(state_mlstm_m[0]),
            "r": state_ret_S[0], "s": state_hgrn_S[0], "conv": ((state_ffn_conv, 0), (state_ffn_conv, 1))}
    pos_s = PAST_LEN + jnp.arange(dec_seq, dtype=F32)
    y_sample, st_s = _trunk(x_sample, pos_s, st_s, w, chunk_ab=dec_seq, chunk_c=dec_seq,
                            sb_ab=min(bs, 16), sb_c=min(bs, 16), unroll_ab=min(bs, 2), group_c=n_hc)

    def outs(s):
        return (s["c"][None], s["n"][:, :, 0, :][None], s["m"][:, :, 0, 0][None], s["r"][None], s["s"][None],
                jnp.stack(s["conv"]))

    cp, n_p, mp, rp, sp, convp = outs(st_p)
    cs, n_s, ms, rs, ss, convs = outs(st_s)
    return (y_prompt, y_sample, cp, cs, n_p, n_s, mp, ms, rp, rs, sp, ss, convp, convs)
```

```python
import functools

import numpy as np

import jax
import jax.numpy as jnp
from jax import lax
from jax.experimental import pallas as pl
from jax.experimental.pallas import tpu as pltpu

EPS = 1e-6
N_META = 16
PAST_LEN = 16384
ROPE_BASE = 10000.0
HEAD_DIM = 128
H_AB = 4
CONV_W = 3
LANES = 128
SUBLANES = 8
VMEM_LIMIT = 56 * 1024 * 1024
F32 = jnp.float32
BF16 = jnp.bfloat16


def _bf(x):
    return x.astype(BF16)


def _dot(a, b):
    return jnp.dot(_bf(a), _bf(b), preferred_element_type=F32)


def _dot_nt(a, b):
    return lax.dot_general(_bf(a), _bf(b), (((1,), (1,)), ((), ())), preferred_element_type=F32)


def _dot_tn(a, b):
    return lax.dot_general(_bf(a), _bf(b), (((0,), (0,)), ((), ())), preferred_element_type=F32)


def _rms_norm(x, gain):
    y = x * lax.rsqrt(jnp.mean(x * x, axis=-1, keepdims=True) + EPS)
    return y * gain


def _sigmoid(x):
    return 1.0 / (1.0 + jnp.exp(-x))


def _layer_norm(x, gain, tick):
    mu = jnp.mean(x, axis=-1, keepdims=True)
    if x.shape[0] > SUBLANES:
        tick()
        yield
    cen = x - mu
    var = jnp.mean(cen * cen, axis=-1, keepdims=True)
    if x.shape[0] > SUBLANES:
        tick()
        yield
    return cen * lax.rsqrt(var + EPS) * gain


def _cumsum_rows(x):
    n = x.shape[0]
    row = lax.broadcasted_iota(jnp.int32, x.shape, 0)
    k = 1
    while k < n:
        x = x + jnp.where(row >= k, pltpu.roll(x, k, 0), 0.0)
        k *= 2
    return x


def _split_dot(mat_bf16, x):
    hi = _bf(x)
    lo = _bf(x - hi.astype(F32))
    n = x.shape[1]
    r = jnp.dot(mat_bf16, jnp.concatenate([hi, lo], axis=1), preferred_element_type=F32)
    return r[:, :n] + r[:, n:]


def _lock_step(gens):
    results = [None] * len(gens)
    live = list(enumerate(gens))
    while live:
        still = []
        for idx, g in live:
            try:
                next(g)
                still.append((idx, g))
            except StopIteration as stop:
                results[idx] = stop.value
        live = still
    return results


def _params(n_grid):
    return pltpu.CompilerParams(dimension_semantics=("arbitrary",) * n_grid,
                                vmem_limit_bytes=VMEM_LIMIT)


def _ab_kernel(*refs, sb, L, unroll, n_chunks):
    assert sb % unroll == 0
    refs = list(refs)
    xc_ref = refs.pop(0)
    xnext_ref = refs.pop(0) if n_chunks > 1 else None
    (gnm_ref, w_ref, cos_ref, sin_ref, bias_ref, lg_ref, gna_ref, gnb_ref,
     c0_ref, n0_ref, m0_ref, r0_ref) = [refs.pop(0) for _ in range(12)]
    if L > SUBLANES:
        tri_ref = refs.pop(0)
    mxa_ref, mxb_ref, c_ref, n_ref, m_ref, r_ref, za_ref = [refs.pop(0) for _ in range(7)]
    if n_chunks > 1:
        zb_ref, xnb_ref = refs
    chunk = pl.program_id(1)
    rows_all = sb * L
    d_model = xc_ref.shape[-1]

    @pl.when(chunk == 0)
    def _():
        c_ref[...] = jnp.broadcast_to(c0_ref[...], c_ref.shape)
        n_ref[...] = jnp.broadcast_to(n0_ref[...], n_ref.shape)
        m_ref[...] = jnp.broadcast_to(m0_ref[...], m_ref.shape)
        r_ref[...] = jnp.broadcast_to(r0_ref[...], r_ref.shape)
        xn0 = _rms_norm(xc_ref[...].reshape(rows_all, d_model), gnm_ref[...])
        za_ref[...] = _dot(xn0, w_ref[...])

    if n_chunks > 1:
        xnb_ref[...] = _bf(_rms_norm(xnext_ref[...].reshape(rows_all, d_model), gnm_ref[...]))
    cur = {}

    def seq_rows(s):
        return pl.ds(pl.multiple_of(s * L, L), L)

    def project_next(s, c0, c1):
        if cur["zn"] is not None:
            rows = seq_rows(s)
            cur["zn"][rows, c0:c1] = jnp.dot(xnb_ref[rows, :], w_ref[:, c0:c1], preferred_element_type=F32)

    def ticker(s, slab, first_round):
        half_w = 2 * HEAD_DIM
        pieces = {first_round: 0, first_round + 1: 1} if cur["zn"] is not None else {}
        rounds = [0]

        def tick():
            rounds[0] += 1
            half = pieces.pop(rounds[0], None)
            if half is not None:
                c0 = (2 * slab + half) * half_w
                project_next(s, c0, c0 + half_w)

        tick.pending = pieces
        return tick

    scale = HEAD_DIM ** -0.5
    row = lax.broadcasted_iota(jnp.int32, (L, L), 0)
    col = lax.broadcasted_iota(jnp.int32, (L, L), 1)
    causal = row >= col
    eye = row == col
    rel = jnp.where(causal, row - col, 0).astype(F32)
    tcol = lax.broadcasted_iota(jnp.int32, (L, 1), 0).astype(F32)
    cosv = cos_ref[...]
    sinv = sin_ref[...]
    bias = bias_ref[...]
    gate_col = 8 * H_AB * HEAD_DIM
    ones_bf = jnp.ones((L, LANES), BF16)
    ret_tabs = []
    for h in range(H_AB):
        lg = lg_ref[h][:, :1]
        ret_tabs.append((jnp.where(causal, jnp.exp(rel * lg), 0.0),
                         jnp.broadcast_to(jnp.exp((tcol + 1.0) * lg), (L, LANES)),
                         jnp.broadcast_to(jnp.exp((L - 1.0 - tcol) * lg), (L, LANES)),
                         jnp.exp(L * lg)))

    def gates(s):
        project_next(s, gate_col, gate_col + LANES)
        pre = cur["zc"][seq_rows(s), gate_col:gate_col + LANES] + bias
        lf = jnp.minimum(pre, 0.0) - jnp.log1p(jnp.exp(-jnp.abs(pre)))
        lf = pltpu.roll(lf, LANES - H_AB, 1)
        if L > SUBLANES:
            b = _split_dot(tri_ref[...], lf)
            yield
        else:
            b = _cumsum_rows(lf)
        a = pre - b
        a_rows = jnp.transpose(a) if L == LANES else None
        return a, b, a_rows

    def zcol(s, j, h):
        c0 = (j * H_AB + h) * HEAD_DIM
        return cur["zc"][seq_rows(s), c0:c0 + HEAD_DIM]

    def mlstm_head(s, h, a, b, a_rows):
        tick = ticker(s, 2 * h, 1 + 2 * (h % 2))
        hs = slice(h * HEAD_DIM, (h + 1) * HEAD_DIM)
        a_col = a[:, h:h + 1]
        b_col = b[:, h:h + 1]
        if a_rows is not None:
            a_row = a_rows[h:h + 1, :]
        else:
            a_row = jnp.sum(jnp.where(eye, a_col, 0.0), axis=0, keepdims=True)
        m0 = m_ref[s, h][:, :1]
        a_caus = jnp.where(causal, a_row, -jnp.inf)
        a_max = jnp.max(a_caus, axis=-1, keepdims=True)
        if L > SUBLANES:
            tick()
            yield
        mx = jnp.maximum(m0, a_max)
        m_t = b_col + mx
        d = jnp.exp(a_caus - mx)
        inter = jnp.broadcast_to(jnp.exp(m0 - mx), (L, LANES))
        q_f32 = zcol(s, 0, h)
        q = _bf(q_f32)
        k = zcol(s, 1, h) * scale
        v = _bf(zcol(s, 2, h))
        c_prev = c_ref[s, h]
        n_prev = n_ref[s, h]
        qk = _dot_nt(q, k)
        qc = _dot(q, c_prev)
        tick()
        yield
        sm_f32 = qk * d
        sm = _bf(sm_f32)
        m_new = m_t[L - 1:L, :]
        b_last = b_col[L - 1:L, :]
        w_col = jnp.exp(a_col + b_last - m_new)
        keep = jnp.exp(b_last + m0 - m_new)
        kw = k * w_col
        if L > SUBLANES:
            nd = jnp.dot(sm, jnp.concatenate([v, ones_bf], axis=1), preferred_element_type=F32)
            qn = _dot_nt(q, jnp.broadcast_to(n_prev, (HEAD_DIM, LANES)))
            kv = _dot_tn(kw, v)
            tick()
            yield
            num = nd[:, :LANES] + inter * qc
            den = nd[:, LANES:] + inter * qn
        else:
            sv = _dot(sm, v)
            kv = _dot_tn(kw, v)
            tick()
            yield
            num = sv + inter * qc
            den = (jnp.sum(sm_f32, axis=-1, keepdims=True)
                   + inter * jnp.sum(q_f32 * n_prev, axis=-1, keepdims=True))
        hv = num / jnp.maximum(jnp.abs(den), jnp.exp(-m_t))
        y = yield from _layer_norm(hv, gna_ref[:, hs], tick)
        assert not tick.pending, "not enough lock-step rounds for the projection pieces"
        mxa_ref[s, :, hs] = (_sigmoid(zcol(s, 3, h)) * y).astype(mxa_ref.dtype)
        return [(c_ref, h, keep * c_prev + kv),
                (n_ref, h, keep * n_prev + jnp.sum(kw, axis=0, keepdims=True)),
                (m_ref, h, jnp.broadcast_to(m_new, (1, LANES)))]

    def ret_head(s, h):
        tick = ticker(s, 2 * h + 1, 3 - 2 * (h % 2))
        hs = slice(h * HEAD_DIM, (h + 1) * HEAD_DIM)
        decay, inner, tail, g_pow = ret_tabs[h]
        q2 = zcol(s, 4, h)
        k2 = zcol(s, 5, h)
        v2 = _bf(zcol(s, 6, h))
        qr = _bf(q2 * cosv + pltpu.roll(q2, HEAD_DIM // 2, 1) * sinv)
        kr = (k2 * cosv + pltpu.roll(k2, HEAD_DIM // 2, 1) * sinv) * scale
        r_prev = r_ref[s, h]
        qk = _dot_nt(qr, kr)
        qs = _dot(qr, r_prev)
        kv = _dot_tn(kr * tail, v2)
        tick()
        yield
        o2 = _dot(qk * decay, v2)
        tick()
        yield
        o2 = o2 + qs * inner
        y2 = yield from _layer_norm(o2, gnb_ref[:, hs], tick)
        assert not tick.pending, "not enough lock-step rounds for the projection pieces"
        gv = zcol(s, 7, h)
        mxb_ref[s, :, hs] = (gv * _sigmoid(gv) * y2).astype(mxb_ref.dtype)
        return [(r_ref, h, g_pow * r_prev + kv)]

    def group(i, carry_):
        seqs = [i * unroll + j for j in range(unroll)]
        prep = _lock_step([gates(s) for s in seqs])
        chains = []
        for s, (a, b, a_rows) in zip(seqs, prep):
            for h in range(H_AB):
                chains.append((s, mlstm_head(s, h, a, b, a_rows)))
                chains.append((s, ret_head(s, h)))
        new = _lock_step([g for _, g in chains])
        for (s, _), new_state in zip(chains, new):
            for ref, h, val in new_state:
                ref[s, h] = val
        return carry_

    def step(zc_ref, zn_ref):
        cur["zc"], cur["zn"] = zc_ref, zn_ref
        lax.fori_loop(0, sb // unroll, lambda i, carry_: group(i, carry_), 0)

    if n_chunks > 1:
        @pl.when(chunk % 2 == 0)
        def _():
            step(za_ref, zb_ref)

        @pl.when(chunk % 2 == 1)
        def _():
            step(zb_ref, za_ref)
    else:
        step(za_ref, None)


def _ab_mixer(x, gnm, w, cos_t, sin_t, bias, lg, gna, gnb, c0, n0, m0, r0, *, chunk, sb, unroll):
    n_seq, seq_len, d = x.shape
    n_chunks = seq_len // chunk
    width = w.shape[1]
    hd = H_AB * HEAD_DIM
    bcast = c0.shape[0] == 1 and n_seq > 1
    sb0 = 1 if bcast else sb
    st = (lambda i: 0) if bcast else (lambda i: i)
    full2 = lambda a: pl.BlockSpec(a.shape, lambda i, c: (0, 0))
    mat_in = pl.BlockSpec((sb0, H_AB, HEAD_DIM, HEAD_DIM), lambda i, c: (st(i), 0, 0, 0))
    vec_in = pl.BlockSpec((sb0, H_AB, 1, LANES), lambda i, c: (st(i), 0, 0, 0))
    mat_out = pl.BlockSpec((sb, H_AB, HEAD_DIM, HEAD_DIM), lambda i, c: (i, 0, 0, 0))
    vec_out = pl.BlockSpec((sb, H_AB, 1, LANES), lambda i, c: (i, 0, 0, 0))
    mix_out = pl.BlockSpec((sb, chunk, hd), lambda i, c: (i, c, 0))
    tab = pl.BlockSpec((chunk, LANES), lambda i, c: (c, 0))
    in_specs = [pl.BlockSpec((sb, chunk, d), lambda i, c: (i, c, 0))]
    args = [x]
    if n_chunks > 1:
        in_specs.append(pl.BlockSpec((sb, chunk, d), lambda i, c: (i, jnp.minimum(c + 1, n_chunks - 1), 0)))
        args.append(x)
    w_spec = pl.BlockSpec(w.shape, lambda i, c: (0, 0), pipeline_mode=pl.Buffered(1))
    in_specs += [full2(gnm), w_spec, tab, tab, full2(bias),
                 pl.BlockSpec(lg.shape, lambda i, c: (0, 0, 0)), full2(gna), full2(gnb),
                 mat_in, vec_in, vec_in, mat_in]
    args += [gnm, w, cos_t, sin_t, bias, lg, gna, gnb, c0, n0, m0, r0]
    if chunk > SUBLANES:
        tri = jnp.asarray(np.tril(np.ones((chunk, chunk), np.float32)), BF16)
        in_specs.append(full2(tri))
        args.append(tri)
    scratch = [pltpu.VMEM((sb * chunk, width), F32)]
    if n_chunks > 1:
        scratch += [pltpu.VMEM((sb * chunk, width), F32), pltpu.VMEM((sb * chunk, d), BF16)]
    out_shape = (jax.ShapeDtypeStruct((n_seq, seq_len, hd), BF16),
                 jax.ShapeDtypeStruct((n_seq, seq_len, hd), BF16),
                 jax.ShapeDtypeStruct((n_seq, H_AB, HEAD_DIM, HEAD_DIM), F32),
                 jax.ShapeDtypeStruct((n_seq, H_AB, 1, LANES), F32),
                 jax.ShapeDtypeStruct((n_seq, H_AB, 1, LANES), F32),
                 jax.ShapeDtypeStruct((n_seq, H_AB, HEAD_DIM, HEAD_DIM), F32))
    return pl.pallas_call(
        functools.partial(_ab_kernel, sb=sb, L=chunk, unroll=unroll, n_chunks=n_chunks),
        grid=(n_seq // sb, n_chunks),
        in_specs=in_specs,
        out_specs=(mix_out, mix_out, mat_out, vec_out, vec_out, mat_out),
        out_shape=out_shape,
        scratch_shapes=scratch,
        compiler_params=_params(2),
        name="ab_mixer",
    )(*args)


def _hgrn_level_tables(L):
    t = np.arange(L)[:, None]
    u = np.arange(L)[None, :]
    lvl = np.full((L, L), -1, np.int32)
    size, j = 1, 0
    while size < L:
        upper = ((t // size) % 2) == 1
        lvl[((t // size) == (u // size) + 1) & upper] = j
        size *= 2
        j += 1
    return (u <= t).astype(np.float32), lvl


def _hgrn_small_level_factors(f, r):
    one = jnp.ones_like(f)
    prev = lambda x, d: pltpu.roll(x, d, 1)
    nxt = lambda x, d: pltpu.roll(x, SUBLANES - d, 1)
    f_n1 = nxt(f, 1)
    p1 = f * prev(f, 1)
    sel = lambda idx, *vals: functools.reduce(
        lambda acc, iv: jnp.where(idx == iv[0], iv[1], acc), list(enumerate(vals))[:-1], vals[-1])
    g0 = jnp.where((r & 1) == 1, f, one)
    g1 = sel(r & 3, f_n1, one, f, p1)
    g2 = sel(r, f_n1 * nxt(p1, 3), nxt(p1, 2), f_n1, one, f, p1, p1 * prev(f, 2), p1 * prev(p1, 2))
    return [g0, g1, g2]


def _hgrn_kernel(*refs, sb, L, layer, n_heads, n_chunks, group_heads):
    refs = list(refs)
    xc_ref = refs.pop(0)
    xnext_ref = refs.pop(0) if n_chunks > 1 else None
    gnm_ref, w_ref, lbl_ref, gn_ref, s0_ref = [refs.pop(0) for _ in range(5)]
    if L > SUBLANES:
        tri_ref, lvl_ref = refs.pop(0), refs.pop(0)
    mx_ref, s_ref, za_ref = refs.pop(0), refs.pop(0), refs.pop(0)
    if n_chunks > 1:
        zb_ref, xnb_ref = refs
    chunk = pl.program_id(1)
    rows_all = sb * L
    d_model = xc_ref.shape[-1]

    @pl.when(chunk == 0)
    def _():
        s_ref[...] = jnp.broadcast_to(s0_ref[...], s_ref.shape)
        xn0 = _rms_norm(xc_ref[...].reshape(rows_all, d_model), gnm_ref[...])
        za_ref[...] = _dot(xn0, w_ref[...])

    if n_chunks > 1:
        xnb_ref[...] = _bf(_rms_norm(xnext_ref[...].reshape(rows_all, d_model), gnm_ref[...]))

    logits = lbl_ref[...]
    ex = jnp.exp(logits - jnp.max(logits, axis=0, keepdims=True))
    p = ex / jnp.sum(ex, axis=0, keepdims=True)
    cum = p[0:1, :]
    for r in range(1, layer + 1):
        cum = cum + p[r:r + 1, :]
    lb_all = cum - p[0:1, :]

    row = lax.broadcasted_iota(jnp.int32, (L, LANES), 0)
    n_levels = L.bit_length() - 1
    width = n_heads * HEAD_DIM
    slab_w = 4 * width // (sb * n_heads)
    assert n_chunks == 1 or slab_w % (2 * LANES) == 0

    def unit(s, h, zc_ref, zn_ref, slot):
        rows = pl.ds(s * L, L) if isinstance(s, int) else pl.ds(pl.multiple_of(s * L, L), L)
        hs = slice(h * HEAD_DIM, (h + 1) * HEAD_DIM)
        zblock = lambda j: zc_ref[rows, j * width + h * HEAD_DIM:j * width + (h + 1) * HEAD_DIM]
        pieces = {slot + 1: s * n_heads + h} if zn_ref is not None else {}
        rounds = [0]

        def tick():
            rounds[0] += 1
            slab = pieces.pop(rounds[0], None)
            if slab is not None:
                c0 = slab * slab_w
                zn_ref[:, c0:c0 + slab_w] = jnp.dot(xnb_ref[...], w_ref[:, c0:c0 + slab_w],
                                                    preferred_element_type=F32)

        lb = lb_all[:, hs]
        qv = zblock(0)
        fv = zblock(1)
        iv = zblock(2)
        e = jnp.exp(-jnp.abs(fv))
        rcp = 1.0 / (1.0 + e)
        pos = fv >= 0.0
        sig = jnp.where(pos, rcp, e * rcp)
        sig_neg = jnp.where(pos, e * rcp, rcp)
        f_gate = lb + (1.0 - lb) * sig
        log_f = jnp.log2(f_gate)
        k = (1.0 - lb) * sig_neg
        s_prev = s_ref[s, h]
        diag = jnp.sum(qv * k, axis=-1, keepdims=True)

        if L > SUBLANES:
            b = _split_dot(tri_ref[...], log_f)
            tick()
            yield
            nt = L // SUBLANES
            tile3 = lambda x: x.reshape(nt, SUBLANES, x.shape[-1])
            q3, k3, b3 = tile3(qv), tile3(k), tile3(b)
            r3 = lax.broadcasted_iota(jnp.int32, (1, SUBLANES, LANES), 1)
            small = _hgrn_small_level_factors(tile3(f_gate), r3)
            acc = [jnp.zeros((SUBLANES, L), F32)] * nt
            for j in range(n_levels):
                size = 1 << j
                if size < SUBLANES:
                    xj = (jnp.where((r3 & size) != 0, q3, k3) * small[j]).reshape(L, LANES)
                    up = list(range(nt))
                    a_j = _dot_nt(xj, xj)
                else:
                    s8 = size // SUBLANES
                    up = [i for i in range(nt) if i & s8]
                    mids = {i: jnp.broadcast_to(b3[i, SUBLANES - 1:, :], (SUBLANES, LANES))
                            for i in range(s8 - 1, nt, 2 * s8)}
                    b_mid = jnp.stack([mids[(i // (2 * s8)) * 2 * s8 + s8 - 1] for i in range(nt)])
                    fac = jnp.exp2(-jnp.abs(b3 - b_mid))
                    x3 = jnp.stack([q3[i] if i & s8 else k3[i] for i in range(nt)]) * fac
                    xj = x3.reshape(L, LANES)
                    a_j = _dot_nt(jnp.concatenate([x3[i] for i in up], axis=0), xj)
                tick()
                yield
                for n, i in enumerate(up):
                    keep = lvl_ref[i * SUBLANES:(i + 1) * SUBLANES, :] == j
                    acc[i] = jnp.where(keep, a_j[n * SUBLANES:(n + 1) * SUBLANES, :], acc[i])
            o = diag * iv + _dot(jnp.concatenate(acc, axis=0), iv)
            tick()
            yield
        else:
            b = _cumsum_rows(log_f)
            pair = []
            for j in range(1, L):
                dec = jnp.exp2(jnp.minimum(b - pltpu.roll(b, j, 0), 0.0))
                pair.append(jnp.sum(jnp.where(row >= j, qv * pltpu.roll(k, j, 0) * dec, 0.0),
                                    axis=-1, keepdims=True))
            tick()
            yield
            o = diag * iv
            for j, a in enumerate(pair, start=1):
                o = o + a * pltpu.roll(iv, j, 0)

        o = o + _dot(qv * jnp.exp2(b), s_prev)
        tick()
        yield
        b_last = b[L - 1:L, :]
        e_last = jnp.exp2(b_last)
        e_hi = _bf(e_last).astype(F32)
        sub = lax.broadcasted_iota(jnp.int32, (SUBLANES, LANES), 0)
        e_rows = jnp.where(sub == 0, e_hi, jnp.where(sub == 1, e_last - e_hi, 0.0))
        e_col = _dot_tn(e_rows, jnp.ones((SUBLANES, LANES), BF16))
        s_new = e_col * s_prev + _dot_tn(k * jnp.exp2(b_last - b), iv)
        ms = jnp.mean(o * o, axis=-1, keepdims=True)
        tick()
        yield
        assert not pieces, "not enough lock-step rounds for the projection pieces"
        y = o * lax.rsqrt(ms + EPS) * gn_ref[:, hs]
        gv = zblock(3)
        mx_ref[s, :, hs] = (y * (gv * _sigmoid(gv))).astype(mx_ref.dtype)
        return s_new

    def step(zc_ref, zn_ref):
        def per_seq(s, carry_):
            for h0 in range(0, n_heads, group_heads):
                heads = list(range(h0, h0 + group_heads))
                new = _lock_step([unit(s, h, zc_ref, zn_ref, slot) for slot, h in enumerate(heads)])
                for h, s_new in zip(heads, new):
                    s_ref[s, h] = s_new
            return carry_
        if zn_ref is None and L <= SUBLANES and sb % 2 == 0:
            def seq_pair(i, carry_):
                units = [(2 * i + j, h) for j in range(2) for h in range(n_heads)]
                new = _lock_step([unit(s, h, zc_ref, zn_ref, 0) for s, h in units])
                for (s, h), s_new in zip(units, new):
                    s_ref[s, h] = s_new
                return carry_
            lax.fori_loop(0, sb // 2, seq_pair, 0)
        elif zn_ref is None:
            lax.fori_loop(0, sb, per_seq, 0)
        else:
            for s in range(sb):
                per_seq(s, 0)

    if n_chunks > 1:
        @pl.when(chunk % 2 == 0)
        def _():
            step(za_ref, zb_ref)

        @pl.when(chunk % 2 == 1)
        def _():
            step(zb_ref, za_ref)
    else:
        step(za_ref, None)


def _hgrn_mixer(x, gnm, w, lb_logits, gain, s0, *, chunk, sb, layer, group_heads):
    n_seq, seq_len, d = x.shape
    n_heads = s0.shape[1]
    n_chunks = seq_len // chunk
    width = n_heads * HEAD_DIM
    bcast = s0.shape[0] == 1 and n_seq > 1
    sb0 = 1 if bcast else sb
    st = (lambda i: 0) if bcast else (lambda i: i)
    full2 = lambda a: pl.BlockSpec(a.shape, lambda i, c: (0, 0))
    in_specs = [pl.BlockSpec((sb, chunk, d), lambda i, c: (i, c, 0))]
    args = [x]
    if n_chunks > 1:
        in_specs.append(pl.BlockSpec((sb, chunk, d), lambda i, c: (i, jnp.minimum(c + 1, n_chunks - 1), 0)))
        args.append(x)
    w_spec = pl.BlockSpec(w.shape, lambda i, c: (0, 0), pipeline_mode=pl.Buffered(1))
    in_specs += [full2(gnm), w_spec, full2(lb_logits), full2(gain),
                 pl.BlockSpec((sb0, n_heads, HEAD_DIM, HEAD_DIM), lambda i, c: (st(i), 0, 0, 0))]
    args += [gnm, w, lb_logits, gain, s0]
    if chunk > SUBLANES:
        tri, lvl = _hgrn_level_tables(chunk)
        in_specs += [pl.BlockSpec(tri.shape, lambda i, c: (0, 0)), pl.BlockSpec(lvl.shape, lambda i, c: (0, 0))]
        args += [jnp.asarray(tri, BF16), jnp.asarray(lvl)]
    out_specs = (pl.BlockSpec((sb, chunk, width), lambda i, c: (i, c, 0)),
                 pl.BlockSpec((sb, n_heads, HEAD_DIM, HEAD_DIM), lambda i, c: (i, 0, 0, 0)))
    out_shape = (jax.ShapeDtypeStruct((n_seq, seq_len, width), BF16),
                 jax.ShapeDtypeStruct((n_seq, n_heads, HEAD_DIM, HEAD_DIM), F32))
    scratch = [pltpu.VMEM((sb * chunk, w.shape[1]), F32)]
    if n_chunks > 1:
        scratch += [pltpu.VMEM((sb * chunk, w.shape[1]), F32), pltpu.VMEM((sb * chunk, d), BF16)]
    return pl.pallas_call(
        functools.partial(_hgrn_kernel, sb=sb, L=chunk, layer=layer, n_heads=n_heads, n_chunks=n_chunks,
                          group_heads=group_heads),
        grid=(n_seq // sb, n_chunks),
        in_specs=in_specs,
        out_specs=out_specs,
        out_shape=out_shape,
        scratch_shapes=scratch,
        compiler_params=_params(2),
        name="hgrn_mixer",
    )(*args)


def _ffn_kernel(*refs, n_mixed, s_blk, L, d_ff, final_norm):
    x_ref = refs[0]
    mix_refs = refs[1:1 + n_mixed]
    wo_refs = refs[1 + n_mixed:1 + 2 * n_mixed]
    (gn_ref, win_ref, cw_ref, cb_ref, wout_ref, buf_ref) = refs[1 + 2 * n_mixed:7 + 2 * n_mixed]
    pos = 7 + 2 * n_mixed
    if final_norm:
        gfin_ref = refs[pos]
        pos += 1
    xo_ref, bufo_ref, carry_ref = refs[pos:pos + 3]
    tile = pl.program_id(1)
    tm = s_blk * L

    @pl.when(tile == 0)
    def _():
        carry_ref[...] = jnp.broadcast_to(buf_ref[...], carry_ref.shape)

    x1 = x_ref[...]
    for m_ref, w_ref in zip(mix_refs, wo_refs):
        x1 = x1 + jnp.dot(m_ref[...], w_ref[...], preferred_element_type=F32)
    xn = _rms_norm(x1, gn_ref[...])
    ug = _dot(xn, win_ref[...])
    u = ug[:, :d_ff]
    gate = ug[:, d_ff:]

    t = lax.broadcasted_iota(jnp.int32, (tm, 1), 0) & (L - 1)
    if s_blk == 1:
        p0 = carry_ref[0, 0:1, :]
        p1 = carry_ref[0, 1:2, :]
    else:
        cr = carry_ref[...]
        p0 = jnp.broadcast_to(cr[:, 0:1, :], (s_blk, L, d_ff)).reshape(tm, d_ff)
        p1 = jnp.broadcast_to(cr[:, 1:2, :], (s_blk, L, d_ff)).reshape(tm, d_ff)
    prev1 = jnp.where(t == 0, p1, pltpu.roll(u, 1, 0))
    prev2 = jnp.where(t == 0, p0, jnp.where(t == 1, p1, pltpu.roll(u, 2, 0)))
    cw = cw_ref[...]
    conv = cb_ref[...] + (prev2 * cw[0:1, :] + prev1 * cw[1:2, :] + u * cw[2:3, :])
    hid = conv * _sigmoid(conv) * gate
    x2 = x1 + _dot(hid, wout_ref[...])
    if final_norm:
        xo_ref[...] = _rms_norm(x2, gfin_ref[...])
    else:
        xo_ref[...] = x2

    if s_blk == 1:
        carry_ref[0] = u[tm - (CONV_W - 1):tm, :]
    else:
        carry_ref[...] = u.reshape(s_blk, L, d_ff)[:, L - (CONV_W - 1):L, :]
    bufo_ref[...] = carry_ref[...]


def _ffn(x2d, mixed, wo_parts, gn, w_in, cw, cb, w_out, buf, gfin, *, n_seq, seq_len, tm):
    rows, d = x2d.shape
    d_ff = w_out.shape[0]
    if tm <= seq_len:
        s_blk, L = 1, tm
        tiles = seq_len // tm
        grid = (n_seq, tiles)
        rowmap = lambda s, j: (s * tiles + j, 0)
    else:
        s_blk, L = tm // seq_len, seq_len
        grid = (n_seq // s_blk, 1)
        rowmap = lambda s, j: (s, 0)
    if isinstance(buf, tuple):
        buf, layer = buf
        buf_spec = pl.BlockSpec((None, s_blk, CONV_W - 1, d_ff), lambda s, j: (layer, s, 0, 0))
    else:
        bcast = buf.shape[0] == 1 and n_seq > 1
        bufmap = (lambda s, j: (0, 0, 0)) if bcast else (lambda s, j: (s, 0, 0))
        buf_spec = pl.BlockSpec((1 if bcast else s_blk, CONV_W - 1, d_ff), bufmap)
    const = lambda s, j: (0, 0)
    n_mixed = len(mixed)
    in_specs = [pl.BlockSpec((tm, d), rowmap)]
    in_specs += [pl.BlockSpec((tm, m.shape[1]), rowmap) for m in mixed]
    once = lambda a: pl.BlockSpec(a.shape, const, pipeline_mode=pl.Buffered(1))
    in_specs += [once(w) for w in wo_parts]
    in_specs += [pl.BlockSpec((1, d), const),
                 once(w_in),
                 pl.BlockSpec(cw.shape, const),
                 pl.BlockSpec((1, d_ff), const),
                 once(w_out),
                 buf_spec]
    args = [x2d, *mixed, *wo_parts, gn, w_in, cw, cb, w_out, buf]
    if gfin is not None:
        in_specs.append(pl.BlockSpec((1, d), const))
        args.append(gfin)
    out_specs = (pl.BlockSpec((tm, d), rowmap),
                 pl.BlockSpec((s_blk, CONV_W - 1, d_ff), lambda s, j: (s, 0, 0)))
    out_shape = (jax.ShapeDtypeStruct((rows, d), F32),
                 jax.ShapeDtypeStruct((n_seq, CONV_W - 1, d_ff), F32))
    return pl.pallas_call(
        functools.partial(_ffn_kernel, n_mixed=n_mixed, s_blk=s_blk, L=L, d_ff=d_ff,
                          final_norm=gfin is not None),
        grid=grid,
        in_specs=in_specs,
        out_specs=out_specs,
        out_shape=out_shape,
        scratch_shapes=[pltpu.VMEM((s_blk, CONV_W - 1, d_ff), F32)],
        compiler_params=_params(2),
        name="ffn",
    )(*args)


def _tile_rows(n_seq, seq_len, target):
    rows = n_seq * seq_len
    if seq_len >= target:
        return target
    return min(rows, target)


def _trunk(x, pos, st, w, *, chunk_ab, chunk_c, sb_ab, sb_c, unroll_ab, group_c):
    n_seq, seq_len, d = x.shape
    rows = n_seq * seq_len
    tm = _tile_rows(n_seq, seq_len, 512)
    x2d = x.reshape(rows, d)

    half = HEAD_DIM // 2
    inv = 1.0 / (ROPE_BASE ** jnp.linspace(0.0, 1.0, half, dtype=F32))
    ang = pos[:, None] * inv[None, :]
    cos_t = jnp.concatenate([jnp.cos(ang), jnp.cos(ang)], axis=-1)
    sin_t = jnp.concatenate([-jnp.sin(ang), jnp.sin(ang)], axis=-1)

    mxa, mxb, c_new, n_new, m_new, r_new = _ab_mixer(
        x, w["norm_mix"][0], w["w_in_ab"], cos_t, sin_t, w["gate_bias"], w["lg"], w["gn_a"], w["gn_b"],
        st["c"], st["n"], st["m"], st["r"], chunk=chunk_ab, sb=sb_ab, unroll=unroll_ab)
    hd = H_AB * HEAD_DIM
    x2d, buf0 = _ffn(x2d, [mxa.reshape(rows, hd), mxb.reshape(rows, hd)], w["w_out_ab"],
                     w["norm_ffn"][0], w["w_ffn_in"][0], w["conv_w"][0], w["conv_b"][0], w["w_ffn_out"][0],
                     st["conv"][0], None, n_seq=n_seq, seq_len=seq_len, tm=tm)

    mx, s_new = _hgrn_mixer(x2d.reshape(n_seq, seq_len, d), w["norm_mix"][1], w["w_in_c"], w["lb_logits"],
                            w["gn_c"], st["s"], chunk=chunk_c, sb=sb_c, layer=1, group_heads=group_c)
    y2d, buf1 = _ffn(x2d, [mx.reshape(rows, -1)], w["w_out_c"],
                     w["norm_ffn"][1], w["w_ffn_in"][1], w["conv_w"][1], w["conv_b"][1], w["w_ffn_out"][1],
                     st["conv"][1], w["norm_final"], n_seq=n_seq, seq_len=seq_len, tm=tm)
    new_st = {"c": c_new, "n": n_new, "m": m_new, "r": r_new, "s": s_new, "conv": (buf0, buf1)}
    return y2d.reshape(n_seq, seq_len, d), new_st


def _lane_rep(v):
    return jnp.broadcast_to(v[..., None, None], v.shape + (1, LANES))


def kernel(x_prompt, x_sample, state_mlstm_C, state_mlstm_n, state_mlstm_m, state_ret_S, state_hgrn_S,
           state_ffn_conv, meta_tokens, norm_mix, w_in_ab, b_igate, b_fgate, gn_mlstm, gn_ret, w_out_ab,
           lb_logits, w_in_c, gn_hgrn, w_out_c, norm_ffn, w_ffn_in, conv_w, conv_b, w_ffn_out, norm_final):
    bp, seq, d = x_prompt.shape
    bs, dec_seq, _ = x_sample.shape
    hd = H_AB * HEAD_DIM
    n_hc = state_hgrn_S.shape[2]
    d_ff = w_ffn_out.shape[1]
    assert w_in_ab.shape[0] == 1 and w_in_c.shape[0] == 1 and norm_mix.shape[0] == 2

    wab = w_in_ab[0].astype(BF16)
    g0 = 4 * hd
    w_ab = jnp.concatenate([wab[:, :g0], wab[:, g0 + 2 * H_AB:], wab[:, g0:g0 + 2 * H_AB],
                            jnp.zeros((d, LANES - 2 * H_AB), BF16)], axis=1)
    wo_ab = w_out_ab[0].astype(BF16)
    log_gamma = jnp.log1p(-jnp.exp2(-5.0 - jnp.arange(H_AB, dtype=F32)))
    gate_bias = jnp.concatenate([b_igate[0], b_fgate[0], jnp.zeros((LANES - 2 * H_AB,), F32)])[None, :]
    w = {
        "norm_mix": norm_mix[:, None, :],
        "norm_ffn": norm_ffn[:, None, :],
        "norm_final": norm_final[None, :],
        "w_in_ab": w_ab,
        "gate_bias": gate_bias,
        "lg": _lane_rep(log_gamma),
        "gn_a": gn_mlstm[0][None, :],
        "gn_b": gn_ret[0][None, :],
        "w_out_ab": [wo_ab[:hd], wo_ab[hd:]],
        "lb_logits": lb_logits,
        "w_in_c": w_in_c[0].astype(BF16),
        "gn_c": gn_hgrn[0][None, :],
        "w_out_c": [w_out_c[0].astype(BF16)],
        "w_ffn_in": w_ffn_in.astype(BF16),
        "conv_w": conv_w,
        "conv_b": conv_b[:, None, :],
        "w_ffn_out": w_ffn_out.astype(BF16),
    }

    def zero_state(n):
        return {"c": jnp.zeros((n, H_AB, HEAD_DIM, HEAD_DIM), F32),
                "n": jnp.zeros((n, H_AB, 1, LANES), F32),
                "m": jnp.zeros((n, H_AB, 1, LANES), F32),
                "r": jnp.zeros((n, H_AB, HEAD_DIM, HEAD_DIM), F32),
                "s": jnp.zeros((n, n_hc, HEAD_DIM, HEAD_DIM), F32),
                "conv": (jnp.zeros((n, CONV_W - 1, d_ff), F32), jnp.zeros((n, CONV_W - 1, d_ff), F32))}

    _, st_meta = _trunk(meta_tokens[None].astype(F32), jnp.arange(N_META, dtype=F32), zero_state(1), w,
                        chunk_ab=N_META, chunk_c=N_META, sb_ab=1, sb_c=1, unroll_ab=1, group_c=4)
    pos_p = N_META + jnp.arange(seq, dtype=F32)
    y_prompt, st_p = _trunk(x_prompt, pos_p, st_meta, w, chunk_ab=128, chunk_c=128,
                            sb_ab=min(bp, 2), sb_c=min(bp, 2), unroll_ab=min(bp, 2), group_c=n_hc)
    st_s = {"c": state_mlstm_C[0], "n": state_mlstm_n[0][:, :, None, :], "m": _lane_rep(state_mlstm_m[0]),
            "r": state_ret_S[0], "s": state_hgrn_S[0], "conv": ((state_ffn_conv, 0), (state_ffn_conv, 1))}
    pos_s = PAST_LEN + jnp.arange(dec_seq, dtype=F32)
    y_sample, st_s = _trunk(x_sample, pos_s, st_s, w, chunk_ab=dec_seq, chunk_c=dec_seq,
                            sb_ab=min(bs, 16), sb_c=min(bs, 16), unroll_ab=min(bs, 2), group_c=n_hc)

    def outs(s):
        return (s["c"][None], s["n"][:, :, 0, :][None], s["m"][:, :, 0, 0][None], s["r"][None], s["s"][None],
                jnp.stack(s["conv"]))

    cp, n_p, mp, rp, sp, convp = outs(st_p)
    cs, n_s, ms, rs, ss, convs = outs(st_s)
    return (y_prompt, y_sample, cp, cs, n_p, n_s, mp, ms, rp, rs, sp, ss, convp, convs)
```

```python
import functools

import numpy as np

import jax
import jax.numpy as jnp
from jax import lax
from jax.experimental import pallas as pl
from jax.experimental.pallas import tpu as pltpu

EPS = 1e-6
N_META = 16
PAST_LEN = 16384
ROPE_BASE = 10000.0
HEAD_DIM = 128
H_AB = 4
CONV_W = 3
LANES = 128
SUBLANES = 8
VMEM_LIMIT = 56 * 1024 * 1024
F32 = jnp.float32
BF16 = jnp.bfloat16


def _bf(x):
    return x.astype(BF16)


def _dot(a, b):
    return jnp.dot(_bf(a), _bf(b), preferred_element_type=F32)


def _dot_nt(a, b):
    return lax.dot_general(_bf(a), _bf(b), (((1,), (1,)), ((), ())), preferred_element_type=F32)


def _dot_tn(a, b):
    return lax.dot_general(_bf(a), _bf(b), (((0,), (0,)), ((), ())), preferred_element_type=F32)


def _rms_norm(x, gain):
    y = x * lax.rsqrt(jnp.mean(x * x, axis=-1, keepdims=True) + EPS)
    return y * gain


def _sigmoid(x):
    return 1.0 / (1.0 + jnp.exp(-x))


def _layer_norm(x, gain, tick):
    mu = jnp.mean(x, axis=-1, keepdims=True)
    if x.shape[0] > SUBLANES:
        tick()
        yield
    cen = x - mu
    var = jnp.mean(cen * cen, axis=-1, keepdims=True)
    if x.shape[0] > SUBLANES:
        tick()
        yield
    return cen * lax.rsqrt(var + EPS) * gain


def _cumsum_rows(x):
    n = x.shape[0]
    row = lax.broadcasted_iota(jnp.int32, x.shape, 0)
    k = 1
    while k < n:
        x = x + jnp.where(row >= k, pltpu.roll(x, k, 0), 0.0)
        k *= 2
    return x


def _split_dot(mat_bf16, x):
    hi = _bf(x)
    lo = _bf(x - hi.astype(F32))
    n = x.shape[1]
    r = jnp.dot(mat_bf16, jnp.concatenate([hi, lo], axis=1), preferred_element_type=F32)
    return r[:, :n] + r[:, n:]


def _lock_step(gens):
    results = [None] * len(gens)
    live = list(enumerate(gens))
    while live:
        still = []
        for idx, g in live:
            try:
                next(g)
                still.append((idx, g))
            except StopIteration as stop:
                results[idx] = stop.value
        live = still
    return results


def _params(n_grid):
    return pltpu.CompilerParams(dimension_semantics=("arbitrary",) * n_grid,
                                vmem_limit_bytes=VMEM_LIMIT)


def _ab_kernel(*refs, sb, L, unroll, n_chunks):
    assert sb % unroll == 0
    refs = list(refs)
    xc_ref = refs.pop(0)
    xnext_ref = refs.pop(0) if n_chunks > 1 else None
    (gnm_ref, w_ref, cos_ref, sin_ref, bias_ref, lg_ref, gna_ref, gnb_ref,
     c0_ref, n0_ref, m0_ref, r0_ref) = [refs.pop(0) for _ in range(12)]
    if L > SUBLANES:
        tri_ref = refs.pop(0)
    mxa_ref, mxb_ref, c_ref, n_ref, m_ref, r_ref, za_ref = [refs.pop(0) for _ in range(7)]
    if n_chunks > 1:
        zb_ref, xnb_ref = refs
    chunk = pl.program_id(1)
    rows_all = sb * L
    d_model = xc_ref.shape[-1]

    @pl.when(chunk == 0)
    def _():
        c_ref[...] = jnp.broadcast_to(c0_ref[...], c_ref.shape)
        n_ref[...] = jnp.broadcast_to(n0_ref[...], n_ref.shape)
        m_ref[...] = jnp.broadcast_to(m0_ref[...], m_ref.shape)
        r_ref[...] = jnp.broadcast_to(r0_ref[...], r_ref.shape)
        xn0 = _rms_norm(xc_ref[...].reshape(rows_all, d_model), gnm_ref[...])
        za_ref[...] = _dot(xn0, w_ref[...])

    if n_chunks > 1:
        xnb_ref[...] = _bf(_rms_norm(xnext_ref[...].reshape(rows_all, d_model), gnm_ref[...]))
    cur = {}

    def seq_rows(s):
        return pl.ds(pl.multiple_of(s * L, L), L)

    def project_next(s, c0, c1):
        if cur["zn"] is not None:
            rows = seq_rows(s)
            cur["zn"][rows, c0:c1] = jnp.dot(xnb_ref[rows, :], w_ref[:, c0:c1], preferred_element_type=F32)

    def ticker(s, slab, first_round):
        half_w = 2 * HEAD_DIM
        pieces = {first_round: 0, first_round + 1: 1} if cur["zn"] is not None else {}
        rounds = [0]

        def tick():
            rounds[0] += 1
            half = pieces.pop(rounds[0], None)
            if half is not None:
                c0 = (2 * slab + half) * half_w
                project_next(s, c0, c0 + half_w)

        tick.pending = pieces
        return tick

    scale = HEAD_DIM ** -0.5
    row = lax.broadcasted_iota(jnp.int32, (L, L), 0)
    col = lax.broadcasted_iota(jnp.int32, (L, L), 1)
    causal = row >= col
    eye = row == col
    rel = jnp.where(causal, row - col, 0).astype(F32)
    tcol = lax.broadcasted_iota(jnp.int32, (L, 1), 0).astype(F32)
    cosv = cos_ref[...]
    sinv = sin_ref[...]
    bias = bias_ref[...]
    gate_col = 8 * H_AB * HEAD_DIM
    ones_bf = jnp.ones((L, LANES), BF16)
    ret_tabs = []
    for h in range(H_AB):
        lg = lg_ref[h][:, :1]
        ret_tabs.append((jnp.where(causal, jnp.exp(rel * lg), 0.0),
                         jnp.broadcast_to(jnp.exp((tcol + 1.0) * lg), (L, LANES)),
                         jnp.broadcast_to(jnp.exp((L - 1.0 - tcol) * lg), (L, LANES)),
                         jnp.exp(L * lg)))

    def gates(s):
        project_next(s, gate_col, gate_col + LANES)
        pre = cur["zc"][seq_rows(s), gate_col:gate_col + LANES] + bias
        lf = jnp.minimum(pre, 0.0) - jnp.log1p(jnp.exp(-jnp.abs(pre)))
        lf = pltpu.roll(lf, LANES - H_AB, 1)
        if L > SUBLANES:
            b = _split_dot(tri_ref[...], lf)
            yield
        else:
            b = _cumsum_rows(lf)
        a = pre - b
        a_rows = jnp.transpose(a) if L == LANES else None
        return a, b, a_rows

    def zcol(s, j, h):
        c0 = (j * H_AB + h) * HEAD_DIM
        return cur["zc"][seq_rows(s), c0:c0 + HEAD_DIM]

    def mlstm_head(s, h, a, b, a_rows):
        tick = ticker(s, 2 * h, 1 + 2 * (h % 2))
        hs = slice(h * HEAD_DIM, (h + 1) * HEAD_DIM)
        a_col = a[:, h:h + 1]
        b_col = b[:, h:h + 1]
        if a_rows is not None:
            a_row = a_rows[h:h + 1, :]
        else:
            a_row = jnp.sum(jnp.where(eye, a_col, 0.0), axis=0, keepdims=True)
        m0 = m_ref[s, h][:, :1]
        a_caus = jnp.where(causal, a_row, -jnp.inf)
        a_max = jnp.max(a_caus, axis=-1, keepdims=True)
        if L > SUBLANES:
            tick()
            yield
        mx = jnp.maximum(m0, a_max)
        m_t = b_col + mx
        d = jnp.exp(a_caus - mx)
        inter = jnp.broadcast_to(jnp.exp(m0 - mx), (L, LANES))
        q_f32 = zcol(s, 0, h)
        q = _bf(q_f32)
        k = zcol(s, 1, h) * scale
        v = _bf(zcol(s, 2, h))
        c_prev = c_ref[s, h]
        n_prev = n_ref[s, h]
        qk = _dot_nt(q, k)
        qc = _dot(q, c_prev)
        tick()
        yield
        sm_f32 = qk * d
        sm = _bf(sm_f32)
        m_new = m_t[L - 1:L, :]
        b_last = b_col[L - 1:L, :]
        w_col = jnp.exp(a_col + b_last - m_new)
        keep = jnp.exp(b_last + m0 - m_new)
        kw = k * w_col
        if L > SUBLANES:
            nd = jnp.dot(sm, jnp.concatenate([v, ones_bf], axis=1), preferred_element_type=F32)
            qn = _dot_nt(q, jnp.broadcast_to(n_prev, (HEAD_DIM, LANES)))
            kv = _dot_tn(kw, v)
            tick()
            yield
            num = nd[:, :LANES] + inter * qc
            den = nd[:, LANES:] + inter * qn
        else:
            sv = _dot(sm, v)
            kv = _dot_tn(kw, v)
            tick()
            yield
            num = sv + inter * qc
            den = (jnp.sum(sm_f32, axis=-1, keepdims=True)
                   + inter * jnp.sum(q_f32 * n_prev, axis=-1, keepdims=True))
        hv = num / jnp.maximum(jnp.abs(den), jnp.exp(-m_t))
        y = yield from _layer_norm(hv, gna_ref[:, hs], tick)
        assert not tick.pending, "not enough lock-step rounds for the projection pieces"
        mxa_ref[s, :, hs] = (_sigmoid(zcol(s, 3, h)) * y).astype(mxa_ref.dtype)
        return [(c_ref, h, keep * c_prev + kv),
                (n_ref, h, keep * n_prev + jnp.sum(kw, axis=0, keepdims=True)),
                (m_ref, h, jnp.broadcast_to(m_new, (1, LANES)))]

    def ret_head(s, h):
        tick = ticker(s, 2 * h + 1, 3 - 2 * (h % 2))
        hs = slice(h * HEAD_DIM, (h + 1) * HEAD_DIM)
        decay, inner, tail, g_pow = ret_tabs[h]
        q2 = zcol(s, 4, h)
        k2 = zcol(s, 5, h)
        v2 = _bf(zcol(s, 6, h))
        qr = _bf(q2 * cosv + pltpu.roll(q2, HEAD_DIM // 2, 1) * sinv)
        kr = (k2 * cosv + pltpu.roll(k2, HEAD_DIM // 2, 1) * sinv) * scale
        r_prev = r_ref[s, h]
        qk = _dot_nt(qr, kr)
        qs = _dot(qr, r_prev)
        kv = _dot_tn(kr * tail, v2)
        tick()
        yield
        o2 = _dot(qk * decay, v2)
        tick()
        yield
        o2 = o2 + qs * inner
        y2 = yield from _layer_norm(o2, gnb_ref[:, hs], tick)
        assert not tick.pending, "not enough lock-step rounds for the projection pieces"
        gv = zcol(s, 7, h)
        mxb_ref[s, :, hs] = (gv * _sigmoid(gv) * y2).astype(mxb_ref.dtype)
        return [(r_ref, h, g_pow * r_prev + kv)]

    def group(i, carry_):
        seqs = [i * unroll + j for j in range(unroll)]
        prep = _lock_step([gates(s) for s in seqs])
        chains = []
        for s, (a, b, a_rows) in zip(seqs, prep):
            for h in range(H_AB):
                chains.append((s, mlstm_head(s, h, a, b, a_rows)))
        for s in seqs:
            for h in range(H_AB):
                chains.append((s, ret_head(s, h)))
        new = _lock_step([g for _, g in chains])
        for (s, _), new_state in zip(chains, new):
            for ref, h, val in new_state:
                ref[s, h] = val
        return carry_

    def step(zc_ref, zn_ref):
        cur["zc"], cur["zn"] = zc_ref, zn_ref
        lax.fori_loop(0, sb // unroll, lambda i, carry_: group(i, carry_), 0)

    if n_chunks > 1:
        @pl.when(chunk % 2 == 0)
        def _():
            step(za_ref, zb_ref)

        @pl.when(chunk % 2 == 1)
        def _():
            step(zb_ref, za_ref)
    else:
        step(za_ref, None)


def _ab_mixer(x, gnm, w, cos_t, sin_t, bias, lg, gna, gnb, c0, n0, m0, r0, *, chunk, sb, unroll):
    n_seq, seq_len, d = x.shape
    n_chunks = seq_len // chunk
    width = w.shape[1]
    hd = H_AB * HEAD_DIM
    bcast = c0.shape[0] == 1 and n_seq > 1
    sb0 = 1 if bcast else sb
    st = (lambda i: 0) if bcast else (lambda i: i)
    full2 = lambda a: pl.BlockSpec(a.shape, lambda i, c: (0, 0))
    mat_in = pl.BlockSpec((sb0, H_AB, HEAD_DIM, HEAD_DIM), lambda i, c: (st(i), 0, 0, 0))
    vec_in = pl.BlockSpec((sb0, H_AB, 1, LANES), lambda i, c: (st(i), 0, 0, 0))
    mat_out = pl.BlockSpec((sb, H_AB, HEAD_DIM, HEAD_DIM), lambda i, c: (i, 0, 0, 0))
    vec_out = pl.BlockSpec((sb, H_AB, 1, LANES), lambda i, c: (i, 0, 0, 0))
    mix_out = pl.BlockSpec((sb, chunk, hd), lambda i, c: (i, c, 0))
    tab = pl.BlockSpec((chunk, LANES), lambda i, c: (c, 0))
    in_specs = [pl.BlockSpec((sb, chunk, d), lambda i, c: (i, c, 0))]
    args = [x]
    if n_chunks > 1:
        in_specs.append(pl.BlockSpec((sb, chunk, d), lambda i, c: (i, jnp.minimum(c + 1, n_chunks - 1), 0)))
        args.append(x)
    w_spec = pl.BlockSpec(w.shape, lambda i, c: (0, 0), pipeline_mode=pl.Buffered(1))
    in_specs += [full2(gnm), w_spec, tab, tab, full2(bias),
                 pl.BlockSpec(lg.shape, lambda i, c: (0, 0, 0)), full2(gna), full2(gnb),
                 mat_in, vec_in, vec_in, mat_in]
    args += [gnm, w, cos_t, sin_t, bias, lg, gna, gnb, c0, n0, m0, r0]
    if chunk > SUBLANES:
        tri = jnp.asarray(np.tril(np.ones((chunk, chunk), np.float32)), BF16)
        in_specs.append(full2(tri))
        args.append(tri)
    scratch = [pltpu.VMEM((sb * chunk, width), F32)]
    if n_chunks > 1:
        scratch += [pltpu.VMEM((sb * chunk, width), F32), pltpu.VMEM((sb * chunk, d), BF16)]
    out_shape = (jax.ShapeDtypeStruct((n_seq, seq_len, hd), BF16),
                 jax.ShapeDtypeStruct((n_seq, seq_len, hd), BF16),
                 jax.ShapeDtypeStruct((n_seq, H_AB, HEAD_DIM, HEAD_DIM), F32),
                 jax.ShapeDtypeStruct((n_seq, H_AB, 1, LANES), F32),
                 jax.ShapeDtypeStruct((n_seq, H_AB, 1, LANES), F32),
                 jax.ShapeDtypeStruct((n_seq, H_AB, HEAD_DIM, HEAD_DIM), F32))
    return pl.pallas_call(
        functools.partial(_ab_kernel, sb=sb, L=chunk, unroll=unroll, n_chunks=n_chunks),
        grid=(n_seq // sb, n_chunks),
        in_specs=in_specs,
        out_specs=(mix_out, mix_out, mat_out, vec_out, vec_out, mat_out),
        out_shape=out_shape,
        scratch_shapes=scratch,
        compiler_params=_params(2),
        name="ab_mixer",
    )(*args)


def _hgrn_level_tables(L):
    t = np.arange(L)[:, None]
    u = np.arange(L)[None, :]
    lvl = np.full((L, L), -1, np.int32)
    size, j = 1, 0
    while size < L:
        upper = ((t // size) % 2) == 1
        lvl[((t // size) == (u // size) + 1) & upper] = j
        size *= 2
        j += 1
    return (u <= t).astype(np.float32), lvl


def _hgrn_small_level_factors(f, r):
    one = jnp.ones_like(f)
    prev = lambda x, d: pltpu.roll(x, d, 1)
    nxt = lambda x, d: pltpu.roll(x, SUBLANES - d, 1)
    f_n1 = nxt(f, 1)
    p1 = f * prev(f, 1)
    sel = lambda idx, *vals: functools.reduce(
        lambda acc, iv: jnp.where(idx == iv[0], iv[1], acc), list(enumerate(vals))[:-1], vals[-1])
    g0 = jnp.where((r & 1) == 1, f, one)
    g1 = sel(r & 3, f_n1, one, f, p1)
    g2 = sel(r, f_n1 * nxt(p1, 3), nxt(p1, 2), f_n1, one, f, p1, p1 * prev(f, 2), p1 * prev(p1, 2))
    return [g0, g1, g2]


def _hgrn_kernel(*refs, sb, L, layer, n_heads, n_chunks, group_heads):
    refs = list(refs)
    xc_ref = refs.pop(0)
    xnext_ref = refs.pop(0) if n_chunks > 1 else None
    gnm_ref, w_ref, lbl_ref, gn_ref, s0_ref = [refs.pop(0) for _ in range(5)]
    if L > SUBLANES:
        tri_ref, lvl_ref = refs.pop(0), refs.pop(0)
    mx_ref, s_ref, za_ref = refs.pop(0), refs.pop(0), refs.pop(0)
    if n_chunks > 1:
        zb_ref, xnb_ref = refs
    chunk = pl.program_id(1)
    rows_all = sb * L
    d_model = xc_ref.shape[-1]

    @pl.when(chunk == 0)
    def _():
        s_ref[...] = jnp.broadcast_to(s0_ref[...], s_ref.shape)
        xn0 = _rms_norm(xc_ref[...].reshape(rows_all, d_model), gnm_ref[...])
        za_ref[...] = _dot(xn0, w_ref[...])

    if n_chunks > 1:
        xnb_ref[...] = _bf(_rms_norm(xnext_ref[...].reshape(rows_all, d_model), gnm_ref[...]))

    logits = lbl_ref[...]
    ex = jnp.exp(logits - jnp.max(logits, axis=0, keepdims=True))
    p = ex / jnp.sum(ex, axis=0, keepdims=True)
    cum = p[0:1, :]
    for r in range(1, layer + 1):
        cum = cum + p[r:r + 1, :]
    lb_all = cum - p[0:1, :]

    row = lax.broadcasted_iota(jnp.int32, (L, LANES), 0)
    n_levels = L.bit_length() - 1
    width = n_heads * HEAD_DIM
    slab_w = 4 * width // (sb * n_heads)
    assert n_chunks == 1 or slab_w % (2 * LANES) == 0

    def unit(s, h, zc_ref, zn_ref, slot):
        rows = pl.ds(s * L, L) if isinstance(s, int) else pl.ds(pl.multiple_of(s * L, L), L)
        hs = slice(h * HEAD_DIM, (h + 1) * HEAD_DIM)
        zblock = lambda j: zc_ref[rows, j * width + h * HEAD_DIM:j * width + (h + 1) * HEAD_DIM]
        pieces = {slot + 1: s * n_heads + h} if zn_ref is not None else {}
        rounds = [0]

        def tick():
            rounds[0] += 1
            slab = pieces.pop(rounds[0], None)
            if slab is not None:
                c0 = slab * slab_w
                zn_ref[:, c0:c0 + slab_w] = jnp.dot(xnb_ref[...], w_ref[:, c0:c0 + slab_w],
                                                    preferred_element_type=F32)

        lb = lb_all[:, hs]
        qv = zblock(0)
        fv = zblock(1)
        iv = zblock(2)
        e = jnp.exp(-jnp.abs(fv))
        rcp = 1.0 / (1.0 + e)
        pos = fv >= 0.0
        sig = jnp.where(pos, rcp, e * rcp)
        sig_neg = jnp.where(pos, e * rcp, rcp)
        f_gate = lb + (1.0 - lb) * sig
        log_f = jnp.log2(f_gate)
        k = (1.0 - lb) * sig_neg
        s_prev = s_ref[s, h]
        diag = jnp.sum(qv * k, axis=-1, keepdims=True)

        if L > SUBLANES:
            b = _split_dot(tri_ref[...], log_f)
            tick()
            yield
            nt = L // SUBLANES
            tile3 = lambda x: x.reshape(nt, SUBLANES, x.shape[-1])
            q3, k3, b3 = tile3(qv), tile3(k), tile3(b)
            r3 = lax.broadcasted_iota(jnp.int32, (1, SUBLANES, LANES), 1)
            small = _hgrn_small_level_factors(tile3(f_gate), r3)
            acc = [jnp.zeros((SUBLANES, L), F32)] * nt
            for j in range(n_levels):
                size = 1 << j
                if size < SUBLANES:
                    xj = (jnp.where((r3 & size) != 0, q3, k3) * small[j]).reshape(L, LANES)
                    up = list(range(nt))
                    a_j = _dot_nt(xj, xj)
                else:
                    s8 = size // SUBLANES
                    up = [i for i in range(nt) if i & s8]
                    mids = {i: jnp.broadcast_to(b3[i, SUBLANES - 1:, :], (SUBLANES, LANES))
                            for i in range(s8 - 1, nt, 2 * s8)}
                    b_mid = jnp.stack([mids[(i // (2 * s8)) * 2 * s8 + s8 - 1] for i in range(nt)])
                    fac = jnp.exp2(-jnp.abs(b3 - b_mid))
                    x3 = jnp.stack([q3[i] if i & s8 else k3[i] for i in range(nt)]) * fac
                    xj = x3.reshape(L, LANES)
                    a_j = _dot_nt(jnp.concatenate([x3[i] for i in up], axis=0), xj)
                tick()
                yield
                for n, i in enumerate(up):
                    keep = lvl_ref[i * SUBLANES:(i + 1) * SUBLANES, :] == j
                    acc[i] = jnp.where(keep, a_j[n * SUBLANES:(n + 1) * SUBLANES, :], acc[i])
            o = diag * iv + _dot(jnp.concatenate(acc, axis=0), iv)
            tick()
            yield
        else:
            b = _cumsum_rows(log_f)
            pair = []
            for j in range(1, L):
                dec = jnp.exp2(jnp.minimum(b - pltpu.roll(b, j, 0), 0.0))
                pair.append(jnp.sum(jnp.where(row >= j, qv * pltpu.roll(k, j, 0) * dec, 0.0),
                                    axis=-1, keepdims=True))
            tick()
            yield
            o = diag * iv
            for j, a in enumerate(pair, start=1):
                o = o + a * pltpu.roll(iv, j, 0)

        o = o + _dot(qv * jnp.exp2(b), s_prev)
        tick()
        yield
        b_last = b[L - 1:L, :]
        e_last = jnp.exp2(b_last)
        e_hi = _bf(e_last).astype(F32)
        sub = lax.broadcasted_iota(jnp.int32, (SUBLANES, LANES), 0)
        e_rows = jnp.where(sub == 0, e_hi, jnp.where(sub == 1, e_last - e_hi, 0.0))
        e_col = _dot_tn(e_rows, jnp.ones((SUBLANES, LANES), BF16))
        s_new = e_col * s_prev + _dot_tn(k * jnp.exp2(b_last - b), iv)
        ms = jnp.mean(o * o, axis=-1, keepdims=True)
        tick()
        yield
        assert not pieces, "not enough lock-step rounds for the projection pieces"
        y = o * lax.rsqrt(ms + EPS) * gn_ref[:, hs]
        gv = zblock(3)
        mx_ref[s, :, hs] = (y * (gv * _sigmoid(gv))).astype(mx_ref.dtype)
        return s_new

    def step(zc_ref, zn_ref):
        def per_seq(s, carry_):
            for h0 in range(0, n_heads, group_heads):
                heads = list(range(h0, h0 + group_heads))
                new = _lock_step([unit(s, h, zc_ref, zn_ref, slot) for slot, h in enumerate(heads)])
                for h, s_new in zip(heads, new):
                    s_ref[s, h] = s_new
            return carry_
        if zn_ref is None and L <= SUBLANES and sb % 2 == 0:
            def seq_pair(i, carry_):
                units = [(2 * i + j, h) for j in range(2) for h in range(n_heads)]
                new = _lock_step([unit(s, h, zc_ref, zn_ref, 0) for s, h in units])
                for (s, h), s_new in zip(units, new):
                    s_ref[s, h] = s_new
                return carry_
            lax.fori_loop(0, sb // 2, seq_pair, 0)
        elif zn_ref is None:
            lax.fori_loop(0, sb, per_seq, 0)
        else:
            for s in range(sb):
                per_seq(s, 0)

    if n_chunks > 1:
        @pl.when(chunk % 2 == 0)
        def _():
            step(za_ref, zb_ref)

        @pl.when(chunk % 2 == 1)
        def _():
            step(zb_ref, za_ref)
    else:
        step(za_ref, None)


def _hgrn_mixer(x, gnm, w, lb_logits, gain, s0, *, chunk, sb, layer, group_heads):
    n_seq, seq_len, d = x.shape
    n_heads = s0.shape[1]
    n_chunks = seq_len // chunk
    width = n_heads * HEAD_DIM
    bcast = s0.shape[0] == 1 and n_seq > 1
    sb0 = 1 if bcast else sb
    st = (lambda i: 0) if bcast else (lambda i: i)
    full2 = lambda a: pl.BlockSpec(a.shape, lambda i, c: (0, 0))
    in_specs = [pl.BlockSpec((sb, chunk, d), lambda i, c: (i, c, 0))]
    args = [x]
    if n_chunks > 1:
        in_specs.append(pl.BlockSpec((sb, chunk, d), lambda i, c: (i, jnp.minimum(c + 1, n_chunks - 1), 0)))
        args.append(x)
    w_spec = pl.BlockSpec(w.shape, lambda i, c: (0, 0), pipeline_mode=pl.Buffered(1))
    in_specs += [full2(gnm), w_spec, full2(lb_logits), full2(gain),
                 pl.BlockSpec((sb0, n_heads, HEAD_DIM, HEAD_DIM), lambda i, c: (st(i), 0, 0, 0))]
    args += [gnm, w, lb_logits, gain, s0]
    if chunk > SUBLANES:
        tri, lvl = _hgrn_level_tables(chunk)
        in_specs += [pl.BlockSpec(tri.shape, lambda i, c: (0, 0)), pl.BlockSpec(lvl.shape, lambda i, c: (0, 0))]
        args += [jnp.asarray(tri, BF16), jnp.asarray(lvl)]
    out_specs = (pl.BlockSpec((sb, chunk, width), lambda i, c: (i, c, 0)),
                 pl.BlockSpec((sb, n_heads, HEAD_DIM, HEAD_DIM), lambda i, c: (i, 0, 0, 0)))
    out_shape = (jax.ShapeDtypeStruct((n_seq, seq_len, width), BF16),
                 jax.ShapeDtypeStruct((n_seq, n_heads, HEAD_DIM, HEAD_DIM), F32))
    scratch = [pltpu.VMEM((sb * chunk, w.shape[1]), F32)]
    if n_chunks > 1:
        scratch += [pltpu.VMEM((sb * chunk, w.shape[1]), F32), pltpu.VMEM((sb * chunk, d), BF16)]
    return pl.pallas_call(
        functools.partial(_hgrn_kernel, sb=sb, L=chunk, layer=layer, n_heads=n_heads, n_chunks=n_chunks,
                          group_heads=group_heads),
        grid=(n_seq // sb, n_chunks),
        in_specs=in_specs,
        out_specs=out_specs,
        out_shape=out_shape,
        scratch_shapes=scratch,
        compiler_params=_params(2),
        name="hgrn_mixer",
    )(*args)


def _ffn_kernel(*refs, n_mixed, s_blk, L, d_ff, final_norm):
    x_ref = refs[0]
    mix_refs = refs[1:1 + n_mixed]
    wo_refs = refs[1 + n_mixed:1 + 2 * n_mixed]
    (gn_ref, win_ref, cw_ref, cb_ref, wout_ref, buf_ref) = refs[1 + 2 * n_mixed:7 + 2 * n_mixed]
    pos = 7 + 2 * n_mixed
    if final_norm:
        gfin_ref = refs[pos]
        pos += 1
    xo_ref, bufo_ref, carry_ref = refs[pos:pos + 3]
    tile = pl.program_id(1)
    tm = s_blk * L

    @pl.when(tile == 0)
    def _():
        carry_ref[...] = jnp.broadcast_to(buf_ref[...], carry_ref.shape)

    x1 = x_ref[...]
    for m_ref, w_ref in zip(mix_refs, wo_refs):
        x1 = x1 + jnp.dot(m_ref[...], w_ref[...], preferred_element_type=F32)
    xn = _rms_norm(x1, gn_ref[...])
    ug = _dot(xn, win_ref[...])
    u = ug[:, :d_ff]
    gate = ug[:, d_ff:]

    t = lax.broadcasted_iota(jnp.int32, (tm, 1), 0) & (L - 1)
    if s_blk == 1:
        p0 = carry_ref[0, 0:1, :]
        p1 = carry_ref[0, 1:2, :]
    else:
        cr = carry_ref[...]
        p0 = jnp.broadcast_to(cr[:, 0:1, :], (s_blk, L, d_ff)).reshape(tm, d_ff)
        p1 = jnp.broadcast_to(cr[:, 1:2, :], (s_blk, L, d_ff)).reshape(tm, d_ff)
    prev1 = jnp.where(t == 0, p1, pltpu.roll(u, 1, 0))
    prev2 = jnp.where(t == 0, p0, jnp.where(t == 1, p1, pltpu.roll(u, 2, 0)))
    cw = cw_ref[...]
    conv = cb_ref[...] + (prev2 * cw[0:1, :] + prev1 * cw[1:2, :] + u * cw[2:3, :])
    hid = conv * _sigmoid(conv) * gate
    x2 = x1 + _dot(hid, wout_ref[...])
    if final_norm:
        xo_ref[...] = _rms_norm(x2, gfin_ref[...])
    else:
        xo_ref[...] = x2

    if s_blk == 1:
        carry_ref[0] = u[tm - (CONV_W - 1):tm, :]
    else:
        carry_ref[...] = u.reshape(s_blk, L, d_ff)[:, L - (CONV_W - 1):L, :]
    bufo_ref[...] = carry_ref[...]


def _ffn(x2d, mixed, wo_parts, gn, w_in, cw, cb, w_out, buf, gfin, *, n_seq, seq_len, tm):
    rows, d = x2d.shape
    d_ff = w_out.shape[0]
    if tm <= seq_len:
        s_blk, L = 1, tm
        tiles = seq_len // tm
        grid = (n_seq, tiles)
        rowmap = lambda s, j: (s * tiles + j, 0)
    else:
        s_blk, L = tm // seq_len, seq_len
        grid = (n_seq // s_blk, 1)
        rowmap = lambda s, j: (s, 0)
    if isinstance(buf, tuple):
        buf, layer = buf
        buf_spec = pl.BlockSpec((None, s_blk, CONV_W - 1, d_ff), lambda s, j: (layer, s, 0, 0))
    else:
        bcast = buf.shape[0] == 1 and n_seq > 1
        bufmap = (lambda s, j: (0, 0, 0)) if bcast else (lambda s, j: (s, 0, 0))
        buf_spec = pl.BlockSpec((1 if bcast else s_blk, CONV_W - 1, d_ff), bufmap)
    const = lambda s, j: (0, 0)
    n_mixed = len(mixed)
    in_specs = [pl.BlockSpec((tm, d), rowmap)]
    in_specs += [pl.BlockSpec((tm, m.shape[1]), rowmap) for m in mixed]
    once = lambda a: pl.BlockSpec(a.shape, const, pipeline_mode=pl.Buffered(1))
    in_specs += [once(w) for w in wo_parts]
    in_specs += [pl.BlockSpec((1, d), const),
                 once(w_in),
                 pl.BlockSpec(cw.shape, const),
                 pl.BlockSpec((1, d_ff), const),
                 once(w_out),
                 buf_spec]
    args = [x2d, *mixed, *wo_parts, gn, w_in, cw, cb, w_out, buf]
    if gfin is not None:
        in_specs.append(pl.BlockSpec((1, d), const))
        args.append(gfin)
    out_specs = (pl.BlockSpec((tm, d), rowmap),
                 pl.BlockSpec((s_blk, CONV_W - 1, d_ff), lambda s, j: (s, 0, 0)))
    out_shape = (jax.ShapeDtypeStruct((rows, d), F32),
                 jax.ShapeDtypeStruct((n_seq, CONV_W - 1, d_ff), F32))
    return pl.pallas_call(
        functools.partial(_ffn_kernel, n_mixed=n_mixed, s_blk=s_blk, L=L, d_ff=d_ff,
                          final_norm=gfin is not None),
        grid=grid,
        in_specs=in_specs,
        out_specs=out_specs,
        out_shape=out_shape,
        scratch_shapes=[pltpu.VMEM((s_blk, CONV_W - 1, d_ff), F32)],
        compiler_params=_params(2),
        name="ffn",
    )(*args)


def _tile_rows(n_seq, seq_len, target):
    rows = n_seq * seq_len
    if seq_len >= target:
        return target
    return min(rows, target)


def _trunk(x, pos, st, w, *, chunk_ab, chunk_c, sb_ab, sb_c, unroll_ab, group_c):
    n_seq, seq_len, d = x.shape
    rows = n_seq * seq_len
    tm = _tile_rows(n_seq, seq_len, 512)
    x2d = x.reshape(rows, d)

    half = HEAD_DIM // 2
    inv = 1.0 / (ROPE_BASE ** jnp.linspace(0.0, 1.0, half, dtype=F32))
    ang = pos[:, None] * inv[None, :]
    cos_t = jnp.concatenate([jnp.cos(ang), jnp.cos(ang)], axis=-1)
    sin_t = jnp.concatenate([-jnp.sin(ang), jnp.sin(ang)], axis=-1)

    mxa, mxb, c_new, n_new, m_new, r_new = _ab_mixer(
        x, w["norm_mix"][0], w["w_in_ab"], cos_t, sin_t, w["gate_bias"], w["lg"], w["gn_a"], w["gn_b"],
        st["c"], st["n"], st["m"], st["r"], chunk=chunk_ab, sb=sb_ab, unroll=unroll_ab)
    hd = H_AB * HEAD_DIM
    x2d, buf0 = _ffn(x2d, [mxa.reshape(rows, hd), mxb.reshape(rows, hd)], w["w_out_ab"],
                     w["norm_ffn"][0], w["w_ffn_in"][0], w["conv_w"][0], w["conv_b"][0], w["w_ffn_out"][0],
                     st["conv"][0], None, n_seq=n_seq, seq_len=seq_len, tm=tm)

    mx, s_new = _hgrn_mixer(x2d.reshape(n_seq, seq_len, d), w["norm_mix"][1], w["w_in_c"], w["lb_logits"],
                            w["gn_c"], st["s"], chunk=chunk_c, sb=sb_c, layer=1, group_heads=group_c)
    y2d, buf1 = _ffn(x2d, [mx.reshape(rows, -1)], w["w_out_c"],
                     w["norm_ffn"][1], w["w_ffn_in"][1], w["conv_w"][1], w["conv_b"][1], w["w_ffn_out"][1],
                     st["conv"][1], w["norm_final"], n_seq=n_seq, seq_len=seq_len, tm=tm)
    new_st = {"c": c_new, "n": n_new, "m": m_new, "r": r_new, "s": s_new, "conv": (buf0, buf1)}
    return y2d.reshape(n_seq, seq_len, d), new_st


def _lane_rep(v):
    return jnp.broadcast_to(v[..., None, None], v.shape + (1, LANES))


def kernel(x_prompt, x_sample, state_mlstm_C, state_mlstm_n, state_mlstm_m, state_ret_S, state_hgrn_S,
           state_ffn_conv, meta_tokens, norm_mix, w_in_ab, b_igate, b_fgate, gn_mlstm, gn_ret, w_out_ab,
           lb_logits, w_in_c, gn_hgrn, w_out_c, norm_ffn, w_ffn_in, conv_w, conv_b, w_ffn_out, norm_final):
    bp, seq, d = x_prompt.shape
    bs, dec_seq, _ = x_sample.shape
    hd = H_AB * HEAD_DIM
    n_hc = state_hgrn_S.shape[2]
    d_ff = w_ffn_out.shape[1]
    assert w_in_ab.shape[0] == 1 and w_in_c.shape[0] == 1 and norm_mix.shape[0] == 2

    wab = w_in_ab[0].astype(BF16)
    g0 = 4 * hd
    w_ab = jnp.concatenate([wab[:, :g0], wab[:, g0 + 2 * H_AB:], wab[:, g0:g0 + 2 * H_AB],
                            jnp.zeros((d, LANES - 2 * H_AB), BF16)], axis=1)
    wo_ab = w_out_ab[0].astype(BF16)
    log_gamma = jnp.log1p(-jnp.exp2(-5.0 - jnp.arange(H_AB, dtype=F32)))
    gate_bias = jnp.concatenate([b_igate[0], b_fgate[0], jnp.zeros((LANES - 2 * H_AB,), F32)])[None, :]
    w = {
        "norm_mix": norm_mix[:, None, :],
        "norm_ffn": norm_ffn[:, None, :],
        "norm_final": norm_final[None, :],
        "w_in_ab": w_ab,
        "gate_bias": gate_bias,
        "lg": _lane_rep(log_gamma),
        "gn_a": gn_mlstm[0][None, :],
        "gn_b": gn_ret[0][None, :],
        "w_out_ab": [wo_ab[:hd], wo_ab[hd:]],
        "lb_logits": lb_logits,
        "w_in_c": w_in_c[0].astype(BF16),
        "gn_c": gn_hgrn[0][None, :],
        "w_out_c": [w_out_c[0].astype(BF16)],
        "w_ffn_in": w_ffn_in.astype(BF16),
        "conv_w": conv_w,
        "conv_b": conv_b[:, None, :],
        "w_ffn_out": w_ffn_out.astype(BF16),
    }

    def zero_state(n):
        return {"c": jnp.zeros((n, H_AB, HEAD_DIM, HEAD_DIM), F32),
                "n": jnp.zeros((n, H_AB, 1, LANES), F32),
                "m": jnp.zeros((n, H_AB, 1, LANES), F32),
                "r": jnp.zeros((n, H_AB, HEAD_DIM, HEAD_DIM), F32),
                "s": jnp.zeros((n, n_hc, HEAD_DIM, HEAD_DIM), F32),
                "conv": (jnp.zeros((n, CONV_W - 1, d_ff), F32), jnp.zeros((n, CONV_W - 1, d_ff), F32))}

    _, st_meta = _trunk(meta_tokens[None].astype(F32), jnp.arange(N_META, dtype=F32), zero_state(1), w,
                        chunk_ab=N_META, chunk_c=N_META, sb_ab=1, sb_c=1, unroll_ab=1, group_c=4)
    pos_p = N_META + jnp.arange(seq, dtype=F32)
    y_prompt, st_p = _trunk(x_prompt, pos_p, st_meta, w, chunk_ab=128, chunk_c=128,
                            sb_ab=min(bp, 2), sb_c=min(bp, 2), unroll_ab=min(bp, 2), group_c=n_hc)
    st_s = {"c": state_mlstm_C[0], "n": state_mlstm_n[0][:, :, None, :], "m": _lane_rep(state_mlstm_m[0]),
            "r": state_ret_S[0], "s": state_hgrn_S[0], "conv": ((state_ffn_conv, 0), (state_ffn_conv, 1))}
    pos_s = PAST_LEN + jnp.arange(dec_seq, dtype=F32)
    y_sample, st_s = _trunk(x_sample, pos_s, st_s, w, chunk_ab=dec_seq, chunk_c=dec_seq,
                            sb_ab=min(bs, 16), sb_c=min(bs, 16), unroll_ab=min(bs, 2), group_c=n_hc)

    def outs(s):
        return (s["c"][None], s["n"][:, :, 0, :][None], s["m"][:, :, 0, 0][None], s["r"][None], s["s"][None],
                jnp.stack(s["conv"]))

    cp, n_p, mp, rp, sp, convp = outs(st_p)
    cs, n_s, ms, rs, ss, convs = outs(st_s)
    return (y_prompt, y_sample, cp, cs, n_p, n_s, mp, ms, rp, rs, sp, ss, convp, convs)
```

```python
import functools

import numpy as np

import jax
import jax.numpy as jnp
from jax import lax
from jax.experimental import pallas as pl
from jax.experimental.pallas import tpu as pltpu

EPS = 1e-6
N_META = 16
PAST_LEN = 16384
ROPE_BASE = 10000.0
HEAD_DIM = 128
H_AB = 4
CONV_W = 3
LANES = 128
SUBLANES = 8
VMEM_LIMIT = 56 * 1024 * 1024
F32 = jnp.float32
BF16 = jnp.bfloat16


def _bf(x):
    return x.astype(BF16)


def _dot(a, b):
    return jnp.dot(_bf(a), _bf(b), preferred_element_type=F32)


def _dot_nt(a, b):
    return lax.dot_general(_bf(a), _bf(b), (((1,), (1,)), ((), ())), preferred_element_type=F32)


def _dot_tn(a, b):
    return lax.dot_general(_bf(a), _bf(b), (((0,), (0,)), ((), ())), preferred_element_type=F32)


def _rms_norm(x, gain):
    y = x * lax.rsqrt(jnp.mean(x * x, axis=-1, keepdims=True) + EPS)
    return y * gain


def _sigmoid(x):
    return 1.0 / (1.0 + jnp.exp(-x))


def _layer_norm(x, gain, tick):
    mu = jnp.mean(x, axis=-1, keepdims=True)
    if x.shape[0] > SUBLANES:
        tick()
        yield
    cen = x - mu
    var = jnp.mean(cen * cen, axis=-1, keepdims=True)
    if x.shape[0] > SUBLANES:
        tick()
        yield
    return cen * lax.rsqrt(var + EPS) * gain


def _cumsum_rows(x):
    n = x.shape[0]
    row = lax.broadcasted_iota(jnp.int32, x.shape, 0)
    k = 1
    while k < n:
        x = x + jnp.where(row >= k, pltpu.roll(x, k, 0), 0.0)
        k *= 2
    return x


def _split_dot(mat_bf16, x):
    hi = _bf(x)
    lo = _bf(x - hi.astype(F32))
    n = x.shape[1]
    r = jnp.dot(mat_bf16, jnp.concatenate([hi, lo], axis=1), preferred_element_type=F32)
    return r[:, :n] + r[:, n:]


def _lock_step(gens):
    results = [None] * len(gens)
    live = list(enumerate(gens))
    while live:
        still = []
        for idx, g in live:
            try:
                next(g)
                still.append((idx, g))
            except StopIteration as stop:
                results[idx] = stop.value
        live = still
    return results


def _params(n_grid):
    return pltpu.CompilerParams(dimension_semantics=("arbitrary",) * n_grid,
                                vmem_limit_bytes=VMEM_LIMIT)


def _ab_kernel(*refs, sb, L, unroll, n_chunks):
    assert sb % unroll == 0
    refs = list(refs)
    xc_ref = refs.pop(0)
    xnext_ref = refs.pop(0) if n_chunks > 1 else None
    (gnm_ref, w_ref, cos_ref, sin_ref, bias_ref, lg_ref, gna_ref, gnb_ref,
     c0_ref, n0_ref, m0_ref, r0_ref) = [refs.pop(0) for _ in range(12)]
    if L > SUBLANES:
        tri_ref = refs.pop(0)
    mxa_ref, mxb_ref, c_ref, n_ref, m_ref, r_ref, za_ref = [refs.pop(0) for _ in range(7)]
    if n_chunks > 1:
        zb_ref, xnb_ref = refs
    chunk = pl.program_id(1)
    rows_all = sb * L
    d_model = xc_ref.shape[-1]

    @pl.when(chunk == 0)
    def _():
        c_ref[...] = jnp.broadcast_to(c0_ref[...], c_ref.shape)
        n_ref[...] = jnp.broadcast_to(n0_ref[...], n_ref.shape)
        m_ref[...] = jnp.broadcast_to(m0_ref[...], m_ref.shape)
        r_ref[...] = jnp.broadcast_to(r0_ref[...], r_ref.shape)
        xn0 = _rms_norm(xc_ref[...].reshape(rows_all, d_model), gnm_ref[...])
        za_ref[...] = _dot(xn0, w_ref[...])

    if n_chunks > 1:
        xnb_ref[...] = _bf(_rms_norm(xnext_ref[...].reshape(rows_all, d_model), gnm_ref[...]))
    cur = {}

    def seq_rows(s):
        return pl.ds(pl.multiple_of(s * L, L), L)

    def project_next(s, c0, c1):
        if cur["zn"] is not None:
            rows = seq_rows(s)
            cur["zn"][rows, c0:c1] = jnp.dot(xnb_ref[rows, :], w_ref[:, c0:c1], preferred_element_type=F32)

    def ticker(s, slab, first_round):
        half_w = 2 * HEAD_DIM
        pieces = {first_round: 0, first_round + 1: 1} if cur["zn"] is not None else {}
        rounds = [0]

        def tick():
            rounds[0] += 1
            half = pieces.pop(rounds[0], None)
            if half is not None:
                c0 = (2 * slab + half) * half_w
                project_next(s, c0, c0 + half_w)

        tick.pending = pieces
        return tick

    scale = HEAD_DIM ** -0.5
    row = lax.broadcasted_iota(jnp.int32, (L, L), 0)
    col = lax.broadcasted_iota(jnp.int32, (L, L), 1)
    causal = row >= col
    eye = row == col
    rel = jnp.where(causal, row - col, 0).astype(F32)
    tcol = lax.broadcasted_iota(jnp.int32, (L, 1), 0).astype(F32)
    cosv = cos_ref[...]
    sinv = sin_ref[...]
    bias = bias_ref[...]
    gate_col = 8 * H_AB * HEAD_DIM
    ones_bf = jnp.ones((L, LANES), BF16)
    ret_tabs = []
    for h in range(H_AB):
        lg = lg_ref[h][:, :1]
        ret_tabs.append((jnp.where(causal, jnp.exp(rel * lg), 0.0),
                         jnp.broadcast_to(jnp.exp((tcol + 1.0) * lg), (L, LANES)),
                         jnp.broadcast_to(jnp.exp((L - 1.0 - tcol) * lg), (L, LANES)),
                         jnp.exp(L * lg)))

    def gates(s):
        project_next(s, gate_col, gate_col + LANES)
        pre = cur["zc"][seq_rows(s), gate_col:gate_col + LANES] + bias
        lf = jnp.minimum(pre, 0.0) - jnp.log1p(jnp.exp(-jnp.abs(pre)))
        lf = pltpu.roll(lf, LANES - H_AB, 1)
        if L > SUBLANES:
            b = _split_dot(tri_ref[...], lf)
            yield
        else:
            b = _cumsum_rows(lf)
        a = pre - b
        a_rows = jnp.transpose(a) if L == LANES else None
        return a, b, a_rows

    def zcol(s, j, h):
        c0 = (j * H_AB + h) * HEAD_DIM
        return cur["zc"][seq_rows(s), c0:c0 + HEAD_DIM]

    def mlstm_head(s, h, a, b, a_rows):
        tick = ticker(s, 2 * h, 1 + 2 * (h % 2))
        hs = slice(h * HEAD_DIM, (h + 1) * HEAD_DIM)
        a_col = a[:, h:h + 1]
        b_col = b[:, h:h + 1]
        if a_rows is not None:
            a_row = a_rows[h:h + 1, :]
        else:
            a_row = jnp.sum(jnp.where(eye, a_col, 0.0), axis=0, keepdims=True)
        m0 = m_ref[s, h][:, :1]
        a_caus = jnp.where(causal, a_row, -jnp.inf)
        a_max = jnp.max(a_caus, axis=-1, keepdims=True)
        if L > SUBLANES:
            tick()
            yield
        mx = jnp.maximum(m0, a_max)
        m_t = b_col + mx
        d = jnp.exp(a_caus - mx)
        inter = jnp.broadcast_to(jnp.exp(m0 - mx), (L, LANES))
        q_f32 = zcol(s, 0, h)
        q = _bf(q_f32)
        k = zcol(s, 1, h) * scale
        v = _bf(zcol(s, 2, h))
        c_prev = c_ref[s, h]
        n_prev = n_ref[s, h]
        qk = _dot_nt(q, k)
        qc = _dot(q, c_prev)
        tick()
        yield
        sm_f32 = qk * d
        sm = _bf(sm_f32)
        m_new = m_t[L - 1:L, :]
        b_last = b_col[L - 1:L, :]
        w_col = jnp.exp(a_col + b_last - m_new)
        keep = jnp.exp(b_last + m0 - m_new)
        kw = k * w_col
        if L > SUBLANES:
            nd = jnp.dot(sm, jnp.concatenate([v, ones_bf], axis=1), preferred_element_type=F32)
            qn = _dot_nt(q, jnp.broadcast_to(n_prev, (HEAD_DIM, LANES)))
            kv = _dot_tn(kw, v)
            tick()
            yield
            num = nd[:, :LANES] + inter * qc
            den = nd[:, LANES:] + inter * qn
        else:
            sv = _dot(sm, v)
            kv = _dot_tn(kw, v)
            tick()
            yield
            num = sv + inter * qc
            den = (jnp.sum(sm_f32, axis=-1, keepdims=True)
                   + inter * jnp.sum(q_f32 * n_prev, axis=-1, keepdims=True))
        hv = num / jnp.maximum(jnp.abs(den), jnp.exp(-m_t))
        y = yield from _layer_norm(hv, gna_ref[:, hs], tick)
        assert not tick.pending, "not enough lock-step rounds for the projection pieces"
        mxa_ref[s, :, hs] = (_sigmoid(zcol(s, 3, h)) * y).astype(mxa_ref.dtype)
        return [(c_ref, h, keep * c_prev + kv),
                (n_ref, h, keep * n_prev + jnp.sum(kw, axis=0, keepdims=True)),
                (m_ref, h, jnp.broadcast_to(m_new, (1, LANES)))]

    def ret_head(s, h):
        tick = ticker(s, 2 * h + 1, 3 - 2 * (h % 2))
        hs = slice(h * HEAD_DIM, (h + 1) * HEAD_DIM)
        decay, inner, tail, g_pow = ret_tabs[h]
        q2 = zcol(s, 4, h)
        k2 = zcol(s, 5, h)
        v2 = _bf(zcol(s, 6, h))
        qr = _bf(q2 * cosv + pltpu.roll(q2, HEAD_DIM // 2, 1) * sinv)
        kr = (k2 * cosv + pltpu.roll(k2, HEAD_DIM // 2, 1) * sinv) * scale
        r_prev = r_ref[s, h]
        qk = _dot_nt(qr, kr)
        qs = _dot(qr, r_prev)
        kv = _dot_tn(kr * tail, v2)
        tick()
        yield
        o2 = _dot(qk * decay, v2)
        tick()
        yield
        o2 = o2 + qs * inner
        y2 = yield from _layer_norm(o2, gnb_ref[:, hs], tick)
        assert not tick.pending, "not enough lock-step rounds for the projection pieces"
        gv = zcol(s, 7, h)
        mxb_ref[s, :, hs] = (gv * _sigmoid(gv) * y2).astype(mxb_ref.dtype)
        return [(r_ref, h, g_pow * r_prev + kv)]

    def group(i, carry_):
        seqs = [i * unroll + j for j in range(unroll)]
        prep = _lock_step([gates(s) for s in seqs])
        chains = []
        for s, (a, b, a_rows) in zip(seqs, prep):
            for h in range(H_AB):
                chains.append((s, mlstm_head(s, h, a, b, a_rows)))
        for s in seqs:
            for h in range(H_AB):
                chains.append((s, ret_head(s, h)))
        new = _lock_step([g for _, g in chains])
        for (s, _), new_state in zip(chains, new):
            for ref, h, val in new_state:
                ref[s, h] = val
        return carry_

    def step(zc_ref, zn_ref):
        cur["zc"], cur["zn"] = zc_ref, zn_ref
        lax.fori_loop(0, sb // unroll, lambda i, carry_: group(i, carry_), 0)

    if n_chunks > 1:
        @pl.when(chunk % 2 == 0)
        def _():
            step(za_ref, zb_ref)

        @pl.when(chunk % 2 == 1)
        def _():
            step(zb_ref, za_ref)
    else:
        step(za_ref, None)


def _ab_mixer(x, gnm, w, cos_t, sin_t, bias, lg, gna, gnb, c0, n0, m0, r0, *, chunk, sb, unroll):
    n_seq, seq_len, d = x.shape
    assert n_seq % sb == 0 and seq_len % chunk == 0 and chunk & (chunk - 1) == 0
    n_chunks = seq_len // chunk
    width = w.shape[1]
    hd = H_AB * HEAD_DIM
    bcast = c0.shape[0] == 1 and n_seq > 1
    sb0 = 1 if bcast else sb
    st = (lambda i: 0) if bcast else (lambda i: i)
    full2 = lambda a: pl.BlockSpec(a.shape, lambda i, c: (0, 0))
    mat_in = pl.BlockSpec((sb0, H_AB, HEAD_DIM, HEAD_DIM), lambda i, c: (st(i), 0, 0, 0))
    vec_in = pl.BlockSpec((sb0, H_AB, 1, LANES), lambda i, c: (st(i), 0, 0, 0))
    mat_out = pl.BlockSpec((sb, H_AB, HEAD_DIM, HEAD_DIM), lambda i, c: (i, 0, 0, 0))
    vec_out = pl.BlockSpec((sb, H_AB, 1, LANES), lambda i, c: (i, 0, 0, 0))
    mix_out = pl.BlockSpec((sb, chunk, hd), lambda i, c: (i, c, 0))
    tab = pl.BlockSpec((chunk, LANES), lambda i, c: (c, 0))
    in_specs = [pl.BlockSpec((sb, chunk, d), lambda i, c: (i, c, 0))]
    args = [x]
    if n_chunks > 1:
        in_specs.append(pl.BlockSpec((sb, chunk, d), lambda i, c: (i, jnp.minimum(c + 1, n_chunks - 1), 0)))
        args.append(x)
    w_spec = pl.BlockSpec(w.shape, lambda i, c: (0, 0), pipeline_mode=pl.Buffered(1))
    in_specs += [full2(gnm), w_spec, tab, tab, full2(bias),
                 pl.BlockSpec(lg.shape, lambda i, c: (0, 0, 0)), full2(gna), full2(gnb),
                 mat_in, vec_in, vec_in, mat_in]
    args += [gnm, w, cos_t, sin_t, bias, lg, gna, gnb, c0, n0, m0, r0]
    if chunk > SUBLANES:
        tri = jnp.asarray(np.tril(np.ones((chunk, chunk), np.float32)), BF16)
        in_specs.append(full2(tri))
        args.append(tri)
    scratch = [pltpu.VMEM((sb * chunk, width), F32)]
    if n_chunks > 1:
        scratch += [pltpu.VMEM((sb * chunk, width), F32), pltpu.VMEM((sb * chunk, d), BF16)]
    out_shape = (jax.ShapeDtypeStruct((n_seq, seq_len, hd), BF16),
                 jax.ShapeDtypeStruct((n_seq, seq_len, hd), BF16),
                 jax.ShapeDtypeStruct((n_seq, H_AB, HEAD_DIM, HEAD_DIM), F32),
                 jax.ShapeDtypeStruct((n_seq, H_AB, 1, LANES), F32),
                 jax.ShapeDtypeStruct((n_seq, H_AB, 1, LANES), F32),
                 jax.ShapeDtypeStruct((n_seq, H_AB, HEAD_DIM, HEAD_DIM), F32))
    return pl.pallas_call(
        functools.partial(_ab_kernel, sb=sb, L=chunk, unroll=unroll, n_chunks=n_chunks),
        grid=(n_seq // sb, n_chunks),
        in_specs=in_specs,
        out_specs=(mix_out, mix_out, mat_out, vec_out, vec_out, mat_out),
        out_shape=out_shape,
        scratch_shapes=scratch,
        compiler_params=_params(2),
        name="ab_mixer",
    )(*args)


def _hgrn_level_tables(L):
    t = np.arange(L)[:, None]
    u = np.arange(L)[None, :]
    lvl = np.full((L, L), -1, np.int32)
    size, j = 1, 0
    while size < L:
        upper = ((t // size) % 2) == 1
        lvl[((t // size) == (u // size) + 1) & upper] = j
        size *= 2
        j += 1
    return (u <= t).astype(np.float32), lvl


def _hgrn_small_level_factors(f, r):
    one = jnp.ones_like(f)
    prev = lambda x, d: pltpu.roll(x, d, 1)
    nxt = lambda x, d: pltpu.roll(x, SUBLANES - d, 1)
    f_n1 = nxt(f, 1)
    p1 = f * prev(f, 1)
    sel = lambda idx, *vals: functools.reduce(
        lambda acc, iv: jnp.where(idx == iv[0], iv[1], acc), list(enumerate(vals))[:-1], vals[-1])
    g0 = jnp.where((r & 1) == 1, f, one)
    g1 = sel(r & 3, f_n1, one, f, p1)
    g2 = sel(r, f_n1 * nxt(p1, 3), nxt(p1, 2), f_n1, one, f, p1, p1 * prev(f, 2), p1 * prev(p1, 2))
    return [g0, g1, g2]


def _hgrn_kernel(*refs, sb, L, layer, n_heads, n_chunks, group_heads):
    refs = list(refs)
    xc_ref = refs.pop(0)
    xnext_ref = refs.pop(0) if n_chunks > 1 else None
    gnm_ref, w_ref, lbl_ref, gn_ref, s0_ref = [refs.pop(0) for _ in range(5)]
    if L > SUBLANES:
        tri_ref, lvl_ref = refs.pop(0), refs.pop(0)
    mx_ref, s_ref, za_ref = refs.pop(0), refs.pop(0), refs.pop(0)
    if n_chunks > 1:
        zb_ref, xnb_ref = refs
    chunk = pl.program_id(1)
    rows_all = sb * L
    d_model = xc_ref.shape[-1]

    @pl.when(chunk == 0)
    def _():
        s_ref[...] = jnp.broadcast_to(s0_ref[...], s_ref.shape)
        xn0 = _rms_norm(xc_ref[...].reshape(rows_all, d_model), gnm_ref[...])
        za_ref[...] = _dot(xn0, w_ref[...])

    if n_chunks > 1:
        xnb_ref[...] = _bf(_rms_norm(xnext_ref[...].reshape(rows_all, d_model), gnm_ref[...]))

    logits = lbl_ref[...]
    ex = jnp.exp(logits - jnp.max(logits, axis=0, keepdims=True))
    p = ex / jnp.sum(ex, axis=0, keepdims=True)
    cum = p[0:1, :]
    for r in range(1, layer + 1):
        cum = cum + p[r:r + 1, :]
    lb_all = cum - p[0:1, :]

    row = lax.broadcasted_iota(jnp.int32, (L, LANES), 0)
    n_levels = L.bit_length() - 1
    width = n_heads * HEAD_DIM
    slab_w = 4 * width // (sb * n_heads)
    assert n_chunks == 1 or slab_w % (2 * LANES) == 0

    def unit(s, h, zc_ref, zn_ref, slot):
        rows = pl.ds(s * L, L) if isinstance(s, int) else pl.ds(pl.multiple_of(s * L, L), L)
        hs = slice(h * HEAD_DIM, (h + 1) * HEAD_DIM)
        zblock = lambda j: zc_ref[rows, j * width + h * HEAD_DIM:j * width + (h + 1) * HEAD_DIM]
        pieces = {slot + 1: s * n_heads + h} if zn_ref is not None else {}
        rounds = [0]

        def tick():
            rounds[0] += 1
            slab = pieces.pop(rounds[0], None)
            if slab is not None:
                c0 = slab * slab_w
                zn_ref[:, c0:c0 + slab_w] = jnp.dot(xnb_ref[...], w_ref[:, c0:c0 + slab_w],
                                                    preferred_element_type=F32)

        lb = lb_all[:, hs]
        qv = zblock(0)
        fv = zblock(1)
        iv = zblock(2)
        e = jnp.exp(-jnp.abs(fv))
        rcp = 1.0 / (1.0 + e)
        pos = fv >= 0.0
        sig = jnp.where(pos, rcp, e * rcp)
        sig_neg = jnp.where(pos, e * rcp, rcp)
        f_gate = lb + (1.0 - lb) * sig
        log_f = jnp.log2(f_gate)
        k = (1.0 - lb) * sig_neg
        s_prev = s_ref[s, h]
        diag = jnp.sum(qv * k, axis=-1, keepdims=True)

        if L > SUBLANES:
            b = _split_dot(tri_ref[...], log_f)
            tick()
            yield
            nt = L // SUBLANES
            tile3 = lambda x: x.reshape(nt, SUBLANES, x.shape[-1])
            q3, k3, b3 = tile3(qv), tile3(k), tile3(b)
            r3 = lax.broadcasted_iota(jnp.int32, (1, SUBLANES, LANES), 1)
            small = _hgrn_small_level_factors(tile3(f_gate), r3)
            acc = [jnp.zeros((SUBLANES, L), F32)] * nt
            for j in range(n_levels):
                size = 1 << j
                if size < SUBLANES:
                    xj = (jnp.where((r3 & size) != 0, q3, k3) * small[j]).reshape(L, LANES)
                    up = list(range(nt))
                    a_j = _dot_nt(xj, xj)
                else:
                    s8 = size // SUBLANES
                    up = [i for i in range(nt) if i & s8]
                    mids = {i: jnp.broadcast_to(b3[i, SUBLANES - 1:, :], (SUBLANES, LANES))
                            for i in range(s8 - 1, nt, 2 * s8)}
                    b_mid = jnp.stack([mids[(i // (2 * s8)) * 2 * s8 + s8 - 1] for i in range(nt)])
                    fac = jnp.exp2(-jnp.abs(b3 - b_mid))
                    x3 = jnp.stack([q3[i] if i & s8 else k3[i] for i in range(nt)]) * fac
                    xj = x3.reshape(L, LANES)
                    a_j = _dot_nt(jnp.concatenate([x3[i] for i in up], axis=0), xj)
                tick()
                yield
                for n, i in enumerate(up):
                    keep = lvl_ref[i * SUBLANES:(i + 1) * SUBLANES, :] == j
                    acc[i] = jnp.where(keep, a_j[n * SUBLANES:(n + 1) * SUBLANES, :], acc[i])
            o = diag * iv + _dot(jnp.concatenate(acc, axis=0), iv)
            tick()
            yield
        else:
            b = _cumsum_rows(log_f)
            pair = []
            for j in range(1, L):
                dec = jnp.exp2(jnp.minimum(b - pltpu.roll(b, j, 0), 0.0))
                pair.append(jnp.sum(jnp.where(row >= j, qv * pltpu.roll(k, j, 0) * dec, 0.0),
                                    axis=-1, keepdims=True))
            tick()
            yield
            o = diag * iv
            for j, a in enumerate(pair, start=1):
                o = o + a * pltpu.roll(iv, j, 0)

        o = o + _dot(qv * jnp.exp2(b), s_prev)
        tick()
        yield
        b_last = b[L - 1:L, :]
        e_last = jnp.exp2(b_last)
        e_hi = _bf(e_last).astype(F32)
        sub = lax.broadcasted_iota(jnp.int32, (SUBLANES, LANES), 0)
        e_rows = jnp.where(sub == 0, e_hi, jnp.where(sub == 1, e_last - e_hi, 0.0))
        e_col = _dot_tn(e_rows, jnp.ones((SUBLANES, LANES), BF16))
        s_new = e_col * s_prev + _dot_tn(k * jnp.exp2(b_last - b), iv)
        ms = jnp.mean(o * o, axis=-1, keepdims=True)
        tick()
        yield
        assert not pieces, "not enough lock-step rounds for the projection pieces"
        y = o * lax.rsqrt(ms + EPS) * gn_ref[:, hs]
        gv = zblock(3)
        mx_ref[s, :, hs] = (y * (gv * _sigmoid(gv))).astype(mx_ref.dtype)
        return s_new

    def step(zc_ref, zn_ref):
        def per_seq(s, carry_):
            for h0 in range(0, n_heads, group_heads):
                heads = list(range(h0, h0 + group_heads))
                new = _lock_step([unit(s, h, zc_ref, zn_ref, slot) for slot, h in enumerate(heads)])
                for h, s_new in zip(heads, new):
                    s_ref[s, h] = s_new
            return carry_
        if zn_ref is None and L <= SUBLANES and sb % 2 == 0:
            def seq_pair(i, carry_):
                units = [(2 * i + j, h) for j in range(2) for h in range(n_heads)]
                new = _lock_step([unit(s, h, zc_ref, zn_ref, 0) for s, h in units])
                for (s, h), s_new in zip(units, new):
                    s_ref[s, h] = s_new
                return carry_
            lax.fori_loop(0, sb // 2, seq_pair, 0)
        elif zn_ref is None:
            lax.fori_loop(0, sb, per_seq, 0)
        else:
            for s in range(sb):
                per_seq(s, 0)

    if n_chunks > 1:
        @pl.when(chunk % 2 == 0)
        def _():
            step(za_ref, zb_ref)

        @pl.when(chunk % 2 == 1)
        def _():
            step(zb_ref, za_ref)
    else:
        step(za_ref, None)


def _hgrn_mixer(x, gnm, w, lb_logits, gain, s0, *, chunk, sb, layer, group_heads):
    n_seq, seq_len, d = x.shape
    assert n_seq % sb == 0 and seq_len % chunk == 0 and chunk & (chunk - 1) == 0
    n_heads = s0.shape[1]
    n_chunks = seq_len // chunk
    width = n_heads * HEAD_DIM
    bcast = s0.shape[0] == 1 and n_seq > 1
    sb0 = 1 if bcast else sb
    st = (lambda i: 0) if bcast else (lambda i: i)
    full2 = lambda a: pl.BlockSpec(a.shape, lambda i, c: (0, 0))
    in_specs = [pl.BlockSpec((sb, chunk, d), lambda i, c: (i, c, 0))]
    args = [x]
    if n_chunks > 1:
        in_specs.append(pl.BlockSpec((sb, chunk, d), lambda i, c: (i, jnp.minimum(c + 1, n_chunks - 1), 0)))
        args.append(x)
    w_spec = pl.BlockSpec(w.shape, lambda i, c: (0, 0), pipeline_mode=pl.Buffered(1))
    in_specs += [full2(gnm), w_spec, full2(lb_logits), full2(gain),
                 pl.BlockSpec((sb0, n_heads, HEAD_DIM, HEAD_DIM), lambda i, c: (st(i), 0, 0, 0))]
    args += [gnm, w, lb_logits, gain, s0]
    if chunk > SUBLANES:
        tri, lvl = _hgrn_level_tables(chunk)
        in_specs += [pl.BlockSpec(tri.shape, lambda i, c: (0, 0)), pl.BlockSpec(lvl.shape, lambda i, c: (0, 0))]
        args += [jnp.asarray(tri, BF16), jnp.asarray(lvl)]
    out_specs = (pl.BlockSpec((sb, chunk, width), lambda i, c: (i, c, 0)),
                 pl.BlockSpec((sb, n_heads, HEAD_DIM, HEAD_DIM), lambda i, c: (i, 0, 0, 0)))
    out_shape = (jax.ShapeDtypeStruct((n_seq, seq_len, width), BF16),
                 jax.ShapeDtypeStruct((n_seq, n_heads, HEAD_DIM, HEAD_DIM), F32))
    scratch = [pltpu.VMEM((sb * chunk, w.shape[1]), F32)]
    if n_chunks > 1:
        scratch += [pltpu.VMEM((sb * chunk, w.shape[1]), F32), pltpu.VMEM((sb * chunk, d), BF16)]
    return pl.pallas_call(
        functools.partial(_hgrn_kernel, sb=sb, L=chunk, layer=layer, n_heads=n_heads, n_chunks=n_chunks,
                          group_heads=group_heads),
        grid=(n_seq // sb, n_chunks),
        in_specs=in_specs,
        out_specs=out_specs,
        out_shape=out_shape,
        scratch_shapes=scratch,
        compiler_params=_params(2),
        name="hgrn_mixer",
    )(*args)


def _ffn_kernel(*refs, n_mixed, s_blk, L, d_ff, final_norm):
    x_ref = refs[0]
    mix_refs = refs[1:1 + n_mixed]
    wo_refs = refs[1 + n_mixed:1 + 2 * n_mixed]
    (gn_ref, win_ref, cw_ref, cb_ref, wout_ref, buf_ref) = refs[1 + 2 * n_mixed:7 + 2 * n_mixed]
    pos = 7 + 2 * n_mixed
    if final_norm:
        gfin_ref = refs[pos]
        pos += 1
    xo_ref, bufo_ref, carry_ref = refs[pos:pos + 3]
    tile = pl.program_id(1)
    tm = s_blk * L

    @pl.when(tile == 0)
    def _():
        carry_ref[...] = jnp.broadcast_to(buf_ref[...], carry_ref.shape)

    x1 = x_ref[...]
    for m_ref, w_ref in zip(mix_refs, wo_refs):
        x1 = x1 + jnp.dot(m_ref[...], w_ref[...], preferred_element_type=F32)
    xn = _rms_norm(x1, gn_ref[...])
    ug = _dot(xn, win_ref[...])
    u = ug[:, :d_ff]
    gate = ug[:, d_ff:]

    t = lax.broadcasted_iota(jnp.int32, (tm, 1), 0) & (L - 1)
    if s_blk == 1:
        p0 = carry_ref[0, 0:1, :]
        p1 = carry_ref[0, 1:2, :]
    else:
        cr = carry_ref[...]
        p0 = jnp.broadcast_to(cr[:, 0:1, :], (s_blk, L, d_ff)).reshape(tm, d_ff)
        p1 = jnp.broadcast_to(cr[:, 1:2, :], (s_blk, L, d_ff)).reshape(tm, d_ff)
    prev1 = jnp.where(t == 0, p1, pltpu.roll(u, 1, 0))
    prev2 = jnp.where(t == 0, p0, jnp.where(t == 1, p1, pltpu.roll(u, 2, 0)))
    cw = cw_ref[...]
    conv = cb_ref[...] + (prev2 * cw[0:1, :] + prev1 * cw[1:2, :] + u * cw[2:3, :])
    hid = conv * _sigmoid(conv) * gate
    x2 = x1 + _dot(hid, wout_ref[...])
    if final_norm:
        xo_ref[...] = _rms_norm(x2, gfin_ref[...])
    else:
        xo_ref[...] = x2

    if s_blk == 1:
        carry_ref[0] = u[tm - (CONV_W - 1):tm, :]
    else:
        carry_ref[...] = u.reshape(s_blk, L, d_ff)[:, L - (CONV_W - 1):L, :]
    bufo_ref[...] = carry_ref[...]


def _ffn(x2d, mixed, wo_parts, gn, w_in, cw, cb, w_out, buf, gfin, *, n_seq, seq_len, tm):
    rows, d = x2d.shape
    d_ff = w_out.shape[0]
    if tm <= seq_len:
        s_blk, L = 1, tm
        tiles = seq_len // tm
        grid = (n_seq, tiles)
        rowmap = lambda s, j: (s * tiles + j, 0)
    else:
        s_blk, L = tm // seq_len, seq_len
        grid = (n_seq // s_blk, 1)
        rowmap = lambda s, j: (s, 0)
    assert rows == n_seq * seq_len and rows % tm == 0 and (seq_len % tm == 0 or tm % seq_len == 0)
    assert L & (L - 1) == 0 and L > CONV_W - 1
    if isinstance(buf, tuple):
        buf, layer = buf
        buf_spec = pl.BlockSpec((None, s_blk, CONV_W - 1, d_ff), lambda s, j: (layer, s, 0, 0))
    else:
        bcast = buf.shape[0] == 1 and n_seq > 1
        bufmap = (lambda s, j: (0, 0, 0)) if bcast else (lambda s, j: (s, 0, 0))
        buf_spec = pl.BlockSpec((1 if bcast else s_blk, CONV_W - 1, d_ff), bufmap)
    const = lambda s, j: (0, 0)
    n_mixed = len(mixed)
    in_specs = [pl.BlockSpec((tm, d), rowmap)]
    in_specs += [pl.BlockSpec((tm, m.shape[1]), rowmap) for m in mixed]
    once = lambda a: pl.BlockSpec(a.shape, const, pipeline_mode=pl.Buffered(1))
    in_specs += [once(w) for w in wo_parts]
    in_specs += [pl.BlockSpec((1, d), const),
                 once(w_in),
                 pl.BlockSpec(cw.shape, const),
                 pl.BlockSpec((1, d_ff), const),
                 once(w_out),
                 buf_spec]
    args = [x2d, *mixed, *wo_parts, gn, w_in, cw, cb, w_out, buf]
    if gfin is not None:
        in_specs.append(pl.BlockSpec((1, d), const))
        args.append(gfin)
    out_specs = (pl.BlockSpec((tm, d), rowmap),
                 pl.BlockSpec((s_blk, CONV_W - 1, d_ff), lambda s, j: (s, 0, 0)))
    out_shape = (jax.ShapeDtypeStruct((rows, d), F32),
                 jax.ShapeDtypeStruct((n_seq, CONV_W - 1, d_ff), F32))
    return pl.pallas_call(
        functools.partial(_ffn_kernel, n_mixed=n_mixed, s_blk=s_blk, L=L, d_ff=d_ff,
                          final_norm=gfin is not None),
        grid=grid,
        in_specs=in_specs,
        out_specs=out_specs,
        out_shape=out_shape,
        scratch_shapes=[pltpu.VMEM((s_blk, CONV_W - 1, d_ff), F32)],
        compiler_params=_params(2),
        name="ffn",
    )(*args)


def _tile_rows(n_seq, seq_len, target):
    rows = n_seq * seq_len
    if seq_len >= target:
        return target
    return min(rows, target)


def _trunk(x, pos, st, w, *, chunk_ab, chunk_c, sb_ab, sb_c, unroll_ab, group_c):
    n_seq, seq_len, d = x.shape
    rows = n_seq * seq_len
    tm = _tile_rows(n_seq, seq_len, 512)
    x2d = x.reshape(rows, d)

    half = HEAD_DIM // 2
    inv = 1.0 / (ROPE_BASE ** jnp.linspace(0.0, 1.0, half, dtype=F32))
    ang = pos[:, None] * inv[None, :]
    cos_t = jnp.concatenate([jnp.cos(ang), jnp.cos(ang)], axis=-1)
    sin_t = jnp.concatenate([-jnp.sin(ang), jnp.sin(ang)], axis=-1)

    mxa, mxb, c_new, n_new, m_new, r_new = _ab_mixer(
        x, w["norm_mix"][0], w["w_in_ab"], cos_t, sin_t, w["gate_bias"], w["lg"], w["gn_a"], w["gn_b"],
        st["c"], st["n"], st["m"], st["r"], chunk=chunk_ab, sb=sb_ab, unroll=unroll_ab)
    hd = H_AB * HEAD_DIM
    x2d, buf0 = _ffn(x2d, [mxa.reshape(rows, hd), mxb.reshape(rows, hd)], w["w_out_ab"],
                     w["norm_ffn"][0], w["w_ffn_in"][0], w["conv_w"][0], w["conv_b"][0], w["w_ffn_out"][0],
                     st["conv"][0], None, n_seq=n_seq, seq_len=seq_len, tm=tm)

    mx, s_new = _hgrn_mixer(x2d.reshape(n_seq, seq_len, d), w["norm_mix"][1], w["w_in_c"], w["lb_logits"],
                            w["gn_c"], st["s"], chunk=chunk_c, sb=sb_c, layer=1, group_heads=group_c)
    y2d, buf1 = _ffn(x2d, [mx.reshape(rows, -1)], w["w_out_c"],
                     w["norm_ffn"][1], w["w_ffn_in"][1], w["conv_w"][1], w["conv_b"][1], w["w_ffn_out"][1],
                     st["conv"][1], w["norm_final"], n_seq=n_seq, seq_len=seq_len, tm=tm)
    new_st = {"c": c_new, "n": n_new, "m": m_new, "r": r_new, "s": s_new, "conv": (buf0, buf1)}
    return y2d.reshape(n_seq, seq_len, d), new_st


def _lane_rep(v):
    return jnp.broadcast_to(v[..., None, None], v.shape + (1, LANES))


def kernel(x_prompt, x_sample, state_mlstm_C, state_mlstm_n, state_mlstm_m, state_ret_S, state_hgrn_S,
           state_ffn_conv, meta_tokens, norm_mix, w_in_ab, b_igate, b_fgate, gn_mlstm, gn_ret, w_out_ab,
           lb_logits, w_in_c, gn_hgrn, w_out_c, norm_ffn, w_ffn_in, conv_w, conv_b, w_ffn_out, norm_final):
    bp, seq, d = x_prompt.shape
    bs, dec_seq, _ = x_sample.shape
    hd = H_AB * HEAD_DIM
    n_hc = state_hgrn_S.shape[2]
    d_ff = w_ffn_out.shape[1]
    assert w_in_ab.shape[0] == 1 and w_in_c.shape[0] == 1 and norm_mix.shape[0] == 2

    wab = w_in_ab[0].astype(BF16)
    g0 = 4 * hd
    w_ab = jnp.concatenate([wab[:, :g0], wab[:, g0 + 2 * H_AB:], wab[:, g0:g0 + 2 * H_AB],
                            jnp.zeros((d, LANES - 2 * H_AB), BF16)], axis=1)
    wo_ab = w_out_ab[0].astype(BF16)
    log_gamma = jnp.log1p(-jnp.exp2(-5.0 - jnp.arange(H_AB, dtype=F32)))
    gate_bias = jnp.concatenate([b_igate[0], b_fgate[0], jnp.zeros((LANES - 2 * H_AB,), F32)])[None, :]
    w = {
        "norm_mix": norm_mix[:, None, :],
        "norm_ffn": norm_ffn[:, None, :],
        "norm_final": norm_final[None, :],
        "w_in_ab": w_ab,
        "gate_bias": gate_bias,
        "lg": _lane_rep(log_gamma),
        "gn_a": gn_mlstm[0][None, :],
        "gn_b": gn_ret[0][None, :],
        "w_out_ab": [wo_ab[:hd], wo_ab[hd:]],
        "lb_logits": lb_logits,
        "w_in_c": w_in_c[0].astype(BF16),
        "gn_c": gn_hgrn[0][None, :],
        "w_out_c": [w_out_c[0].astype(BF16)],
        "w_ffn_in": w_ffn_in.astype(BF16),
        "conv_w": conv_w,
        "conv_b": conv_b[:, None, :],
        "w_ffn_out": w_ffn_out.astype(BF16),
    }

    def zero_state(n):
        return {"c": jnp.zeros((n, H_AB, HEAD_DIM, HEAD_DIM), F32),
                "n": jnp.zeros((n, H_AB, 1, LANES), F32),
                "m": jnp.zeros((n, H_AB, 1, LANES), F32),
                "r": jnp.zeros((n, H_AB, HEAD_DIM, HEAD_DIM), F32),
                "s": jnp.zeros((n, n_hc, HEAD_DIM, HEAD_DIM), F32),
                "conv": (jnp.zeros((n, CONV_W - 1, d_ff), F32), jnp.zeros((n, CONV_W - 1, d_ff), F32))}

    _, st_meta = _trunk(meta_tokens[None].astype(F32), jnp.arange(N_META, dtype=F32), zero_state(1), w,
                        chunk_ab=N_META, chunk_c=N_META, sb_ab=1, sb_c=1, unroll_ab=1, group_c=4)
    pos_p = N_META + jnp.arange(seq, dtype=F32)
    y_prompt, st_p = _trunk(x_prompt, pos_p, st_meta, w, chunk_ab=128, chunk_c=128,
                            sb_ab=min(bp, 2), sb_c=min(bp, 2), unroll_ab=min(bp, 2), group_c=n_hc)
    st_s = {"c": state_mlstm_C[0], "n": state_mlstm_n[0][:, :, None, :], "m": _lane_rep(state_mlstm_m[0]),
            "r": state_ret_S[0], "s": state_hgrn_S[0], "conv": ((state_ffn_conv, 0), (state_ffn_conv, 1))}
    pos_s = PAST_LEN + jnp.arange(dec_seq, dtype=F32)
    y_sample, st_s = _trunk(x_sample, pos_s, st_s, w, chunk_ab=dec_seq, chunk_c=dec_seq,
                            sb_ab=min(bs, 16), sb_c=min(bs, 16), unroll_ab=min(bs, 2), group_c=n_hc)

    def outs(s):
        return (s["c"][None], s["n"][:, :, 0, :][None], s["m"][:, :, 0, 0][None], s["r"][None], s["s"][None],
                jnp.stack(s["conv"]))

    cp, n_p, mp, rp, sp, convp = outs(st_p)
    cs, n_s, ms, rs, ss, convs = outs(st_s)
    return (y_prompt, y_sample, cp, cs, n_p, n_s, mp, ms, rp, rs, sp, ss, convp, convs)
```

```python
import functools

import numpy as np

import jax
import jax.numpy as jnp
from jax import lax
from jax.experimental import pallas as pl
from jax.experimental.pallas import tpu as pltpu

EPS = 1e-6
N_META = 16
PAST_LEN = 16384
ROPE_BASE = 10000.0
HEAD_DIM = 128
H_AB = 4
CONV_W = 3
LANES = 128
SUBLANES = 8
VMEM_LIMIT = 56 * 1024 * 1024
F32 = jnp.float32
BF16 = jnp.bfloat16


def _bf(x):
    return x.astype(BF16)


def _dot(a, b):
    return jnp.dot(_bf(a), _bf(b), preferred_element_type=F32)


def _dot_nt(a, b):
    return lax.dot_general(_bf(a), _bf(b), (((1,), (1,)), ((), ())), preferred_element_type=F32)


def _dot_tn(a, b):
    return lax.dot_general(_bf(a), _bf(b), (((0,), (0,)), ((), ())), preferred_element_type=F32)


def _rms_norm(x, gain):
    y = x * lax.rsqrt(jnp.mean(x * x, axis=-1, keepdims=True) + EPS)
    return y * gain


def _sigmoid(x):
    return 1.0 / (1.0 + jnp.exp(-x))


def _layer_norm(x, gain, tick):
    mu = jnp.mean(x, axis=-1, keepdims=True)
    if x.shape[0] > SUBLANES:
        tick()
        yield
    cen = x - mu
    var = jnp.mean(cen * cen, axis=-1, keepdims=True)
    if x.shape[0] > SUBLANES:
        tick()
        yield
    return cen * lax.rsqrt(var + EPS) * gain


def _cumsum_rows(x):
    n = x.shape[0]
    row = lax.broadcasted_iota(jnp.int32, x.shape, 0)
    k = 1
    while k < n:
        x = x + jnp.where(row >= k, pltpu.roll(x, k, 0), 0.0)
        k *= 2
    return x


def _split_dot(mat_bf16, x):
    hi = _bf(x)
    lo = _bf(x - hi.astype(F32))
    n = x.shape[1]
    r = jnp.dot(mat_bf16, jnp.concatenate([hi, lo], axis=1), preferred_element_type=F32)
    return r[:, :n] + r[:, n:]


def _lock_step(gens):
    results = [None] * len(gens)
    live = list(enumerate(gens))
    while live:
        still = []
        for idx, g in live:
            try:
                next(g)
                still.append((idx, g))
            except StopIteration as stop:
                results[idx] = stop.value
        live = still
    return results


def _params(n_grid):
    return pltpu.CompilerParams(dimension_semantics=("arbitrary",) * n_grid,
                                vmem_limit_bytes=VMEM_LIMIT)


def _ab_kernel(*refs, sb, L, unroll, n_chunks):
    assert sb % unroll == 0
    refs = list(refs)
    xc_ref = refs.pop(0)
    xnext_ref = refs.pop(0) if n_chunks > 1 else None
    (gnm_ref, w_ref, cos_ref, sin_ref, bias_ref, lg_ref, gna_ref, gnb_ref,
     c0_ref, n0_ref, m0_ref, r0_ref) = [refs.pop(0) for _ in range(12)]
    if L > SUBLANES:
        tri_ref = refs.pop(0)
    mxa_ref, mxb_ref, c_ref, n_ref, m_ref, r_ref, za_ref = [refs.pop(0) for _ in range(7)]
    if n_chunks > 1:
        zb_ref, xnb_ref = refs
    chunk = pl.program_id(1)
    rows_all = sb * L
    d_model = xc_ref.shape[-1]

    @pl.when(chunk == 0)
    def _():
        c_ref[...] = jnp.broadcast_to(c0_ref[...], c_ref.shape)
        n_ref[...] = jnp.broadcast_to(n0_ref[...], n_ref.shape)
        m_ref[...] = jnp.broadcast_to(m0_ref[...], m_ref.shape)
        r_ref[...] = jnp.broadcast_to(r0_ref[...], r_ref.shape)
        xn0 = _rms_norm(xc_ref[...].reshape(rows_all, d_model), gnm_ref[...])
        za_ref[...] = _dot(xn0, w_ref[...])

    if n_chunks > 1:
        xnb_ref[...] = _bf(_rms_norm(xnext_ref[...].reshape(rows_all, d_model), gnm_ref[...]))
    cur = {}

    def seq_rows(s):
        return pl.ds(pl.multiple_of(s * L, L), L)

    def project_next(s, c0, c1):
        if cur["zn"] is not None:
            rows = seq_rows(s)
            cur["zn"][rows, c0:c1] = jnp.dot(xnb_ref[rows, :], w_ref[:, c0:c1], preferred_element_type=F32)

    def ticker(s, slab, first_round):
        half_w = 2 * HEAD_DIM
        pieces = {first_round: 0, first_round + 1: 1} if cur["zn"] is not None else {}
        rounds = [0]

        def tick():
            rounds[0] += 1
            half = pieces.pop(rounds[0], None)
            if half is not None:
                c0 = (2 * slab + half) * half_w
                project_next(s, c0, c0 + half_w)

        tick.pending = pieces
        return tick

    scale = HEAD_DIM ** -0.5
    row = lax.broadcasted_iota(jnp.int32, (L, L), 0)
    col = lax.broadcasted_iota(jnp.int32, (L, L), 1)
    causal = row >= col
    eye = row == col
    rel = jnp.where(causal, row - col, 0).astype(F32)
    tcol = lax.broadcasted_iota(jnp.int32, (L, 1), 0).astype(F32)
    cosv = cos_ref[...]
    sinv = sin_ref[...]
    bias = bias_ref[...]
    gate_col = 8 * H_AB * HEAD_DIM
    ones_bf = jnp.ones((L, LANES), BF16)
    ret_tabs = []
    for h in range(H_AB):
        lg = lg_ref[h][:, :1]
        ret_tabs.append((jnp.where(causal, jnp.exp(rel * lg), 0.0),
                         jnp.broadcast_to(jnp.exp((tcol + 1.0) * lg), (L, LANES)),
                         jnp.broadcast_to(jnp.exp((L - 1.0 - tcol) * lg), (L, LANES)),
                         jnp.exp(L * lg)))

    def gates(s):
        project_next(s, gate_col, gate_col + LANES)
        pre = cur["zc"][seq_rows(s), gate_col:gate_col + LANES] + bias
        lf = jnp.minimum(pre, 0.0) - jnp.log1p(jnp.exp(-jnp.abs(pre)))
        lf = pltpu.roll(lf, LANES - H_AB, 1)
        if L > SUBLANES:
            b = _split_dot(tri_ref[...], lf)
            yield
        else:
            b = _cumsum_rows(lf)
        a = pre - b
        a_rows = jnp.transpose(a) if L == LANES else None
        return a, b, a_rows

    def zcol(s, j, h):
        c0 = (j * H_AB + h) * HEAD_DIM
        return cur["zc"][seq_rows(s), c0:c0 + HEAD_DIM]

    def mlstm_head(s, h, a, b, a_rows):
        tick = ticker(s, 2 * h, 1 + 2 * (h % 2))
        hs = slice(h * HEAD_DIM, (h + 1) * HEAD_DIM)
        a_col = a[:, h:h + 1]
        b_col = b[:, h:h + 1]
        if a_rows is not None:
            a_row = a_rows[h:h + 1, :]
        else:
            a_row = jnp.sum(jnp.where(eye, a_col, 0.0), axis=0, keepdims=True)
        m0 = m_ref[s, h][:, :1]
        a_caus = jnp.where(causal, a_row, -jnp.inf)
        a_max = jnp.max(a_caus, axis=-1, keepdims=True)
        if L > SUBLANES:
            tick()
            yield
        mx = jnp.maximum(m0, a_max)
        m_t = b_col + mx
        d = jnp.exp(a_caus - mx)
        inter = jnp.broadcast_to(jnp.exp(m0 - mx), (L, LANES))
        q_f32 = zcol(s, 0, h)
        q = _bf(q_f32)
        k = zcol(s, 1, h) * scale
        v = _bf(zcol(s, 2, h))
        c_prev = c_ref[s, h]
        n_prev = n_ref[s, h]
        qk = _dot_nt(q, k)
        qc = _dot(q, c_prev)
        tick()
        yield
        sm_f32 = qk * d
        sm = _bf(sm_f32)
        m_new = m_t[L - 1:L, :]
        b_last = b_col[L - 1:L, :]
        w_col = jnp.exp(a_col + b_last - m_new)
        keep = jnp.exp(b_last + m0 - m_new)
        kw = k * w_col
        if L > SUBLANES:
            nd = jnp.dot(sm, jnp.concatenate([v, ones_bf], axis=1), preferred_element_type=F32)
            qn = _dot_nt(q, jnp.broadcast_to(n_prev, (HEAD_DIM, LANES)))
            kv = _dot_tn(kw, v)
            tick()
            yield
            num = nd[:, :LANES] + inter * qc
            den = nd[:, LANES:] + inter * qn
        else:
            sv = _dot(sm, v)
            kv = _dot_tn(kw, v)
            tick()
            yield
            num = sv + inter * qc
            den = (jnp.sum(sm_f32, axis=-1, keepdims=True)
                   + inter * jnp.sum(q_f32 * n_prev, axis=-1, keepdims=True))
        hv = num / jnp.maximum(jnp.abs(den), jnp.exp(-m_t))
        y = yield from _layer_norm(hv, gna_ref[:, hs], tick)
        assert not tick.pending, "not enough lock-step rounds for the projection pieces"
        mxa_ref[s, :, hs] = (_sigmoid(zcol(s, 3, h)) * y).astype(mxa_ref.dtype)
        return [(c_ref, h, keep * c_prev + kv),
                (n_ref, h, keep * n_prev + jnp.sum(kw, axis=0, keepdims=True)),
                (m_ref, h, jnp.broadcast_to(m_new, (1, LANES)))]

    def ret_head(s, h):
        tick = ticker(s, 2 * h + 1, 3 - 2 * (h % 2))
        hs = slice(h * HEAD_DIM, (h + 1) * HEAD_DIM)
        decay, inner, tail, g_pow = ret_tabs[h]
        q2 = zcol(s, 4, h)
        k2 = zcol(s, 5, h)
        v2 = _bf(zcol(s, 6, h))
        qr = _bf(q2 * cosv + pltpu.roll(q2, HEAD_DIM // 2, 1) * sinv)
        kr = (k2 * cosv + pltpu.roll(k2, HEAD_DIM // 2, 1) * sinv) * scale
        r_prev = r_ref[s, h]
        qk = _dot_nt(qr, kr)
        qs = _dot(qr, r_prev)
        kv = _dot_tn(kr * tail, v2)
        tick()
        yield
        o2 = _dot(qk * decay, v2)
        tick()
        yield
        o2 = o2 + qs * inner
        y2 = yield from _layer_norm(o2, gnb_ref[:, hs], tick)
        assert not tick.pending, "not enough lock-step rounds for the projection pieces"
        gv = zcol(s, 7, h)
        mxb_ref[s, :, hs] = (gv * _sigmoid(gv) * y2).astype(mxb_ref.dtype)
        return [(r_ref, h, g_pow * r_prev + kv)]

    def group(i, carry_):
        seqs = [i * unroll + j for j in range(unroll)]
        prep = _lock_step([gates(s) for s in seqs])
        chains = []
        for s, (a, b, a_rows) in zip(seqs, prep):
            for h in range(H_AB):
                chains.append((s, mlstm_head(s, h, a, b, a_rows)))
        for s in seqs:
            for h in range(H_AB):
                chains.append((s, ret_head(s, h)))
        new = _lock_step([g for _, g in chains])
        for (s, _), new_state in zip(chains, new):
            for ref, h, val in new_state:
                ref[s, h] = val
        return carry_

    def step(zc_ref, zn_ref):
        cur["zc"], cur["zn"] = zc_ref, zn_ref
        lax.fori_loop(0, sb // unroll, lambda i, carry_: group(i, carry_), 0)

    if n_chunks > 1:
        @pl.when(chunk % 2 == 0)
        def _():
            step(za_ref, zb_ref)

        @pl.when(chunk % 2 == 1)
        def _():
            step(zb_ref, za_ref)
    else:
        step(za_ref, None)


def _ab_mixer(x, gnm, w, cos_t, sin_t, bias, lg, gna, gnb, c0, n0, m0, r0, *, chunk, sb, unroll):
    n_seq, seq_len, d = x.shape
    assert n_seq % sb == 0 and seq_len % chunk == 0 and chunk & (chunk - 1) == 0
    n_chunks = seq_len // chunk
    width = w.shape[1]
    hd = H_AB * HEAD_DIM
    bcast = c0.shape[0] == 1 and n_seq > 1
    sb0 = 1 if bcast else sb
    st = (lambda i: 0) if bcast else (lambda i: i)
    full2 = lambda a: pl.BlockSpec(a.shape, lambda i, c: (0, 0))
    mat_in = pl.BlockSpec((sb0, H_AB, HEAD_DIM, HEAD_DIM), lambda i, c: (st(i), 0, 0, 0))
    vec_in = pl.BlockSpec((sb0, H_AB, 1, LANES), lambda i, c: (st(i), 0, 0, 0))
    mat_out = pl.BlockSpec((sb, H_AB, HEAD_DIM, HEAD_DIM), lambda i, c: (i, 0, 0, 0))
    vec_out = pl.BlockSpec((sb, H_AB, 1, LANES), lambda i, c: (i, 0, 0, 0))
    mix_out = pl.BlockSpec((sb, chunk, hd), lambda i, c: (i, c, 0))
    tab = pl.BlockSpec((chunk, LANES), lambda i, c: (c, 0))
    in_specs = [pl.BlockSpec((sb, chunk, d), lambda i, c: (i, c, 0))]
    args = [x]
    if n_chunks > 1:
        in_specs.append(pl.BlockSpec((sb, chunk, d), lambda i, c: (i, jnp.minimum(c + 1, n_chunks - 1), 0)))
        args.append(x)
    w_spec = pl.BlockSpec(w.shape, lambda i, c: (0, 0), pipeline_mode=pl.Buffered(1))
    in_specs += [full2(gnm), w_spec, tab, tab, full2(bias),
                 pl.BlockSpec(lg.shape, lambda i, c: (0, 0, 0)), full2(gna), full2(gnb),
                 mat_in, vec_in, vec_in, mat_in]
    args += [gnm, w, cos_t, sin_t, bias, lg, gna, gnb, c0, n0, m0, r0]
    if chunk > SUBLANES:
        tri = jnp.asarray(np.tril(np.ones((chunk, chunk), np.float32)), BF16)
        in_specs.append(full2(tri))
        args.append(tri)
    scratch = [pltpu.VMEM((sb * chunk, width), F32)]
    if n_chunks > 1:
        scratch += [pltpu.VMEM((sb * chunk, width), F32), pltpu.VMEM((sb * chunk, d), BF16)]
    out_shape = (jax.ShapeDtypeStruct((n_seq, seq_len, hd), BF16),
                 jax.ShapeDtypeStruct((n_seq, seq_len, hd), BF16),
                 jax.ShapeDtypeStruct((n_seq, H_AB, HEAD_DIM, HEAD_DIM), F32),
                 jax.ShapeDtypeStruct((n_seq, H_AB, 1, LANES), F32),
                 jax.ShapeDtypeStruct((n_seq, H_AB, 1, LANES), F32),
                 jax.ShapeDtypeStruct((n_seq, H_AB, HEAD_DIM, HEAD_DIM), F32))
    return pl.pallas_call(
        functools.partial(_ab_kernel, sb=sb, L=chunk, unroll=unroll, n_chunks=n_chunks),
        grid=(n_seq // sb, n_chunks),
        in_specs=in_specs,
        out_specs=(mix_out, mix_out, mat_out, vec_out, vec_out, mat_out),
        out_shape=out_shape,
        scratch_shapes=scratch,
        compiler_params=_params(2),
        name="ab_mixer",
    )(*args)


def _hgrn_level_tables(L):
    t = np.arange(L)[:, None]
    u = np.arange(L)[None, :]
    lvl = np.full((L, L), -1, np.int32)
    size, j = 1, 0
    while size < L:
        upper = ((t // size) % 2) == 1
        lvl[((t // size) == (u // size) + 1) & upper] = j
        size *= 2
        j += 1
    return (u <= t).astype(np.float32), lvl


def _hgrn_small_level_factors(f, r):
    one = jnp.ones_like(f)
    prev = lambda x, d: pltpu.roll(x, d, 1)
    nxt = lambda x, d: pltpu.roll(x, SUBLANES - d, 1)
    f_n1 = nxt(f, 1)
    p1 = f * prev(f, 1)
    sel = lambda idx, *vals: functools.reduce(
        lambda acc, iv: jnp.where(idx == iv[0], iv[1], acc), list(enumerate(vals))[:-1], vals[-1])
    g0 = jnp.where((r & 1) == 1, f, one)
    g1 = sel(r & 3, f_n1, one, f, p1)
    g2 = sel(r, f_n1 * nxt(p1, 3), nxt(p1, 2), f_n1, one, f, p1, p1 * prev(f, 2), p1 * prev(p1, 2))
    return [g0, g1, g2]


def _hgrn_kernel(*refs, sb, L, layer, n_heads, n_chunks, group_heads):
    refs = list(refs)
    xc_ref = refs.pop(0)
    xnext_ref = refs.pop(0) if n_chunks > 1 else None
    gnm_ref, w_ref, lbl_ref, gn_ref, s0_ref = [refs.pop(0) for _ in range(5)]
    if L > SUBLANES:
        tri_ref, lvl_ref = refs.pop(0), refs.pop(0)
    mx_ref, s_ref, za_ref = refs.pop(0), refs.pop(0), refs.pop(0)
    if n_chunks > 1:
        zb_ref, xnb_ref = refs
    chunk = pl.program_id(1)
    rows_all = sb * L
    d_model = xc_ref.shape[-1]

    @pl.when(chunk == 0)
    def _():
        s_ref[...] = jnp.broadcast_to(s0_ref[...], s_ref.shape)
        xn0 = _rms_norm(xc_ref[...].reshape(rows_all, d_model), gnm_ref[...])
        za_ref[...] = _dot(xn0, w_ref[...])

    if n_chunks > 1:
        xnb_ref[...] = _bf(_rms_norm(xnext_ref[...].reshape(rows_all, d_model), gnm_ref[...]))

    logits = lbl_ref[...]
    ex = jnp.exp(logits - jnp.max(logits, axis=0, keepdims=True))
    p = ex / jnp.sum(ex, axis=0, keepdims=True)
    cum = p[0:1, :]
    for r in range(1, layer + 1):
        cum = cum + p[r:r + 1, :]
    lb_all = cum - p[0:1, :]

    row = lax.broadcasted_iota(jnp.int32, (L, LANES), 0)
    n_levels = L.bit_length() - 1
    width = n_heads * HEAD_DIM
    slab_w = 4 * width // (sb * n_heads)
    assert n_chunks == 1 or slab_w % (2 * LANES) == 0

    def unit(s, h, zc_ref, zn_ref, slot):
        rows = pl.ds(s * L, L) if isinstance(s, int) else pl.ds(pl.multiple_of(s * L, L), L)
        hs = slice(h * HEAD_DIM, (h + 1) * HEAD_DIM)
        zblock = lambda j: zc_ref[rows, j * width + h * HEAD_DIM:j * width + (h + 1) * HEAD_DIM]
        pieces = {slot + 1: s * n_heads + h} if zn_ref is not None else {}
        rounds = [0]

        def tick():
            rounds[0] += 1
            slab = pieces.pop(rounds[0], None)
            if slab is not None:
                c0 = slab * slab_w
                zn_ref[:, c0:c0 + slab_w] = jnp.dot(xnb_ref[...], w_ref[:, c0:c0 + slab_w],
                                                    preferred_element_type=F32)

        lb = lb_all[:, hs]
        qv = zblock(0)
        fv = zblock(1)
        iv = zblock(2)
        e = jnp.exp(-jnp.abs(fv))
        rcp = 1.0 / (1.0 + e)
        pos = fv >= 0.0
        sig = jnp.where(pos, rcp, e * rcp)
        sig_neg = jnp.where(pos, e * rcp, rcp)
        f_gate = lb + (1.0 - lb) * sig
        log_f = jnp.log2(f_gate)
        k = (1.0 - lb) * sig_neg
        s_prev = s_ref[s, h]
        diag = jnp.sum(qv * k, axis=-1, keepdims=True)

        if L > SUBLANES:
            b = _split_dot(tri_ref[...], log_f)
            tick()
            yield
            nt = L // SUBLANES
            tile3 = lambda x: x.reshape(nt, SUBLANES, x.shape[-1])
            q3, k3, b3 = tile3(qv), tile3(k), tile3(b)
            r3 = lax.broadcasted_iota(jnp.int32, (1, SUBLANES, LANES), 1)
            small = _hgrn_small_level_factors(tile3(f_gate), r3)
            acc = [jnp.zeros((SUBLANES, L), F32)] * nt
            for j in range(n_levels):
                size = 1 << j
                if size < SUBLANES:
                    xj = (jnp.where((r3 & size) != 0, q3, k3) * small[j]).reshape(L, LANES)
                    up = list(range(nt))
                    a_j = _dot_nt(xj, xj)
                else:
                    s8 = size // SUBLANES
                    up = [i for i in range(nt) if i & s8]
                    mids = {i: jnp.broadcast_to(b3[i, SUBLANES - 1:, :], (SUBLANES, LANES))
                            for i in range(s8 - 1, nt, 2 * s8)}
                    b_mid = jnp.stack([mids[(i // (2 * s8)) * 2 * s8 + s8 - 1] for i in range(nt)])
                    fac = jnp.exp2(-jnp.abs(b3 - b_mid))
                    x3 = jnp.stack([q3[i] if i & s8 else k3[i] for i in range(nt)]) * fac
                    xj = x3.reshape(L, LANES)
                    a_j = _dot_nt(jnp.concatenate([x3[i] for i in up], axis=0), xj)
                tick()
                yield
                for n, i in enumerate(up):
                    keep = lvl_ref[i * SUBLANES:(i + 1) * SUBLANES, :] == j
                    acc[i] = jnp.where(keep, a_j[n * SUBLANES:(n + 1) * SUBLANES, :], acc[i])
            o = diag * iv + _dot(jnp.concatenate(acc, axis=0), iv)
            tick()
            yield
        else:
            b = _cumsum_rows(log_f)
            pair = []
            for j in range(1, L):
                dec = jnp.exp2(jnp.minimum(b - pltpu.roll(b, j, 0), 0.0))
                pair.append(jnp.sum(jnp.where(row >= j, qv * pltpu.roll(k, j, 0) * dec, 0.0),
                                    axis=-1, keepdims=True))
            tick()
            yield
            o = diag * iv
            for j, a in enumerate(pair, start=1):
                o = o + a * pltpu.roll(iv, j, 0)

        o = o + _dot(qv * jnp.exp2(b), s_prev)
        tick()
        yield
        b_last = b[L - 1:L, :]
        e_last = jnp.exp2(b_last)
        e_hi = _bf(e_last).astype(F32)
        sub = lax.broadcasted_iota(jnp.int32, (SUBLANES, LANES), 0)
        e_rows = jnp.where(sub == 0, e_hi, jnp.where(sub == 1, e_last - e_hi, 0.0))
        e_col = _dot_tn(e_rows, jnp.ones((SUBLANES, LANES), BF16))
        s_new = e_col * s_prev + _dot_tn(k * jnp.exp2(b_last - b), iv)
        ms = jnp.mean(o * o, axis=-1, keepdims=True)
        tick()
        yield
        assert not pieces, "not enough lock-step rounds for the projection pieces"
        y = o * lax.rsqrt(ms + EPS) * gn_ref[:, hs]
        gv = zblock(3)
        mx_ref[s, :, hs] = (y * (gv * _sigmoid(gv))).astype(mx_ref.dtype)
        return s_new

    def step(zc_ref, zn_ref):
        def per_seq(s, carry_):
            for h0 in range(0, n_heads, group_heads):
                heads = list(range(h0, h0 + group_heads))
                new = _lock_step([unit(s, h, zc_ref, zn_ref, slot) for slot, h in enumerate(heads)])
                for h, s_new in zip(heads, new):
                    s_ref[s, h] = s_new
            return carry_
        if zn_ref is None and L <= SUBLANES and sb % 4 == 0:
            def seq_group(i, carry_):
                units = [(4 * i + j, h) for j in range(4) for h in range(n_heads)]
                new = _lock_step([unit(s, h, zc_ref, zn_ref, 0) for s, h in units])
                for (s, h), s_new in zip(units, new):
                    s_ref[s, h] = s_new
                return carry_
            lax.fori_loop(0, sb // 4, seq_group, 0)
        elif zn_ref is None:
            lax.fori_loop(0, sb, per_seq, 0)
        else:
            for s in range(sb):
                per_seq(s, 0)

    if n_chunks > 1:
        @pl.when(chunk % 2 == 0)
        def _():
            step(za_ref, zb_ref)

        @pl.when(chunk % 2 == 1)
        def _():
            step(zb_ref, za_ref)
    else:
        step(za_ref, None)


def _hgrn_mixer(x, gnm, w, lb_logits, gain, s0, *, chunk, sb, layer, group_heads):
    n_seq, seq_len, d = x.shape
    assert n_seq % sb == 0 and seq_len % chunk == 0 and chunk & (chunk - 1) == 0
    n_heads = s0.shape[1]
    n_chunks = seq_len // chunk
    width = n_heads * HEAD_DIM
    bcast = s0.shape[0] == 1 and n_seq > 1
    sb0 = 1 if bcast else sb
    st = (lambda i: 0) if bcast else (lambda i: i)
    full2 = lambda a: pl.BlockSpec(a.shape, lambda i, c: (0, 0))
    in_specs = [pl.BlockSpec((sb, chunk, d), lambda i, c: (i, c, 0))]
    args = [x]
    if n_chunks > 1:
        in_specs.append(pl.BlockSpec((sb, chunk, d), lambda i, c: (i, jnp.minimum(c + 1, n_chunks - 1), 0)))
        args.append(x)
    w_spec = pl.BlockSpec(w.shape, lambda i, c: (0, 0), pipeline_mode=pl.Buffered(1))
    in_specs += [full2(gnm), w_spec, full2(lb_logits), full2(gain),
                 pl.BlockSpec((sb0, n_heads, HEAD_DIM, HEAD_DIM), lambda i, c: (st(i), 0, 0, 0))]
    args += [gnm, w, lb_logits, gain, s0]
    if chunk > SUBLANES:
        tri, lvl = _hgrn_level_tables(chunk)
        in_specs += [pl.BlockSpec(tri.shape, lambda i, c: (0, 0)), pl.BlockSpec(lvl.shape, lambda i, c: (0, 0))]
        args += [jnp.asarray(tri, BF16), jnp.asarray(lvl)]
    out_specs = (pl.BlockSpec((sb, chunk, width), lambda i, c: (i, c, 0)),
                 pl.BlockSpec((sb, n_heads, HEAD_DIM, HEAD_DIM), lambda i, c: (i, 0, 0, 0)))
    out_shape = (jax.ShapeDtypeStruct((n_seq, seq_len, width), BF16),
                 jax.ShapeDtypeStruct((n_seq, n_heads, HEAD_DIM, HEAD_DIM), F32))
    scratch = [pltpu.VMEM((sb * chunk, w.shape[1]), F32)]
    if n_chunks > 1:
        scratch += [pltpu.VMEM((sb * chunk, w.shape[1]), F32), pltpu.VMEM((sb * chunk, d), BF16)]
    return pl.pallas_call(
        functools.partial(_hgrn_kernel, sb=sb, L=chunk, layer=layer, n_heads=n_heads, n_chunks=n_chunks,
                          group_heads=group_heads),
        grid=(n_seq // sb, n_chunks),
        in_specs=in_specs,
        out_specs=out_specs,
        out_shape=out_shape,
        scratch_shapes=scratch,
        compiler_params=_params(2),
        name="hgrn_mixer",
    )(*args)


def _ffn_kernel(*refs, n_mixed, s_blk, L, d_ff, final_norm):
    x_ref = refs[0]
    mix_refs = refs[1:1 + n_mixed]
    wo_refs = refs[1 + n_mixed:1 + 2 * n_mixed]
    (gn_ref, win_ref, cw_ref, cb_ref, wout_ref, buf_ref) = refs[1 + 2 * n_mixed:7 + 2 * n_mixed]
    pos = 7 + 2 * n_mixed
    if final_norm:
        gfin_ref = refs[pos]
        pos += 1
    xo_ref, bufo_ref, carry_ref = refs[pos:pos + 3]
    tile = pl.program_id(1)
    tm = s_blk * L

    @pl.when(tile == 0)
    def _():
        carry_ref[...] = jnp.broadcast_to(buf_ref[...], carry_ref.shape)

    x1 = x_ref[...]
    for m_ref, w_ref in zip(mix_refs, wo_refs):
        x1 = x1 + jnp.dot(m_ref[...], w_ref[...], preferred_element_type=F32)
    xn = _rms_norm(x1, gn_ref[...])
    ug = _dot(xn, win_ref[...])
    u = ug[:, :d_ff]
    gate = ug[:, d_ff:]

    t = lax.broadcasted_iota(jnp.int32, (tm, 1), 0) & (L - 1)
    if s_blk == 1:
        p0 = carry_ref[0, 0:1, :]
        p1 = carry_ref[0, 1:2, :]
    else:
        cr = carry_ref[...]
        p0 = jnp.broadcast_to(cr[:, 0:1, :], (s_blk, L, d_ff)).reshape(tm, d_ff)
        p1 = jnp.broadcast_to(cr[:, 1:2, :], (s_blk, L, d_ff)).reshape(tm, d_ff)
    prev1 = jnp.where(t == 0, p1, pltpu.roll(u, 1, 0))
    prev2 = jnp.where(t == 0, p0, jnp.where(t == 1, p1, pltpu.roll(u, 2, 0)))
    cw = cw_ref[...]
    conv = cb_ref[...] + (prev2 * cw[0:1, :] + prev1 * cw[1:2, :] + u * cw[2:3, :])
    hid = conv * _sigmoid(conv) * gate
    x2 = x1 + _dot(hid, wout_ref[...])
    if final_norm:
        xo_ref[...] = _rms_norm(x2, gfin_ref[...])
    else:
        xo_ref[...] = x2

    if s_blk == 1:
        carry_ref[0] = u[tm - (CONV_W - 1):tm, :]
    else:
        carry_ref[...] = u.reshape(s_blk, L, d_ff)[:, L - (CONV_W - 1):L, :]
    bufo_ref[...] = carry_ref[...]


def _ffn(x2d, mixed, wo_parts, gn, w_in, cw, cb, w_out, buf, gfin, *, n_seq, seq_len, tm):
    rows, d = x2d.shape
    d_ff = w_out.shape[0]
    if tm <= seq_len:
        s_blk, L = 1, tm
        tiles = seq_len // tm
        grid = (n_seq, tiles)
        rowmap = lambda s, j: (s * tiles + j, 0)
    else:
        s_blk, L = tm // seq_len, seq_len
        grid = (n_seq // s_blk, 1)
        rowmap = lambda s, j: (s, 0)
    assert rows == n_seq * seq_len and rows % tm == 0 and (seq_len % tm == 0 or tm % seq_len == 0)
    assert L & (L - 1) == 0 and L > CONV_W - 1
    if isinstance(buf, tuple):
        buf, layer = buf
        buf_spec = pl.BlockSpec((None, s_blk, CONV_W - 1, d_ff), lambda s, j: (layer, s, 0, 0))
    else:
        bcast = buf.shape[0] == 1 and n_seq > 1
        bufmap = (lambda s, j: (0, 0, 0)) if bcast else (lambda s, j: (s, 0, 0))
        buf_spec = pl.BlockSpec((1 if bcast else s_blk, CONV_W - 1, d_ff), bufmap)
    const = lambda s, j: (0, 0)
    n_mixed = len(mixed)
    in_specs = [pl.BlockSpec((tm, d), rowmap)]
    in_specs += [pl.BlockSpec((tm, m.shape[1]), rowmap) for m in mixed]
    once = lambda a: pl.BlockSpec(a.shape, const, pipeline_mode=pl.Buffered(1))
    in_specs += [once(w) for w in wo_parts]
    in_specs += [pl.BlockSpec((1, d), const),
                 once(w_in),
                 pl.BlockSpec(cw.shape, const),
                 pl.BlockSpec((1, d_ff), const),
                 once(w_out),
                 buf_spec]
    args = [x2d, *mixed, *wo_parts, gn, w_in, cw, cb, w_out, buf]
    if gfin is not None:
        in_specs.append(pl.BlockSpec((1, d), const))
        args.append(gfin)
    out_specs = (pl.BlockSpec((tm, d), rowmap),
                 pl.BlockSpec((s_blk, CONV_W - 1, d_ff), lambda s, j: (s, 0, 0)))
    out_shape = (jax.ShapeDtypeStruct((rows, d), F32),
                 jax.ShapeDtypeStruct((n_seq, CONV_W - 1, d_ff), F32))
    return pl.pallas_call(
        functools.partial(_ffn_kernel, n_mixed=n_mixed, s_blk=s_blk, L=L, d_ff=d_ff,
                          final_norm=gfin is not None),
        grid=grid,
        in_specs=in_specs,
        out_specs=out_specs,
        out_shape=out_shape,
        scratch_shapes=[pltpu.VMEM((s_blk, CONV_W - 1, d_ff), F32)],
        compiler_params=_params(2),
        name="ffn",
    )(*args)


def _tile_rows(n_seq, seq_len, target):
    rows = n_seq * seq_len
    if seq_len >= target:
        return target
    return min(rows, target)


def _trunk(x, pos, st, w, *, chunk_ab, chunk_c, sb_ab, sb_c, unroll_ab, group_c):
    n_seq, seq_len, d = x.shape
    rows = n_seq * seq_len
    tm = _tile_rows(n_seq, seq_len, 512)
    x2d = x.reshape(rows, d)

    half = HEAD_DIM // 2
    inv = 1.0 / (ROPE_BASE ** jnp.linspace(0.0, 1.0, half, dtype=F32))
    ang = pos[:, None] * inv[None, :]
    cos_t = jnp.concatenate([jnp.cos(ang), jnp.cos(ang)], axis=-1)
    sin_t = jnp.concatenate([-jnp.sin(ang), jnp.sin(ang)], axis=-1)

    mxa, mxb, c_new, n_new, m_new, r_new = _ab_mixer(
        x, w["norm_mix"][0], w["w_in_ab"], cos_t, sin_t, w["gate_bias"], w["lg"], w["gn_a"], w["gn_b"],
        st["c"], st["n"], st["m"], st["r"], chunk=chunk_ab, sb=sb_ab, unroll=unroll_ab)
    hd = H_AB * HEAD_DIM
    x2d, buf0 = _ffn(x2d, [mxa.reshape(rows, hd), mxb.reshape(rows, hd)], w["w_out_ab"],
                     w["norm_ffn"][0], w["w_ffn_in"][0], w["conv_w"][0], w["conv_b"][0], w["w_ffn_out"][0],
                     st["conv"][0], None, n_seq=n_seq, seq_len=seq_len, tm=tm)

    mx, s_new = _hgrn_mixer(x2d.reshape(n_seq, seq_len, d), w["norm_mix"][1], w["w_in_c"], w["lb_logits"],
                            w["gn_c"], st["s"], chunk=chunk_c, sb=sb_c, layer=1, group_heads=group_c)
    y2d, buf1 = _ffn(x2d, [mx.reshape(rows, -1)], w["w_out_c"],
                     w["norm_ffn"][1], w["w_ffn_in"][1], w["conv_w"][1], w["conv_b"][1], w["w_ffn_out"][1],
                     st["conv"][1], w["norm_final"], n_seq=n_seq, seq_len=seq_len, tm=tm)
    new_st = {"c": c_new, "n": n_new, "m": m_new, "r": r_new, "s": s_new, "conv": (buf0, buf1)}
    return y2d.reshape(n_seq, seq_len, d), new_st


def _lane_rep(v):
    return jnp.broadcast_to(v[..., None, None], v.shape + (1, LANES))


def kernel(x_prompt, x_sample, state_mlstm_C, state_mlstm_n, state_mlstm_m, state_ret_S, state_hgrn_S,
           state_ffn_conv, meta_tokens, norm_mix, w_in_ab, b_igate, b_fgate, gn_mlstm, gn_ret, w_out_ab,
           lb_logits, w_in_c, gn_hgrn, w_out_c, norm_ffn, w_ffn_in, conv_w, conv_b, w_ffn_out, norm_final):
    bp, seq, d = x_prompt.shape
    bs, dec_seq, _ = x_sample.shape
    hd = H_AB * HEAD_DIM
    n_hc = state_hgrn_S.shape[2]
    d_ff = w_ffn_out.shape[1]
    assert w_in_ab.shape[0] == 1 and w_in_c.shape[0] == 1 and norm_mix.shape[0] == 2

    wab = w_in_ab[0].astype(BF16)
    g0 = 4 * hd
    w_ab = jnp.concatenate([wab[:, :g0], wab[:, g0 + 2 * H_AB:], wab[:, g0:g0 + 2 * H_AB],
                            jnp.zeros((d, LANES - 2 * H_AB), BF16)], axis=1)
    wo_ab = w_out_ab[0].astype(BF16)
    log_gamma = jnp.log1p(-jnp.exp2(-5.0 - jnp.arange(H_AB, dtype=F32)))
    gate_bias = jnp.concatenate([b_igate[0], b_fgate[0], jnp.zeros((LANES - 2 * H_AB,), F32)])[None, :]
    w = {
        "norm_mix": norm_mix[:, None, :],
        "norm_ffn": norm_ffn[:, None, :],
        "norm_final": norm_final[None, :],
        "w_in_ab": w_ab,
        "gate_bias": gate_bias,
        "lg": _lane_rep(log_gamma),
        "gn_a": gn_mlstm[0][None, :],
        "gn_b": gn_ret[0][None, :],
        "w_out_ab": [wo_ab[:hd], wo_ab[hd:]],
        "lb_logits": lb_logits,
        "w_in_c": w_in_c[0].astype(BF16),
        "gn_c": gn_hgrn[0][None, :],
        "w_out_c": [w_out_c[0].astype(BF16)],
        "w_ffn_in": w_ffn_in.astype(BF16),
        "conv_w": conv_w,
        "conv_b": conv_b[:, None, :],
        "w_ffn_out": w_ffn_out.astype(BF16),
    }

    def zero_state(n):
        return {"c": jnp.zeros((n, H_AB, HEAD_DIM, HEAD_DIM), F32),
                "n": jnp.zeros((n, H_AB, 1, LANES), F32),
                "m": jnp.zeros((n, H_AB, 1, LANES), F32),
                "r": jnp.zeros((n, H_AB, HEAD_DIM, HEAD_DIM), F32),
                "s": jnp.zeros((n, n_hc, HEAD_DIM, HEAD_DIM), F32),
                "conv": (jnp.zeros((n, CONV_W - 1, d_ff), F32), jnp.zeros((n, CONV_W - 1, d_ff), F32))}

    _, st_meta = _trunk(meta_tokens[None].astype(F32), jnp.arange(N_META, dtype=F32), zero_state(1), w,
                        chunk_ab=N_META, chunk_c=N_META, sb_ab=1, sb_c=1, unroll_ab=1, group_c=4)
    pos_p = N_META + jnp.arange(seq, dtype=F32)
    y_prompt, st_p = _trunk(x_prompt, pos_p, st_meta, w, chunk_ab=128, chunk_c=128,
                            sb_ab=min(bp, 2), sb_c=min(bp, 2), unroll_ab=min(bp, 2), group_c=n_hc)
    st_s = {"c": state_mlstm_C[0], "n": state_mlstm_n[0][:, :, None, :], "m": _lane_rep(state_mlstm_m[0]),
            "r": state_ret_S[0], "s": state_hgrn_S[0], "conv": ((state_ffn_conv, 0), (state_ffn_conv, 1))}
    pos_s = PAST_LEN + jnp.arange(dec_seq, dtype=F32)
    y_sample, st_s = _trunk(x_sample, pos_s, st_s, w, chunk_ab=dec_seq, chunk_c=dec_seq,
                            sb_ab=min(bs, 16), sb_c=min(bs, 16), unroll_ab=min(bs, 2), group_c=n_hc)

    def outs(s):
        return (s["c"][None], s["n"][:, :, 0, :][None], s["m"][:, :, 0, 0][None], s["r"][None], s["s"][None],
                jnp.stack(s["conv"]))

    cp, n_p, mp, rp, sp, convp = outs(st_p)
    cs, n_s, ms, rs, ss, convs = outs(st_s)
    return (y_prompt, y_sample, cp, cs, n_p, n_s, mp, ms, rp, rs, sp, ss, convp, convs)
```
